```python
import math
import jax, jax.numpy as jnp
from jax import lax
import numpy as np

D_MODEL = 2048
BATCH = 8
SEQ = 2048
DEPTH = 2

CHUNK = 64
Q_BLOCK = 128

FOX_HEAD_DIM = 128
FOX_HEADS = D_MODEL // (2 * FOX_HEAD_DIM)
FOX_WIDTH = FOX_HEADS * FOX_HEAD_DIM
CONV_CHANNELS = D_MODEL - FOX_WIDTH
CONV_WIDTH = 31
IN0_WIDTH = 3 * FOX_WIDTH + FOX_HEADS + 2 * CONV_CHANNELS

GDN_HEAD_DIM = 128
GDN_QK_HEADS = D_MODEL // GDN_HEAD_DIM
GDN_V_HEADS = 2 * GDN_QK_HEADS
GDN_QK_WIDTH = GDN_QK_HEADS * GDN_HEAD_DIM
GDN_V_WIDTH = GDN_V_HEADS * GDN_HEAD_DIM
GDN_QKV_WIDTH = 2 * GDN_QK_WIDTH + GDN_V_WIDTH
SHORT_CONV = 4
IN1_WIDTH = GDN_QKV_WIDTH + GDN_V_WIDTH + 2 * GDN_V_HEADS

D_FF = -(-8 * D_MODEL // (3 * 256)) * 256

N_EVEN = (DEPTH + 1) // 2
N_ODD = DEPTH // 2
NORM_EPS = 1e-6

kernel_name = "hybrid_fox_conformer_gdn_trunk"


def rms_norm(x, g):
    xf = x.astype(jnp.float32)
    y = xf * lax.rsqrt(jnp.mean(xf * xf, axis=-1, keepdims=True) + NORM_EPS)
    return (y * g.astype(jnp.float32)).astype(x.dtype)


def layer_norm(x, g, b):
    xf = x.astype(jnp.float32)
    mu = jnp.mean(xf, axis=-1, keepdims=True)
    var = jnp.mean(jnp.square(xf - mu), axis=-1, keepdims=True)
    y = (xf - mu) * lax.rsqrt(var + NORM_EPS)
    return (y * g.astype(jnp.float32) + b.astype(jnp.float32)).astype(x.dtype)


def l2_normalize(x):
    xf = x.astype(jnp.float32)
    return xf * lax.rsqrt(jnp.sum(xf * xf, axis=-1, keepdims=True) + NORM_EPS)


def causal_depthwise_conv(x, w):
    width, ch = w.shape
    return lax.conv_general_dilated(
        x, w[:, None, :].astype(x.dtype), window_strides=(1,), padding=[(width - 1, 0)],
        dimension_numbers=("NWC", "WIO", "NWC"), feature_group_count=ch)


def forgetting_attention(q, k, v, log_f):
    seq = q.shape[1]
    scale = q.shape[-1] ** -0.5
    c = jnp.transpose(jnp.cumsum(log_f, axis=1), (0, 2, 1))
    outs = []
    for blk in range(seq // Q_BLOCK):
        q0 = blk * Q_BLOCK
        kend = q0 + Q_BLOCK
        s = jnp.einsum("bqhd,bkhd->bhqk", q[:, q0:kend], k[:, :kend]).astype(jnp.float32) * scale
        s = s + c[:, :, q0:kend, None] - c[:, :, None, :kend]
        qpos = q0 + jnp.arange(Q_BLOCK)
        mask = qpos[:, None] >= jnp.arange(kend)[None, :]
        s = jnp.where(mask, s, -jnp.inf)
        p = jax.nn.softmax(s, axis=-1)
        outs.append(jnp.einsum("bhqk,bkhd->bqhd", p.astype(v.dtype), v[:, :kend]))
    return jnp.concatenate(outs, axis=1)


def fox_conformer_mixer(h, w_in, f_bias, conv_w, conv_b, ln_g, ln_b, w_out):
    bsz, seq, _ = h.shape
    proj = h @ w_in
    splits = [FOX_WIDTH, 2 * FOX_WIDTH, 3 * FOX_WIDTH, 3 * FOX_WIDTH + FOX_HEADS,
              3 * FOX_WIDTH + FOX_HEADS + CONV_CHANNELS]
    q, k, v, f_logit, glu_val, glu_gate = jnp.split(proj, splits, axis=-1)
    hs = (bsz, seq, FOX_HEADS, FOX_HEAD_DIM)
    log_f = jax.nn.log_sigmoid(f_logit.astype(jnp.float32) + f_bias.astype(jnp.float32))
    a_out = forgetting_attention(q.reshape(hs), k.reshape(hs), v.reshape(hs), log_f)
    a_out = a_out.reshape(bsz, seq, FOX_WIDTH)
    u = glu_val * jax.nn.sigmoid(glu_gate)
    u = causal_depthwise_conv(u, conv_w) + conv_b.astype(u.dtype)
    u = jax.nn.silu(layer_norm(u, ln_g, ln_b))
    return jnp.concatenate([a_out, u], axis=-1) @ w_out


def chunk_gated_delta_rule(q, k, v, g, beta):
    bsz, seq, nh, dk = q.shape
    dv = v.shape[-1]
    n = seq // CHUNK

    def to_chunks(t):
        t = t.astype(jnp.float32).reshape((bsz, n, CHUNK, nh) + t.shape[3:])
        perm = (1, 0, 3, 2) + tuple(range(4, t.ndim))
        return jnp.transpose(t, perm)

    qc, kc, vc, gc, bc = to_chunks(q), to_chunks(k), to_chunks(v), to_chunks(g), to_chunks(beta)
    gcum = jnp.cumsum(gc, axis=-1)
    idx = jnp.arange(CHUNK)
    tril = idx[:, None] >= idx[None, :]
    strict = idx[:, None] > idx[None, :]
    decay = jnp.exp(jnp.where(tril, gcum[..., :, None] - gcum[..., None, :], -jnp.inf))
    kb = kc * bc[..., None]
    vb = vc * bc[..., None]
    lower = jnp.where(strict, jnp.einsum("nbhid,nbhjd->nbhij", kb, kc) * decay, 0.0)
    a_mat = lower + jnp.eye(CHUNK, dtype=jnp.float32)
    rhs = jnp.concatenate([vb, kb * jnp.exp(gcum)[..., None]], axis=-1)
    sol = lax.linalg.triangular_solve(a_mat, rhs, left_side=True, lower=True, unit_diagonal=True)
    u_c, w_c = sol[..., :dv], sol[..., dv:]

    def step(state, inp):
        qi, ki, ui, wi, gi, di = inp
        intra = jnp.einsum("bhid,bhjd->bhij", qi, ki) * di
        v_new = ui - jnp.einsum("bhck,bhkv->bhcv", wi, state)
        o = (jnp.einsum("bhck,bhkv->bhcv", qi * jnp.exp(gi)[..., None], state)
             + jnp.einsum("bhij,bhjv->bhiv", intra, v_new))
        g_last = gi[..., -1]
        k_dec = ki * jnp.exp(g_last[..., None] - gi)[..., None]
        state = state * jnp.exp(g_last)[..., None, None] + jnp.einsum("bhck,bhcv->bhkv", k_dec, v_new)
        return state, o

    state0 = jnp.zeros((bsz, nh, dk, dv), jnp.float32)
    _, out = lax.scan(step, state0, (qc, kc, u_c, w_c, gcum, decay))
    return jnp.transpose(out, (1, 0, 3, 2, 4)).reshape(bsz, seq, nh, dv)


def gated_deltanet_mixer(h, w_in, short_conv, a_log, dt_bias, o_norm, w_out):
    bsz, seq, _ = h.shape
    proj = h @ w_in
    splits = [GDN_QKV_WIDTH, GDN_QKV_WIDTH + GDN_V_WIDTH, GDN_QKV_WIDTH + GDN_V_WIDTH + GDN_V_HEADS]
    qkv, z, b_logit, a_logit = jnp.split(proj, splits, axis=-1)
    qkv = jax.nn.silu(causal_depthwise_conv(qkv, short_conv))
    q, k, v = jnp.split(qkv, [GDN_QK_WIDTH, 2 * GDN_QK_WIDTH], axis=-1)
    rep = GDN_V_HEADS // GDN_QK_HEADS
    q = l2_normalize(q.reshape(bsz, seq, GDN_QK_HEADS, GDN_HEAD_DIM)) * (GDN_HEAD_DIM ** -0.5)
    k = l2_normalize(k.reshape(bsz, seq, GDN_QK_HEADS, GDN_HEAD_DIM))
    q = jnp.repeat(q, rep, axis=2)
    k = jnp.repeat(k, rep, axis=2)
    v = v.reshape(bsz, seq, GDN_V_HEADS, GDN_HEAD_DIM)
    beta = jax.nn.sigmoid(b_logit.astype(jnp.float32))
    g = -jnp.exp(a_log.astype(jnp.float32)) * jax.nn.softplus(
        a_logit.astype(jnp.float32) + dt_bias.astype(jnp.float32))
    o = chunk_gated_delta_rule(q, k, v, g, beta)
    zf = z.reshape(bsz, seq, GDN_V_HEADS, GDN_HEAD_DIM).astype(jnp.float32)
    o = rms_norm(o, o_norm) * jax.nn.silu(zf)
    return o.reshape(bsz, seq, GDN_V_WIDTH).astype(h.dtype) @ w_out


def swiglu_ffn(h, w_gate, w_up, w_down):
    return (jax.nn.silu(h @ w_gate) * (h @ w_up)) @ w_down


def setup_inputs(seed: int = 0) -> dict:
    key = jax.random.key(seed)
    ks = jax.random.split(key, 24)
    f32 = jnp.float32

    def nrm(k, shape, scale):
        return jax.random.normal(k, shape, f32) * scale

    def gain(k, shape):
        return 1.0 + 0.02 * jax.random.normal(k, shape, f32)

    x = nrm(ks[0], (BATCH, SEQ, D_MODEL), 1.0)
    norm_mix0 = gain(ks[1], (N_EVEN, D_MODEL))
    w_in0 = nrm(ks[2], (N_EVEN, D_MODEL, IN0_WIDTH), D_MODEL ** -0.5)
    fgate_bias0 = jnp.linspace(1.0, 5.0, FOX_HEADS, dtype=f32)[None, :] + nrm(ks[3], (N_EVEN, FOX_HEADS), 0.1)
    conv_w0 = nrm(ks[4], (N_EVEN, CONV_WIDTH, CONV_CHANNELS), CONV_WIDTH ** -0.5)
    conv_b0 = nrm(ks[5], (N_EVEN, CONV_CHANNELS), 0.02)
    conv_ln_g0 = gain(ks[6], (N_EVEN, CONV_CHANNELS))
    conv_ln_b0 = nrm(ks[7], (N_EVEN, CONV_CHANNELS), 0.02)
    w_out0 = nrm(ks[8], (N_EVEN, D_MODEL, D_MODEL), D_MODEL ** -0.5)
    norm_mix1 = gain(ks[9], (N_ODD, D_MODEL))
    w_in1 = nrm(ks[10], (N_ODD, D_MODEL, IN1_WIDTH), D_MODEL ** -0.5)
    short_conv1 = nrm(ks[11], (N_ODD, SHORT_CONV, GDN_QKV_WIDTH), SHORT_CONV ** -0.5)
    a_log1 = jnp.log(jax.random.uniform(ks[12], (N_ODD, GDN_V_HEADS), f32, 1.0, 16.0))
    dt = jnp.exp(jax.random.uniform(ks[13], (N_ODD, GDN_V_HEADS), f32, math.log(1e-3), math.log(1e-1)))
    dt_bias1 = dt + jnp.log(-jnp.expm1(-dt))
    o_norm1 = gain(ks[14], (N_ODD, GDN_HEAD_DIM))
    w_out1 = nrm(ks[15], (N_ODD, GDN_V_WIDTH, D_MODEL), GDN_V_WIDTH ** -0.5)
    norm_ffn = gain(ks[16], (DEPTH, D_MODEL))
    w_gate = nrm(ks[17], (DEPTH, D_MODEL, D_FF), D_MODEL ** -0.5)
    w_up = nrm(ks[18], (DEPTH, D_MODEL, D_FF), D_MODEL ** -0.5)
    w_down = nrm(ks[19], (DEPTH, D_FF, D_MODEL), D_FF ** -0.5)
    final_norm = gain(ks[20], (D_MODEL,))
    return {"x": x, "norm_mix0": norm_mix0, "w_in0": w_in0, "fgate_bias0": fgate_bias0,
            "conv_w0": conv_w0, "conv_b0": conv_b0, "conv_ln_g0": conv_ln_g0, "conv_ln_b0": conv_ln_b0,
            "w_out0": w_out0, "norm_mix1": norm_mix1, "w_in1": w_in1, "short_conv1": short_conv1,
            "a_log1": a_log1, "dt_bias1": dt_bias1, "o_norm1": o_norm1, "w_out1": w_out1,
            "norm_ffn": norm_ffn, "w_gate": w_gate, "w_up": w_up, "w_down": w_down,
            "final_norm": final_norm}


def reference(x, norm_mix0, w_in0, fgate_bias0, conv_w0, conv_b0, conv_ln_g0, conv_ln_b0, w_out0,
              norm_mix1, w_in1, short_conv1, a_log1, dt_bias1, o_norm1, w_out1,
              norm_ffn, w_gate, w_up, w_down, final_norm):
    h = x
    for layer in range(DEPTH):
        i = layer // 2
        if layer % 2 == 0:
            h = h + fox_conformer_mixer(rms_norm(h, norm_mix0[i]), w_in0[i], fgate_bias0[i], conv_w0[i],
                                        conv_b0[i], conv_ln_g0[i], conv_ln_b0[i], w_out0[i])
        else:
            h = h + gated_deltanet_mixer(rms_norm(h, norm_mix1[i]), w_in1[i], short_conv1[i], a_log1[i],
                                         dt_bias1[i], o_norm1[i], w_out1[i])
        h = h + swiglu_ffn(rms_norm(h, norm_ffn[layer]), w_gate[layer], w_up[layer], w_down[layer])
    return rms_norm(h, final_norm)
```

```python
import functools

import jax
import jax.numpy as jnp
from jax import lax
from jax.experimental import pallas as pl
from jax.experimental.pallas import tpu as pltpu

F32 = jnp.float32
BF16 = jnp.bfloat16

NORM_EPS = 1e-6
LANES = 128
HEAD_DIM = 128
CHUNK = 64
CONV_WIDTH = 31
CONV_HALO = 32
SHORT_CONV = 4
SHORT_HALO = 8
VMEM_LIMIT_BYTES = 56 * 1024 * 1024


def _params(*sem):
    return pltpu.CompilerParams(dimension_semantics=sem, vmem_limit_bytes=VMEM_LIMIT_BYTES)


def _blk(dim, pref, align):
    best = dim
    for cand in range(align, min(dim, pref) + 1, align):
        if dim % cand == 0:
            best = cand
    return best


def _sigmoid(x):
    return 1.0 / (1.0 + jnp.exp(-x))


def _softplus(x):
    return jnp.maximum(x, 0.0) + jnp.log(1.0 + jnp.exp(-jnp.abs(x)))


def _rms_to_scratch(x_ref, g_ref, xn_ref):
    x = x_ref[...].astype(F32)
    ms = jnp.mean(x * x, axis=-1, keepdims=True)
    xn_ref[...] = (x * lax.rsqrt(ms + NORM_EPS) * g_ref[...]).astype(xn_ref.dtype)


def _norm_mm_kernel(x_ref, g_ref, w_ref, o_ref, xn_ref):
    @pl.when(pl.program_id(1) == 0)
    def _():
        _rms_to_scratch(x_ref, g_ref, xn_ref)

    o_ref[...] = jnp.dot(xn_ref[...], w_ref[...], preferred_element_type=F32).astype(o_ref.dtype)


def norm_matmul(x, gain, w, out_dtype, bm, bn):
    m, k = x.shape
    n = w.shape[1]
    bm, bn = _blk(m, bm, 8), _blk(n, bn, LANES)
    return pl.pallas_call(
        _norm_mm_kernel,
        grid=(m // bm, n // bn),
        in_specs=[pl.BlockSpec((bm, k), lambda i, j: (i, 0)),
                  pl.BlockSpec((1, k), lambda i, j: (0, 0)),
                  pl.BlockSpec((k, bn), lambda i, j: (0, j))],
        out_specs=pl.BlockSpec((bm, bn), lambda i, j: (i, j)),
        out_shape=jax.ShapeDtypeStruct((m, n), out_dtype),
        scratch_shapes=[pltpu.VMEM((bm, k), BF16)],
        compiler_params=_params("parallel", "arbitrary"),
        name="norm_matmul",
    )(x, gain, w)


def _norm_swiglu_kernel(x_ref, g_ref, wg_ref, wu_ref, o_ref, xn_ref):
    @pl.when(pl.program_id(1) == 0)
    def _():
        _rms_to_scratch(x_ref, g_ref, xn_ref)

    xn = xn_ref[...]
    gate = jnp.dot(xn, wg_ref[...], preferred_element_type=F32)
    up = jnp.dot(xn, wu_ref[...], preferred_element_type=F32)
    o_ref[...] = (gate * _sigmoid(gate) * up).astype(o_ref.dtype)


def norm_swiglu(x, gain, wg, wu, bm, bn):
    m, k = x.shape
    n = wg.shape[1]
    bm, bn = _blk(m, bm, 8), _blk(n, bn, LANES)
    return pl.pallas_call(
        _norm_swiglu_kernel,
        grid=(m // bm, n // bn),
        in_specs=[pl.BlockSpec((bm, k), lambda i, j: (i, 0)),
                  pl.BlockSpec((1, k), lambda i, j: (0, 0)),
                  pl.BlockSpec((k, bn), lambda i, j: (0, j)),
                  pl.BlockSpec((k, bn), lambda i, j: (0, j))],
        out_specs=pl.BlockSpec((bm, bn), lambda i, j: (i, j)),
        out_shape=jax.ShapeDtypeStruct((m, n), BF16),
        scratch_shapes=[pltpu.VMEM((bm, k), BF16)],
        compiler_params=_params("parallel", "arbitrary"),
        name="norm_swiglu",
    )(x, gain, wg, wu)


def _mm_res_kernel(*refs, n_pairs):
    xs = refs[:n_pairs]
    ws = refs[n_pairs:2 * n_pairs]
    res_ref = refs[2 * n_pairs]
    o_ref = refs[2 * n_pairs + 1]
    acc = res_ref[...]
    for x_ref, w_ref in zip(xs, ws):
        acc = acc + jnp.dot(x_ref[...], w_ref[...], preferred_element_type=F32)
    o_ref[...] = acc


def matmul_residual(xs, ws, res, bm, bn):
    m, n = res.shape
    bm, bn = _blk(m, bm, 8), _blk(n, bn, LANES)
    n_pairs = len(xs)
    in_specs = ([pl.BlockSpec((bm, x.shape[1]), lambda j, i: (i, 0)) for x in xs]
                + [pl.BlockSpec((w.shape[0], bn), lambda j, i: (0, j)) for w in ws]
                + [pl.BlockSpec((bm, bn), lambda j, i: (i, j))])
    return pl.pallas_call(
        functools.partial(_mm_res_kernel, n_pairs=n_pairs),
        grid=(n // bn, m // bm),
        in_specs=in_specs,
        out_specs=pl.BlockSpec((bm, bn), lambda j, i: (i, j)),
        out_shape=jax.ShapeDtypeStruct((m, n), F32),
        compiler_params=_params("parallel", "parallel"),
        name="matmul_residual",
    )(*xs, *ws, res)


def _rmsnorm_kernel(x_ref, g_ref, o_ref):
    x = x_ref[...]
    ms = jnp.mean(x * x, axis=-1, keepdims=True)
    o_ref[...] = x * lax.rsqrt(ms + NORM_EPS) * g_ref[...]


def rmsnorm(x, gain, bm):
    m, k = x.shape
    bm = _blk(m, bm, 8)
    return pl.pallas_call(
        _rmsnorm_kernel,
        grid=(m // bm,),
        in_specs=[pl.BlockSpec((bm, k), lambda i: (i, 0)), pl.BlockSpec((1, k), lambda i: (0, 0))],
        out_specs=pl.BlockSpec((bm, k), lambda i: (i, 0)),
        out_shape=jax.ShapeDtypeStruct((m, k), F32),
        compiler_params=_params("parallel"),
        name="final_rmsnorm",
    )(x, gain)


def _prefix_sum_rows(x, period):
    row = lax.broadcasted_iota(jnp.int32, x.shape, 0) % period
    shift = 1
    while shift < period:
        x = x + jnp.where(row >= shift, pltpu.roll(x, shift, axis=0), 0.0)
        shift *= 2
    return x


def _fox_gates_kernel(f_ref, b_ref, c_ref):
    z = f_ref[...] + b_ref[...]
    log_f = jnp.minimum(z, 0.0) - jnp.log(1.0 + jnp.exp(-jnp.abs(z)))
    c_ref[...] = _prefix_sum_rows(log_f, log_f.shape[0])


def fox_gates(f_logit, f_bias, batch, seq):
    return pl.pallas_call(
        _fox_gates_kernel,
        grid=(batch,),
        in_specs=[pl.BlockSpec((seq, LANES), lambda b: (b, 0)), pl.BlockSpec((1, LANES), lambda b: (0, 0))],
        out_specs=pl.BlockSpec((seq, LANES), lambda b: (b, 0)),
        out_shape=jax.ShapeDtypeStruct((batch * seq, LANES), F32),
        compiler_params=_params("parallel"),
        name="fox_gates",
    )(f_logit, f_bias)


def _fox_attn_kernel(q_ref, k_ref, v_ref, cq_ref, ck_ref, o_ref, *, tq, scale):
    i = pl.program_id(2)
    q = q_ref[...]
    cq = cq_ref[0, 0]
    qpos = i * tq + lax.broadcasted_iota(jnp.int32, (tq, tq), 0)
    kofs = lax.broadcasted_iota(jnp.int32, (tq, tq), 1)

    def body(j, carry):
        m, l, acc = carry
        start = pl.multiple_of(j * tq, tq)
        ks = k_ref[pl.ds(start, tq), :]
        vs = v_ref[pl.ds(start, tq), :]
        ck = ck_ref[0, 0, pl.ds(j, 1), :]
        s = lax.dot_general(q, ks, (((1,), (1,)), ((), ())), preferred_element_type=F32)
        s = s * scale + cq - ck
        s = jnp.where(qpos >= j * tq + kofs, s, -jnp.inf)
        m_new = jnp.maximum(m, jnp.max(s, axis=-1, keepdims=True))
        alpha = jnp.exp(m - m_new)
        p = jnp.exp(s - m_new)
        l = alpha * l + jnp.sum(p, axis=-1, keepdims=True)
        acc = alpha * acc + jnp.dot(p.astype(BF16), vs, preferred_element_type=F32)
        return m_new, l, acc

    d = q.shape[-1]
    init = (jnp.full((tq, 1), -jnp.inf, F32), jnp.zeros((tq, 1), F32), jnp.zeros((tq, d), F32))
    _, l, acc = lax.fori_loop(0, i + 1, body, init)
    o_ref[...] = (acc / l).astype(o_ref.dtype)


def fox_attention(qkv, c_col, c_row, batch, seq, heads, tq):
    nq = seq // tq
    d = HEAD_DIM
    kernel = functools.partial(_fox_attn_kernel, tq=tq, scale=d ** -0.5)
    return pl.pallas_call(
        kernel,
        grid=(batch, heads, nq),
        in_specs=[pl.BlockSpec((tq, d), lambda b, h, i: (b * nq + i, h)),
                  pl.BlockSpec((seq, d), lambda b, h, i: (b, heads + h)),
                  pl.BlockSpec((seq, d), lambda b, h, i: (b, 2 * heads + h)),
                  pl.BlockSpec((1, 1, tq, 1), lambda b, h, i: (b, h, i, 0)),
                  pl.BlockSpec((1, 1, nq, tq), lambda b, h, i: (b, h, 0, 0))],
        out_specs=pl.BlockSpec((tq, d), lambda b, h, i: (b * nq + i, h)),
        out_shape=jax.ShapeDtypeStruct((batch * seq, heads * d), BF16),
        compiler_params=_params("parallel", "parallel", "arbitrary"),
        name="fox_attention",
    )(qkv, qkv, qkv, c_col, c_row)


def _conformer_kernel(val_ref, gate_ref, w_ref, cb_ref, lg_ref, lb_ref, o_ref, hist_ref, y_ref, *, ts, rows):
    channels = val_ref.shape[1]

    @pl.when(pl.program_id(1) == 0)
    def _():
        hist_ref[0:CONV_HALO, :] = jnp.zeros((CONV_HALO, channels), F32)

    hist_ref[CONV_HALO:CONV_HALO + ts, :] = val_ref[...].astype(F32) * _sigmoid(gate_ref[...].astype(F32))

    base = CONV_HALO - (CONV_WIDTH - 1)
    for cb in range(channels // LANES):
        lanes = slice(cb * LANES, (cb + 1) * LANES)
        for rb in range(ts // rows):
            acc = jnp.zeros((rows, LANES), F32)
            for j in range(CONV_WIDTH):
                r0 = rb * rows + base + j
                acc = acc + w_ref[j:j + 1, lanes] * hist_ref[r0:r0 + rows, lanes]
            y_ref[rb * rows:(rb + 1) * rows, lanes] = acc + cb_ref[:, lanes]

    hist_ref[0:CONV_HALO, :] = hist_ref[ts:ts + CONV_HALO, :]

    y = y_ref[...]
    mu = jnp.mean(y, axis=-1, keepdims=True)
    yc = y - mu
    var = jnp.mean(yc * yc, axis=-1, keepdims=True)
    yn = yc * lax.rsqrt(var + NORM_EPS) * lg_ref[...] + lb_ref[...]
    o_ref[...] = (yn * _sigmoid(yn)).astype(o_ref.dtype)


def conformer_conv(glu, conv_w, conv_b, ln_g, ln_b, batch, seq, ts):
    channels = glu.shape[1] // 2
    ns = seq // ts
    kernel = functools.partial(_conformer_kernel, ts=ts, rows=64)
    vec = lambda: pl.BlockSpec((1, channels), lambda b, i: (0, 0))
    return pl.pallas_call(
        kernel,
        grid=(batch, ns),
        in_specs=[pl.BlockSpec((ts, channels), lambda b, i: (b * ns + i, 0)),
                  pl.BlockSpec((ts, channels), lambda b, i: (b * ns + i, 1)),
                  pl.BlockSpec((conv_w.shape[0], channels), lambda b, i: (0, 0)),
                  vec(), vec(), vec()],
        out_specs=pl.BlockSpec((ts, channels), lambda b, i: (b * ns + i, 0)),
        out_shape=jax.ShapeDtypeStruct((batch * seq, channels), BF16),
        scratch_shapes=[pltpu.VMEM((ts + CONV_HALO, channels), F32), pltpu.VMEM((ts, channels), F32)],
        compiler_params=_params("arbitrary", "arbitrary"),
        name="conformer_conv",
    )(glu, glu, conv_w, conv_b, ln_g, ln_b)


def _gdn_prep_kernel(x_ref, ba_ref, w_ref, alog_ref, dtb_ref, q_ref, k_ref, v_ref, g_ref, hist_ref,
                     *, ts, qk_width, gate_heads):
    width = x_ref.shape[1]

    @pl.when(pl.program_id(1) == 0)
    def _():
        hist_ref[0:SHORT_HALO, :] = jnp.zeros((SHORT_HALO, width), F32)

    hist_ref[SHORT_HALO:SHORT_HALO + ts, :] = x_ref[...].astype(F32)
    base = SHORT_HALO - (SHORT_CONV - 1)
    n_qk_heads = qk_width // HEAD_DIM
    for hb in range(width // HEAD_DIM):
        lanes = slice(hb * HEAD_DIM, (hb + 1) * HEAD_DIM)
        acc = jnp.zeros((ts, HEAD_DIM), F32)
        for j in range(SHORT_CONV):
            acc = acc + w_ref[j:j + 1, lanes] * hist_ref[base + j:base + j + ts, lanes]
        y = acc * _sigmoid(acc)
        if hb < 2 * n_qk_heads:
            y = y * lax.rsqrt(jnp.sum(y * y, axis=-1, keepdims=True) + NORM_EPS)
            if hb < n_qk_heads:
                q_ref[:, lanes] = (y * HEAD_DIM ** -0.5).astype(q_ref.dtype)
            else:
                k_ref[:, (hb - n_qk_heads) * HEAD_DIM:(hb - n_qk_heads + 1) * HEAD_DIM] = y.astype(k_ref.dtype)
        else:
            c0 = (hb - 2 * n_qk_heads) * HEAD_DIM
            v_ref[:, c0:c0 + HEAD_DIM] = y.astype(v_ref.dtype)
    hist_ref[0:SHORT_HALO, :] = hist_ref[ts:ts + SHORT_HALO, :]

    ba = ba_ref[...]
    beta = _sigmoid(ba)
    g = -jnp.exp(alog_ref[...]) * _softplus(ba + dtb_ref[...])
    gcum = _prefix_sum_rows(g, CHUNK)
    lane = lax.broadcasted_iota(jnp.int32, ba.shape, 1)
    g_ref[...] = jnp.where(lane < gate_heads, beta, gcum)


def gdn_prep(qkv, ba, short_conv, alog_pad, dtb_pad, batch, seq, ts, qk_width, v_width, gate_heads):
    width = qkv.shape[1]
    ns = seq // ts
    m = batch * seq
    kernel = functools.partial(_gdn_prep_kernel, ts=ts, qk_width=qk_width, gate_heads=gate_heads)
    row = lambda b, i: (b * ns + i, 0)
    const = lambda b, i: (0, 0)
    return pl.pallas_call(
        kernel,
        grid=(batch, ns),
        in_specs=[pl.BlockSpec((ts, width), row),
                  pl.BlockSpec((ts, LANES), row),
                  pl.BlockSpec((short_conv.shape[0], width), const),
                  pl.BlockSpec((1, LANES), const),
                  pl.BlockSpec((1, LANES), const)],
        out_specs=[pl.BlockSpec((ts, qk_width), row), pl.BlockSpec((ts, qk_width), row),
                   pl.BlockSpec((ts, v_width), row), pl.BlockSpec((ts, LANES), row)],
        out_shape=[jax.ShapeDtypeStruct((m, qk_width), BF16), jax.ShapeDtypeStruct((m, qk_width), BF16),
                   jax.ShapeDtypeStruct((m, v_width), BF16), jax.ShapeDtypeStruct((m, LANES), F32)],
        scratch_shapes=[pltpu.VMEM((ts + SHORT_HALO, width), F32)],
        compiler_params=_params("arbitrary", "arbitrary"),
        name="gdn_prep",
    )(qkv, ba, short_conv, alog_pad, dtb_pad)


def _unit_lower_inverse(low):
    c = low.shape[0]
    eye = (lax.broadcasted_iota(jnp.int32, (c, c), 0) == lax.broadcasted_iota(jnp.int32, (c, c), 1)).astype(F32)
    inv = eye - low
    power = low
    span = 2
    while span <= c // 2:
        pb = power.astype(BF16)
        power = jnp.dot(pb, pb, preferred_element_type=F32)
        inv = inv + jnp.dot(inv.astype(BF16), power.astype(BF16), preferred_element_type=F32)
        span *= 2
    return inv


def _gdn_chunk_kernel(q_ref, k_ref, v_ref, z_ref, g_ref, on_ref, o_ref, state_ref, *, qk_heads, rep):
    c = CHUNK
    d = HEAD_DIM

    @pl.when(pl.program_id(1) == 0)
    def _():
        state_ref[...] = jnp.zeros(state_ref.shape, F32)

    gates = g_ref[...]
    gates_t = gates.T
    n_v = qk_heads * rep
    ri = lax.broadcasted_iota(jnp.int32, (c, c), 0)
    ci = lax.broadcasted_iota(jnp.int32, (c, c), 1)
    tril = ri >= ci
    strict = ri > ci
    gain = on_ref[...]

    for hq in range(qk_heads):
        lanes = slice(hq * d, (hq + 1) * d)
        q = q_ref[:, lanes]
        k = k_ref[:, lanes]
        kt = k.astype(F32).T.astype(BF16)
        kk = jnp.dot(k, kt, preferred_element_type=F32)
        qk = jnp.dot(q, kt, preferred_element_type=F32)
        for r in range(rep):
            hv = hq * rep + r
            vl = slice(hv * d, (hv + 1) * d)
            beta = gates[:, hv:hv + 1]
            gc = gates[:, n_v + hv:n_v + hv + 1]
            gr = gates_t[n_v + hv:n_v + hv + 1, :]
            decay = jnp.exp(jnp.where(tril, gc - gr, -jnp.inf))
            low = jnp.where(strict, beta * kk * decay, 0.0)
            t_inv = _unit_lower_inverse(low)
            eg = jnp.exp(gc)
            state = state_ref[hv]
            sb = state.astype(BF16)
            ks = jnp.dot(k, sb, preferred_element_type=F32) * eg
            rhs = beta * (v_ref[:, vl].astype(F32) - ks)
            v_new = jnp.dot(t_inv.astype(BF16), rhs.astype(BF16), preferred_element_type=F32)
            vnb = v_new.astype(BF16)
            o = (jnp.dot(q, sb, preferred_element_type=F32) * eg
                 + jnp.dot((qk * decay).astype(BF16), vnb, preferred_element_type=F32))
            g_last = gc[c - 1:c, :]
            v_dec = (jnp.exp(g_last - gc) * v_new).astype(BF16)
            state_ref[hv] = state * jnp.exp(g_last) + jnp.dot(kt, v_dec, preferred_element_type=F32)
            on = o * lax.rsqrt(jnp.mean(o * o, axis=-1, keepdims=True) + NORM_EPS) * gain
            zf = z_ref[:, vl].astype(F32)
            o_ref[:, vl] = (on * (zf * _sigmoid(zf))).astype(o_ref.dtype)


def gdn_chunk(q, k, v, z, gates, o_norm, batch, seq, qk_heads, rep):
    m = batch * seq
    nc = seq // CHUNK
    qk_width = qk_heads * HEAD_DIM
    v_width = qk_width * rep
    kernel = functools.partial(_gdn_chunk_kernel, qk_heads=qk_heads, rep=rep)
    row = lambda b, i: (b * nc + i, 0)
    return pl.pallas_call(
        kernel,
        grid=(batch, nc),
        in_specs=[pl.BlockSpec((CHUNK, qk_width), row), pl.BlockSpec((CHUNK, qk_width), row),
                  pl.BlockSpec((CHUNK, v_width), row), pl.BlockSpec((CHUNK, v_width), row),
                  pl.BlockSpec((CHUNK, LANES), row),
                  pl.BlockSpec((1, HEAD_DIM), lambda b, i: (0, 0))],
        out_specs=pl.BlockSpec((CHUNK, v_width), row),
        out_shape=jax.ShapeDtypeStruct((m, v_width), BF16),
        scratch_shapes=[pltpu.VMEM((qk_heads * rep, HEAD_DIM, HEAD_DIM), F32)],
        compiler_params=_params("arbitrary", "arbitrary"),
        name="gdn_chunk",
    )(q, k, v, z, gates, o_norm)


def _pad_cols(w, n):
    return jnp.pad(w, ((0, 0), (0, n - w.shape[1])))


def _row(v):
    return v.reshape(1, -1).astype(F32)


def kernel(x, norm_mix0, w_in0, fgate_bias0, conv_w0, conv_b0, conv_ln_g0, conv_ln_b0, w_out0, norm_mix1, w_in1,
           short_conv1, a_log1, dt_bias1, o_norm1, w_out1, norm_ffn, w_gate, w_up, w_down, final_norm):
    batch, seq, d_model = x.shape
    m = batch * seq
    fox_heads = fgate_bias0.shape[1]
    fox_width = fox_heads * HEAD_DIM
    conv_ch = conv_w0.shape[2]
    gdn_v_heads = a_log1.shape[1]
    gdn_v_width = gdn_v_heads * HEAD_DIM
    gdn_qkv_width = short_conv1.shape[2]
    gdn_qk_width = (gdn_qkv_width - gdn_v_width) // 2
    gdn_qk_heads = gdn_qk_width // HEAD_DIM
    rep = gdn_v_heads // gdn_qk_heads

    h = x.reshape(m, d_model)

    def ffn(h, layer):
        mid = norm_swiglu(h, _row(norm_ffn[layer]), w_gate[layer].astype(BF16), w_up[layer].astype(BF16),
                          bm=1024, bn=512)
        return matmul_residual([mid], [w_down[layer].astype(BF16)], h, bm=512, bn=1024)

    w0 = w_in0[0]
    g0 = _row(norm_mix0[0])
    qkv = norm_matmul(h, g0, w0[:, :3 * fox_width].astype(BF16), BF16, bm=1024, bn=1024)
    f_logit = norm_matmul(h, g0, _pad_cols(w0[:, 3 * fox_width:3 * fox_width + fox_heads], LANES).astype(BF16),
                          F32, bm=1024, bn=LANES)
    glu = norm_matmul(h, g0, w0[:, 3 * fox_width + fox_heads:].astype(BF16), BF16, bm=1024, bn=1024)

    tq = 256
    c = fox_gates(f_logit, _pad_cols(_row(fgate_bias0[0]), LANES), batch, seq)
    c_heads = jnp.transpose(c.reshape(batch, seq, LANES)[:, :, :fox_heads], (0, 2, 1))
    a_out = fox_attention(qkv, c_heads.reshape(batch, fox_heads, seq, 1),
                          c_heads.reshape(batch, fox_heads, seq // tq, tq), batch, seq, fox_heads, tq)
    u = conformer_conv(glu, jnp.pad(conv_w0[0], ((0, CONV_HALO - CONV_WIDTH), (0, 0))), _row(conv_b0[0]),
                       _row(conv_ln_g0[0]), _row(conv_ln_b0[0]), batch, seq, ts=256)
    wo0 = w_out0[0].astype(BF16)
    h = matmul_residual([a_out, u], [wo0[:fox_width], wo0[fox_width:]], h, bm=512, bn=1024)
    h = ffn(h, 0)

    w1 = w_in1[0]
    g1 = _row(norm_mix1[0])
    qkv1 = norm_matmul(h, g1, w1[:, :gdn_qkv_width].astype(BF16), BF16, bm=1024, bn=1024)
    z1 = norm_matmul(h, g1, w1[:, gdn_qkv_width:gdn_qkv_width + gdn_v_width].astype(BF16), BF16, bm=1024, bn=1024)
    ba1 = norm_matmul(h, g1, _pad_cols(w1[:, gdn_qkv_width + gdn_v_width:], LANES).astype(BF16), F32,
                      bm=1024, bn=LANES)
    alog_pad = jnp.pad(_row(a_log1[0]), ((0, 0), (gdn_v_heads, LANES - 2 * gdn_v_heads)))
    dtb_pad = jnp.pad(_row(dt_bias1[0]), ((0, 0), (gdn_v_heads, LANES - 2 * gdn_v_heads)))
    q1, k1, v1, gates1 = gdn_prep(qkv1, ba1, jnp.pad(short_conv1[0], ((0, 8 - SHORT_CONV), (0, 0))), alog_pad,
                                  dtb_pad, batch, seq, 256, gdn_qk_width, gdn_v_width, gdn_v_heads)
    o1 = gdn_chunk(q1, k1, v1, z1, gates1, _row(o_norm1[0]), batch, seq, gdn_qk_heads, rep)
    h = matmul_residual([o1], [w_out1[0].astype(BF16)], h, bm=512, bn=1024)
    h = ffn(h, 1)

    return rmsnorm(h, _row(final_norm), bm=512).reshape(batch, seq, d_model)
```

```python
import functools

import jax
import jax.numpy as jnp
from jax import lax
from jax.experimental import pallas as pl
from jax.experimental.pallas import tpu as pltpu

F32 = jnp.float32
BF16 = jnp.bfloat16

NORM_EPS = 1e-6
LANES = 128
MXU_TILE = 256
HEAD_DIM = 128
CHUNK = 64
CONV_WIDTH = 31
CONV_HALO = 32
SHORT_CONV = 4
SHORT_HALO = 8
VMEM_LIMIT_BYTES = 56 * 1024 * 1024


def _params(*sem):
    return pltpu.CompilerParams(dimension_semantics=sem, vmem_limit_bytes=VMEM_LIMIT_BYTES)


def _blk(dim, pref, align):
    best = dim
    for cand in range(align, min(dim, pref) + 1, align):
        if dim % cand == 0:
            best = cand
    return best


def _sigmoid(x):
    return 1.0 / (1.0 + jnp.exp(-x))


def _softplus(x):
    return jnp.maximum(x, 0.0) + jnp.log(1.0 + jnp.exp(-jnp.abs(x)))


def _rms_to_scratch(x_ref, g_ref, xn_ref):
    x = x_ref[...].astype(F32)
    ms = jnp.mean(x * x, axis=-1, keepdims=True)
    xn_ref[...] = (x * lax.rsqrt(ms + NORM_EPS) * g_ref[...]).astype(xn_ref.dtype)


def _norm_mm_kernel(x_ref, g_ref, w_ref, o_ref, xn_ref):
    @pl.when(pl.program_id(1) == 0)
    def _():
        _rms_to_scratch(x_ref, g_ref, xn_ref)

    o_ref[...] = jnp.dot(xn_ref[...], w_ref[...], preferred_element_type=F32).astype(o_ref.dtype)


def norm_matmul(x, gain, w, out_dtype, bm, bn):
    m, k = x.shape
    n = w.shape[1]
    bm, bn = _blk(m, bm, 8), _blk(n, bn, LANES)
    return pl.pallas_call(
        _norm_mm_kernel,
        grid=(m // bm, n // bn),
        in_specs=[pl.BlockSpec((bm, k), lambda i, j: (i, 0)),
                  pl.BlockSpec((1, k), lambda i, j: (0, 0)),
                  pl.BlockSpec((k, bn), lambda i, j: (0, j))],
        out_specs=pl.BlockSpec((bm, bn), lambda i, j: (i, j)),
        out_shape=jax.ShapeDtypeStruct((m, n), out_dtype),
        scratch_shapes=[pltpu.VMEM((bm, k), BF16)],
        compiler_params=_params("parallel", "arbitrary"),
        name="norm_matmul",
    )(x, gain, w)


def _norm_swiglu_kernel(x_ref, g_ref, wg_ref, wu_ref, o_ref, xn_ref):
    @pl.when(pl.program_id(1) == 0)
    def _():
        _rms_to_scratch(x_ref, g_ref, xn_ref)

    xn = xn_ref[...]
    gate = jnp.dot(xn, wg_ref[...], preferred_element_type=F32)
    up = jnp.dot(xn, wu_ref[...], preferred_element_type=F32)
    o_ref[...] = (gate * _sigmoid(gate) * up).astype(o_ref.dtype)


def norm_swiglu(x, gain, wg, wu, bm, bn):
    m, k = x.shape
    n = wg.shape[1]
    bm, bn = _blk(m, bm, 8), _blk(n, bn, LANES)
    return pl.pallas_call(
        _norm_swiglu_kernel,
        grid=(m // bm, n // bn),
        in_specs=[pl.BlockSpec((bm, k), lambda i, j: (i, 0)),
                  pl.BlockSpec((1, k), lambda i, j: (0, 0)),
                  pl.BlockSpec((k, bn), lambda i, j: (0, j)),
                  pl.BlockSpec((k, bn), lambda i, j: (0, j))],
        out_specs=pl.BlockSpec((bm, bn), lambda i, j: (i, j)),
        out_shape=jax.ShapeDtypeStruct((m, n), BF16),
        scratch_shapes=[pltpu.VMEM((bm, k), BF16)],
        compiler_params=_params("parallel", "arbitrary"),
        name="norm_swiglu",
    )(x, gain, wg, wu)


def _mm_res_kernel(*refs, n_pairs):
    xs = refs[:n_pairs]
    ws = refs[n_pairs:2 * n_pairs]
    res_ref = refs[2 * n_pairs]
    o_ref = refs[2 * n_pairs + 1]
    acc = res_ref[...]
    for x_ref, w_ref in zip(xs, ws):
        acc = acc + jnp.dot(x_ref[...], w_ref[...], preferred_element_type=F32)
    o_ref[...] = acc


def matmul_residual(xs, ws, res, bm, bn):
    m, n = res.shape
    bm, bn = _blk(m, bm, 8), _blk(n, bn, LANES)
    n_pairs = len(xs)
    in_specs = ([pl.BlockSpec((bm, x.shape[1]), lambda j, i: (i, 0)) for x in xs]
                + [pl.BlockSpec((w.shape[0], bn), lambda j, i: (0, j)) for w in ws]
                + [pl.BlockSpec((bm, bn), lambda j, i: (i, j))])
    return pl.pallas_call(
        functools.partial(_mm_res_kernel, n_pairs=n_pairs),
        grid=(n // bn, m // bm),
        in_specs=in_specs,
        out_specs=pl.BlockSpec((bm, bn), lambda j, i: (i, j)),
        out_shape=jax.ShapeDtypeStruct((m, n), F32),
        compiler_params=_params("parallel", "parallel"),
        name="matmul_residual",
    )(*xs, *ws, res)


def _rmsnorm_kernel(x_ref, g_ref, o_ref):
    x = x_ref[...]
    ms = jnp.mean(x * x, axis=-1, keepdims=True)
    o_ref[...] = x * lax.rsqrt(ms + NORM_EPS) * g_ref[...]


def rmsnorm(x, gain, bm):
    m, k = x.shape
    bm = _blk(m, bm, 8)
    return pl.pallas_call(
        _rmsnorm_kernel,
        grid=(m // bm,),
        in_specs=[pl.BlockSpec((bm, k), lambda i: (i, 0)), pl.BlockSpec((1, k), lambda i: (0, 0))],
        out_specs=pl.BlockSpec((bm, k), lambda i: (i, 0)),
        out_shape=jax.ShapeDtypeStruct((m, k), F32),
        compiler_params=_params("parallel"),
        name="final_rmsnorm",
    )(x, gain)


def _prefix_sum_rows(x, period):
    row = lax.broadcasted_iota(jnp.int32, x.shape, 0) % period
    shift = 1
    while shift < period:
        x = x + jnp.where(row >= shift, pltpu.roll(x, shift, axis=0), 0.0)
        shift *= 2
    return x


def _fox_gates_kernel(f_ref, b_ref, c_ref):
    z = f_ref[...] + b_ref[...]
    log_f = jnp.minimum(z, 0.0) - jnp.log(1.0 + jnp.exp(-jnp.abs(z)))
    c_ref[...] = _prefix_sum_rows(log_f, log_f.shape[0])


def fox_gates(f_logit, f_bias, batch, seq):
    return pl.pallas_call(
        _fox_gates_kernel,
        grid=(batch,),
        in_specs=[pl.BlockSpec((seq, LANES), lambda b: (b, 0)), pl.BlockSpec((1, LANES), lambda b: (0, 0))],
        out_specs=pl.BlockSpec((seq, LANES), lambda b: (b, 0)),
        out_shape=jax.ShapeDtypeStruct((batch * seq, LANES), F32),
        compiler_params=_params("parallel"),
        name="fox_gates",
    )(f_logit, f_bias)


def _fox_attn_kernel(q_ref, k_ref, v_ref, cq_ref, ck_ref, o_ref, *, tq, scale):
    i = pl.program_id(2)
    q = q_ref[...]
    cq = cq_ref[0, 0]
    qpos = i * tq + lax.broadcasted_iota(jnp.int32, (tq, tq), 0)
    kofs = lax.broadcasted_iota(jnp.int32, (tq, tq), 1)

    def body(j, carry):
        m, l, acc = carry
        start = pl.multiple_of(j * tq, tq)
        ks = k_ref[pl.ds(start, tq), :]
        vs = v_ref[pl.ds(start, tq), :]
        ck = ck_ref[0, 0, pl.ds(j, 1), :]
        s = lax.dot_general(q, ks, (((1,), (1,)), ((), ())), preferred_element_type=F32)
        s = s * scale + cq - ck
        s = jnp.where(qpos >= j * tq + kofs, s, -jnp.inf)
        m_new = jnp.maximum(m, jnp.max(s, axis=-1, keepdims=True))
        alpha = jnp.exp(m - m_new)
        p = jnp.exp(s - m_new)
        l = alpha * l + jnp.sum(p, axis=-1, keepdims=True)
        acc = alpha * acc + jnp.dot(p.astype(BF16), vs, preferred_element_type=F32)
        return m_new, l, acc

    d = q.shape[-1]
    init = (jnp.full((tq, 1), -jnp.inf, F32), jnp.zeros((tq, 1), F32), jnp.zeros((tq, d), F32))
    _, l, acc = lax.fori_loop(0, i + 1, body, init)
    o_ref[...] = (acc / l).astype(o_ref.dtype)


def fox_attention(qkv, c_col, c_row, batch, seq, heads, tq):
    nq = seq // tq
    d = HEAD_DIM
    kernel = functools.partial(_fox_attn_kernel, tq=tq, scale=d ** -0.5)
    return pl.pallas_call(
        kernel,
        grid=(batch, heads, nq),
        in_specs=[pl.BlockSpec((tq, d), lambda b, h, i: (b * nq + i, h)),
                  pl.BlockSpec((seq, d), lambda b, h, i: (b, heads + h)),
                  pl.BlockSpec((seq, d), lambda b, h, i: (b, 2 * heads + h)),
                  pl.BlockSpec((1, 1, tq, 1), lambda b, h, i: (b, h, i, 0)),
                  pl.BlockSpec((1, 1, nq, tq), lambda b, h, i: (b, h, 0, 0))],
        out_specs=pl.BlockSpec((tq, d), lambda b, h, i: (b * nq + i, h)),
        out_shape=jax.ShapeDtypeStruct((batch * seq, heads * d), BF16),
        compiler_params=_params("parallel", "parallel", "arbitrary"),
        name="fox_attention",
    )(qkv, qkv, qkv, c_col, c_row)


def _conformer_kernel(val_ref, gate_ref, w_ref, cb_ref, lg_ref, lb_ref, o_ref, hist_ref, y_ref, *, ts, rows):
    channels = val_ref.shape[1]

    @pl.when(pl.program_id(1) == 0)
    def _():
        hist_ref[0:CONV_HALO, :] = jnp.zeros((CONV_HALO, channels), F32)

    hist_ref[CONV_HALO:CONV_HALO + ts, :] = val_ref[...].astype(F32) * _sigmoid(gate_ref[...].astype(F32))

    base = CONV_HALO - (CONV_WIDTH - 1)
    for cb in range(channels // LANES):
        lanes = slice(cb * LANES, (cb + 1) * LANES)
        for rb in range(ts // rows):
            acc = jnp.zeros((rows, LANES), F32)
            for j in range(CONV_WIDTH):
                r0 = rb * rows + base + j
                acc = acc + w_ref[j:j + 1, lanes] * hist_ref[r0:r0 + rows, lanes]
            y_ref[rb * rows:(rb + 1) * rows, lanes] = acc + cb_ref[:, lanes]

    hist_ref[0:CONV_HALO, :] = hist_ref[ts:ts + CONV_HALO, :]

    y = y_ref[...]
    mu = jnp.mean(y, axis=-1, keepdims=True)
    yc = y - mu
    var = jnp.mean(yc * yc, axis=-1, keepdims=True)
    yn = yc * lax.rsqrt(var + NORM_EPS) * lg_ref[...] + lb_ref[...]
    o_ref[...] = (yn * _sigmoid(yn)).astype(o_ref.dtype)


def conformer_conv(glu, conv_w, conv_b, ln_g, ln_b, batch, seq, ts):
    channels = glu.shape[1] // 2
    ns = seq // ts
    kernel = functools.partial(_conformer_kernel, ts=ts, rows=64)
    vec = lambda: pl.BlockSpec((1, channels), lambda b, i: (0, 0))
    return pl.pallas_call(
        kernel,
        grid=(batch, ns),
        in_specs=[pl.BlockSpec((ts, channels), lambda b, i: (b * ns + i, 0)),
                  pl.BlockSpec((ts, channels), lambda b, i: (b * ns + i, 1)),
                  pl.BlockSpec((conv_w.shape[0], channels), lambda b, i: (0, 0)),
                  vec(), vec(), vec()],
        out_specs=pl.BlockSpec((ts, channels), lambda b, i: (b * ns + i, 0)),
        out_shape=jax.ShapeDtypeStruct((batch * seq, channels), BF16),
        scratch_shapes=[pltpu.VMEM((ts + CONV_HALO, channels), F32), pltpu.VMEM((ts, channels), F32)],
        compiler_params=_params("arbitrary", "arbitrary"),
        name="conformer_conv",
    )(glu, glu, conv_w, conv_b, ln_g, ln_b)


def _gdn_prep_kernel(x_ref, ba_ref, w_ref, alog_ref, dtb_ref, q_ref, k_ref, v_ref, g_ref, hist_ref,
                     *, ts, qk_width, gate_heads):
    width = x_ref.shape[1]

    @pl.when(pl.program_id(1) == 0)
    def _():
        hist_ref[0:SHORT_HALO, :] = jnp.zeros((SHORT_HALO, width), F32)

    hist_ref[SHORT_HALO:SHORT_HALO + ts, :] = x_ref[...].astype(F32)
    base = SHORT_HALO - (SHORT_CONV - 1)
    n_qk_heads = qk_width // HEAD_DIM
    for hb in range(width // HEAD_DIM):
        lanes = slice(hb * HEAD_DIM, (hb + 1) * HEAD_DIM)
        acc = jnp.zeros((ts, HEAD_DIM), F32)
        for j in range(SHORT_CONV):
            acc = acc + w_ref[j:j + 1, lanes] * hist_ref[base + j:base + j + ts, lanes]
        y = acc * _sigmoid(acc)
        if hb < 2 * n_qk_heads:
            y = y * lax.rsqrt(jnp.sum(y * y, axis=-1, keepdims=True) + NORM_EPS)
            if hb < n_qk_heads:
                q_ref[:, lanes] = (y * HEAD_DIM ** -0.5).astype(q_ref.dtype)
            else:
                k_ref[:, (hb - n_qk_heads) * HEAD_DIM:(hb - n_qk_heads + 1) * HEAD_DIM] = y.astype(k_ref.dtype)
        else:
            c0 = (hb - 2 * n_qk_heads) * HEAD_DIM
            v_ref[:, c0:c0 + HEAD_DIM] = y.astype(v_ref.dtype)
    hist_ref[0:SHORT_HALO, :] = hist_ref[ts:ts + SHORT_HALO, :]

    ba = ba_ref[...]
    beta = _sigmoid(ba)
    g = -jnp.exp(alog_ref[...]) * _softplus(ba + dtb_ref[...])
    gcum = _prefix_sum_rows(g, CHUNK)
    lane = lax.broadcasted_iota(jnp.int32, ba.shape, 1)
    g_ref[...] = jnp.where(lane < gate_heads, beta, gcum)


def gdn_prep(qkv, ba, short_conv, alog_pad, dtb_pad, batch, seq, ts, qk_width, v_width, gate_heads):
    width = qkv.shape[1]
    ns = seq // ts
    m = batch * seq
    kernel = functools.partial(_gdn_prep_kernel, ts=ts, qk_width=qk_width, gate_heads=gate_heads)
    row = lambda b, i: (b * ns + i, 0)
    const = lambda b, i: (0, 0)
    return pl.pallas_call(
        kernel,
        grid=(batch, ns),
        in_specs=[pl.BlockSpec((ts, width), row),
                  pl.BlockSpec((ts, LANES), row),
                  pl.BlockSpec((short_conv.shape[0], width), const),
                  pl.BlockSpec((1, LANES), const),
                  pl.BlockSpec((1, LANES), const)],
        out_specs=[pl.BlockSpec((ts, qk_width), row), pl.BlockSpec((ts, qk_width), row),
                   pl.BlockSpec((ts, v_width), row), pl.BlockSpec((ts, LANES), row)],
        out_shape=[jax.ShapeDtypeStruct((m, qk_width), BF16), jax.ShapeDtypeStruct((m, qk_width), BF16),
                   jax.ShapeDtypeStruct((m, v_width), BF16), jax.ShapeDtypeStruct((m, LANES), F32)],
        scratch_shapes=[pltpu.VMEM((ts + SHORT_HALO, width), F32)],
        compiler_params=_params("arbitrary", "arbitrary"),
        name="gdn_prep",
    )(qkv, ba, short_conv, alog_pad, dtb_pad)


GROUP = MXU_TILE // CHUNK


def _gdn_solve_kernel(q_ref, k_ref, g_ref, gr_ref, bdm_ref, kkm_ref, t_ref, p_ref, *, v_heads, rep):
    c = CHUNK
    gw = MXU_TILE
    n_groups = v_heads // GROUP
    gates = g_ref[...]
    row = lax.broadcasted_iota(jnp.int32, (c, gw), 0)
    lane = lax.broadcasted_iota(jnp.int32, (c, gw), 1)
    blk = lane // c
    col = lane % c
    tril = row >= col
    strict = row > col
    eye = (row == col).astype(F32)
    bd_mask = bdm_ref[...]
    kk_mask = kkm_ref[...]

    def block_diag(x):
        xb = x.astype(BF16)
        return jnp.concatenate([xb] * GROUP, axis=0) * bd_mask

    def per_head_columns(first_lane, g):
        out = gates[:, first_lane + GROUP * g + GROUP - 1:first_lane + GROUP * g + GROUP]
        for mth in range(GROUP - 2, -1, -1):
            cm = gates[:, first_lane + GROUP * g + mth:first_lane + GROUP * g + mth + 1]
            out = jnp.where(blk == mth, cm, out)
        return out

    groups = range(n_groups)
    lanes = [slice(g * gw, (g + 1) * gw) for g in groups]
    kq = [jnp.concatenate([k_ref[:, lanes[g]], q_ref[:, lanes[g]]], axis=0) for g in groups]
    rhs = [jnp.concatenate([k_ref[:, lanes[g]]] * GROUP, axis=0) * kk_mask for g in groups]
    kkqk = [lax.dot_general(kq[g], rhs[g], (((1,), (1,)), ((), ())), preferred_element_type=F32) for g in groups]
    decay = [jnp.exp(jnp.where(tril, per_head_columns(v_heads, g) - gr_ref[0, :, lanes[g]], -jnp.inf))
             for g in groups]
    low = [jnp.where(strict, per_head_columns(0, g) * kkqk[g][:c] * decay[g], 0.0) for g in groups]
    for g in groups:
        p_ref[:, lanes[g]] = (kkqk[g][c:] * decay[g]).astype(p_ref.dtype)

    inv = [eye - low[g] for g in groups]
    bd = [block_diag(low[g]) for g in groups]
    power = [jnp.dot(low[g].astype(BF16), bd[g], preferred_element_type=F32) for g in groups]
    span = 2
    while span <= c // 2:
        bd = [block_diag(power[g]) for g in groups]
        if span < c // 2:
            both = [jnp.dot(jnp.concatenate([power[g], inv[g]], axis=0).astype(BF16), bd[g],
                            preferred_element_type=F32) for g in groups]
            power = [both[g][:c] for g in groups]
            inv = [inv[g] + both[g][c:] for g in groups]
        else:
            inv = [inv[g] + jnp.dot(inv[g].astype(BF16), bd[g], preferred_element_type=F32) for g in groups]
        span *= 2
    for g in groups:
        t_ref[:, lanes[g]] = inv[g].astype(t_ref.dtype)


def gdn_solve(q, k, gates, g_rows, v_heads, rep):
    m, qk_width = q.shape
    width = v_heads * CHUNK
    assert GROUP * CHUNK == MXU_TILE and rep * HEAD_DIM == MXU_TILE and v_heads % GROUP == 0
    assert qk_width * rep == v_heads * HEAD_DIM and qk_width == width
    rb = lax.broadcasted_iota(jnp.int32, (GROUP * CHUNK, MXU_TILE), 0) // CHUNK
    ln = lax.broadcasted_iota(jnp.int32, (GROUP * CHUNK, MXU_TILE), 1)
    bd_mask = (rb == ln // CHUNK).astype(BF16)
    kk_mask = (rb // rep == ln // HEAD_DIM).astype(BF16)
    kernel = functools.partial(_gdn_solve_kernel, v_heads=v_heads, rep=rep)
    row = lambda i: (i, 0)
    const = lambda i: (0, 0)
    return pl.pallas_call(
        kernel,
        grid=(m // CHUNK,),
        in_specs=[pl.BlockSpec((CHUNK, qk_width), row), pl.BlockSpec((CHUNK, qk_width), row),
                  pl.BlockSpec((CHUNK, LANES), row),
                  pl.BlockSpec((1, 1, width), lambda i: (i, 0, 0)),
                  pl.BlockSpec((GROUP * CHUNK, MXU_TILE), const), pl.BlockSpec((GROUP * CHUNK, MXU_TILE), const)],
        out_specs=[pl.BlockSpec((CHUNK, width), row), pl.BlockSpec((CHUNK, width), row)],
        out_shape=[jax.ShapeDtypeStruct((m, width), BF16), jax.ShapeDtypeStruct((m, width), BF16)],
        compiler_params=_params("parallel"),
        name="gdn_solve",
    )(q, k, gates, g_rows, bd_mask, kk_mask)


def _gdn_chunk_kernel(q_ref, k_ref, v_ref, z_ref, g_ref, t_ref, p_ref, on_ref, o_ref, state_ref, *, qk_heads, rep):
    c = CHUNK
    d = HEAD_DIM

    @pl.when(pl.program_id(1) == 0)
    def _():
        state_ref[...] = jnp.zeros(state_ref.shape, F32)

    gates = g_ref[...]
    n_v = qk_heads * rep
    pw = rep * d
    gain = on_ref[...]
    zeros = jnp.zeros((c, d), BF16)
    pairs = range(qk_heads)

    def on_diagonal(parts):
        return jnp.concatenate(
            [jnp.concatenate([parts[r] if s == r else zeros for s in range(rep)], axis=1) for r in range(rep)],
            axis=0)

    def head_cols(first_lane, p):
        return [gates[:, first_lane + p * rep + r:first_lane + p * rep + r + 1] for r in range(rep)]

    k = [k_ref[:, p * d:(p + 1) * d] for p in pairs]
    kq = [jnp.concatenate([k[p], q_ref[:, p * d:(p + 1) * d]], axis=0) for p in pairs]
    state = [state_ref[p] for p in pairs]
    kqs = [jnp.dot(kq[p], state[p].astype(BF16), preferred_element_type=F32) for p in pairs]

    beta = [head_cols(0, p) for p in pairs]
    gc = [head_cols(n_v, p) for p in pairs]
    eg = [[jnp.exp(g) for g in gc[p]] for p in pairs]
    rhs = [on_diagonal([(beta[p][r] * (v_ref[:, p * pw + r * d:p * pw + (r + 1) * d].astype(F32)
                                       - kqs[p][:c, r * d:(r + 1) * d] * eg[p][r])).astype(BF16)
                        for r in range(rep)]) for p in pairs]
    tw = rep * c
    v_new = [jnp.dot(t_ref[:, p * tw:(p + 1) * tw], rhs[p], preferred_element_type=F32) for p in pairs]
    vn_diag = [on_diagonal([v_new[p][:, r * d:(r + 1) * d].astype(BF16) for r in range(rep)]) for p in pairs]
    intra = [jnp.dot(p_ref[:, p * tw:(p + 1) * tw], vn_diag[p], preferred_element_type=F32) for p in pairs]

    g_last = [[g[c - 1:c, :] for g in gc[p]] for p in pairs]
    v_dec = [jnp.concatenate([jnp.exp(g_last[p][r] - gc[p][r]) * v_new[p][:, r * d:(r + 1) * d]
                              for r in range(rep)], axis=1).astype(BF16) for p in pairs]
    kt = [k[p].astype(F32).T.astype(BF16) for p in pairs]
    upd = [jnp.dot(kt[p], v_dec[p], preferred_element_type=F32) for p in pairs]
    for p in pairs:
        scale = jnp.concatenate([jnp.broadcast_to(jnp.exp(g_last[p][r]), (1, d)) for r in range(rep)], axis=1)
        state_ref[p] = state[p] * scale + upd[p]

    for p in pairs:
        for r in range(rep):
            vl = slice(p * pw + r * d, p * pw + (r + 1) * d)
            o = kqs[p][c:, r * d:(r + 1) * d] * eg[p][r] + intra[p][:, r * d:(r + 1) * d]
            on = o * lax.rsqrt(jnp.mean(o * o, axis=-1, keepdims=True) + NORM_EPS) * gain
            zf = z_ref[:, vl].astype(F32)
            o_ref[:, vl] = (on * (zf * _sigmoid(zf))).astype(o_ref.dtype)


def gdn_chunk(q, k, v, z, gates, t_inv, p_intra, o_norm, batch, seq, qk_heads, rep):
    m = batch * seq
    nc = seq // CHUNK
    qk_width = qk_heads * HEAD_DIM
    v_width = qk_width * rep
    kernel = functools.partial(_gdn_chunk_kernel, qk_heads=qk_heads, rep=rep)
    row = lambda b, i: (b * nc + i, 0)
    return pl.pallas_call(
        kernel,
        grid=(batch, nc),
        in_specs=[pl.BlockSpec((CHUNK, qk_width), row), pl.BlockSpec((CHUNK, qk_width), row),
                  pl.BlockSpec((CHUNK, v_width), row), pl.BlockSpec((CHUNK, v_width), row),
                  pl.BlockSpec((CHUNK, LANES), row),
                  pl.BlockSpec((CHUNK, t_inv.shape[1]), row), pl.BlockSpec((CHUNK, p_intra.shape[1]), row),
                  pl.BlockSpec((1, HEAD_DIM), lambda b, i: (0, 0))],
        out_specs=pl.BlockSpec((CHUNK, v_width), row),
        out_shape=jax.ShapeDtypeStruct((m, v_width), BF16),
        scratch_shapes=[pltpu.VMEM((qk_heads, HEAD_DIM, rep * HEAD_DIM), F32)],
        compiler_params=_params("arbitrary", "arbitrary"),
        name="gdn_chunk",
    )(q, k, v, z, gates, t_inv, p_intra, o_norm)


def _pad_cols(w, n):
    return jnp.pad(w, ((0, 0), (0, n - w.shape[1])))


def _row(v):
    return v.reshape(1, -1).astype(F32)


def kernel(x, norm_mix0, w_in0, fgate_bias0, conv_w0, conv_b0, conv_ln_g0, conv_ln_b0, w_out0, norm_mix1, w_in1,
           short_conv1, a_log1, dt_bias1, o_norm1, w_out1, norm_ffn, w_gate, w_up, w_down, final_norm):
    batch, seq, d_model = x.shape
    m = batch * seq
    fox_heads = fgate_bias0.shape[1]
    fox_width = fox_heads * HEAD_DIM
    conv_ch = conv_w0.shape[2]
    gdn_v_heads = a_log1.shape[1]
    gdn_v_width = gdn_v_heads * HEAD_DIM
    gdn_qkv_width = short_conv1.shape[2]
    gdn_qk_width = (gdn_qkv_width - gdn_v_width) // 2
    gdn_qk_heads = gdn_qk_width // HEAD_DIM
    rep = gdn_v_heads // gdn_qk_heads

    h = x.reshape(m, d_model)

    def ffn(h, layer):
        mid = norm_swiglu(h, _row(norm_ffn[layer]), w_gate[layer].astype(BF16), w_up[layer].astype(BF16),
                          bm=1024, bn=512)
        return matmul_residual([mid], [w_down[layer].astype(BF16)], h, bm=512, bn=1024)

    w0 = w_in0[0]
    g0 = _row(norm_mix0[0])
    qkv = norm_matmul(h, g0, w0[:, :3 * fox_width].astype(BF16), BF16, bm=1024, bn=1024)
    f_logit = norm_matmul(h, g0, _pad_cols(w0[:, 3 * fox_width:3 * fox_width + fox_heads], LANES).astype(BF16),
                          F32, bm=1024, bn=LANES)
    glu = norm_matmul(h, g0, w0[:, 3 * fox_width + fox_heads:].astype(BF16), BF16, bm=1024, bn=1024)

    tq = 256
    c = fox_gates(f_logit, _pad_cols(_row(fgate_bias0[0]), LANES), batch, seq)
    c_heads = jnp.transpose(c.reshape(batch, seq, LANES)[:, :, :fox_heads], (0, 2, 1))
    a_out = fox_attention(qkv, c_heads.reshape(batch, fox_heads, seq, 1),
                          c_heads.reshape(batch, fox_heads, seq // tq, tq), batch, seq, fox_heads, tq)
    u = conformer_conv(glu, jnp.pad(conv_w0[0], ((0, CONV_HALO - CONV_WIDTH), (0, 0))), _row(conv_b0[0]),
                       _row(conv_ln_g0[0]), _row(conv_ln_b0[0]), batch, seq, ts=256)
    wo0 = w_out0[0].astype(BF16)
    h = matmul_residual([a_out, u], [wo0[:fox_width], wo0[fox_width:]], h, bm=512, bn=1024)
    h = ffn(h, 0)

    w1 = w_in1[0]
    g1 = _row(norm_mix1[0])
    qkv1 = norm_matmul(h, g1, w1[:, :gdn_qkv_width].astype(BF16), BF16, bm=1024, bn=1024)
    z1 = norm_matmul(h, g1, w1[:, gdn_qkv_width:gdn_qkv_width + gdn_v_width].astype(BF16), BF16, bm=1024, bn=1024)
    ba1 = norm_matmul(h, g1, _pad_cols(w1[:, gdn_qkv_width + gdn_v_width:], LANES).astype(BF16), F32,
                      bm=1024, bn=LANES)
    alog_pad = jnp.pad(_row(a_log1[0]), ((0, 0), (gdn_v_heads, LANES - 2 * gdn_v_heads)))
    dtb_pad = jnp.pad(_row(dt_bias1[0]), ((0, 0), (gdn_v_heads, LANES - 2 * gdn_v_heads)))
    q1, k1, v1, gates1 = gdn_prep(qkv1, ba1, jnp.pad(short_conv1[0], ((0, 8 - SHORT_CONV), (0, 0))), alog_pad,
                                  dtb_pad, batch, seq, 256, gdn_qk_width, gdn_v_width, gdn_v_heads)
    g_rows = jnp.transpose(gates1[:, gdn_v_heads:2 * gdn_v_heads].reshape(m // CHUNK, CHUNK, gdn_v_heads),
                           (0, 2, 1)).reshape(m // CHUNK, 1, gdn_v_heads * CHUNK)
    t_inv, p_intra = gdn_solve(q1, k1, gates1, g_rows, gdn_v_heads, rep)
    o1 = gdn_chunk(q1, k1, v1, z1, gates1, t_inv, p_intra, _row(o_norm1[0]), batch, seq, gdn_qk_heads, rep)
    h = matmul_residual([o1], [w_out1[0].astype(BF16)], h, bm=512, bn=1024)
    h = ffn(h, 1)

    return rmsnorm(h, _row(final_norm), bm=512).reshape(batch, seq, d_model)
```

```python
import functools
from typing import NamedTuple, Optional

import jax
import jax.numpy as jnp
from jax import lax
from jax.experimental import pallas as pl
from jax.experimental.pallas import tpu as pltpu

F32 = jnp.float32
BF16 = jnp.bfloat16

NORM_EPS = 1e-6
LANES = 128
SUBLANES = 8
MXU_TILE = 256
HEAD_DIM = 128
CHUNK = 64
CONV_WIDTH = 31
CONV_HALO = 32
SHORT_CONV = 4
SHORT_HALO = 8
VMEM_LIMIT_BYTES = 56 * 1024 * 1024


def _params(*sem):
    return pltpu.CompilerParams(dimension_semantics=sem, vmem_limit_bytes=VMEM_LIMIT_BYTES)


def _blk(dim, pref, align):
    best = dim
    for cand in range(align, min(dim, pref) + 1, align):
        if dim % cand == 0:
            best = cand
    return best


def _sigmoid(x):
    return 1.0 / (1.0 + jnp.exp(-x))


def _softplus(x):
    return jnp.maximum(x, 0.0) + jnp.log(1.0 + jnp.exp(-jnp.abs(x)))


def _rms_to_scratch(x_ref, g_ref, xn_ref):
    x = x_ref[...].astype(F32)
    ms = jnp.mean(x * x, axis=-1, keepdims=True)
    xn_ref[...] = (x * lax.rsqrt(ms + NORM_EPS) * g_ref[...]).astype(xn_ref.dtype)


class WeightView(NamedTuple):
    arr: jax.Array
    lead: Optional[int]
    row0: int
    rows: int
    col0: int
    cols: int


def wview(arr, lead=None, row0=0, rows=None, col0=0, cols=None):
    k, n = arr.shape[-2:]
    return WeightView(arr, lead, row0, k - row0 if rows is None else rows, col0, n - col0 if cols is None else cols)


def _w_spec(wv, bn, col_block):
    assert wv.row0 % wv.rows == 0 and wv.col0 % bn == 0 and wv.cols % bn == 0
    rb, cb = wv.row0 // wv.rows, wv.col0 // bn
    if wv.arr.ndim == 3:
        return pl.BlockSpec((None, wv.rows, bn), lambda *g: (wv.lead, rb, cb + col_block(*g)))
    return pl.BlockSpec((wv.rows, bn), lambda *g: (rb, cb + col_block(*g)))


def _norm_mm_kernel(x_ref, g_ref, w_ref, o_ref, xn_ref):
    @pl.when(pl.program_id(1) == 0)
    def _():
        _rms_to_scratch(x_ref, g_ref, xn_ref)

    o_ref[...] = jnp.dot(xn_ref[...], w_ref[...].astype(BF16), preferred_element_type=F32).astype(o_ref.dtype)


def norm_matmul(x, gain, wv, out_dtype, bm, bn):
    m, k = x.shape
    n = wv.cols
    bm, bn = _blk(m, bm, 8), _blk(n, bn, LANES)
    return pl.pallas_call(
        _norm_mm_kernel,
        grid=(m // bm, n // bn),
        in_specs=[pl.BlockSpec((bm, k), lambda i, j: (i, 0)),
                  pl.BlockSpec((1, k), lambda i, j: (0, 0)),
                  _w_spec(wv, bn, lambda i, j: j)],
        out_specs=pl.BlockSpec((bm, bn), lambda i, j: (i, j)),
        out_shape=jax.ShapeDtypeStruct((m, n), out_dtype),
        scratch_shapes=[pltpu.VMEM((bm, k), BF16)],
        compiler_params=_params("parallel", "arbitrary"),
        name="norm_matmul",
    )(x, gain, wv.arr)


def _norm_swiglu_kernel(x_ref, g_ref, wg_ref, wu_ref, o_ref, xn_ref):
    @pl.when(pl.program_id(1) == 0)
    def _():
        _rms_to_scratch(x_ref, g_ref, xn_ref)

    xn = xn_ref[...]
    gate = jnp.dot(xn, wg_ref[...].astype(BF16), preferred_element_type=F32)
    up = jnp.dot(xn, wu_ref[...].astype(BF16), preferred_element_type=F32)
    o_ref[...] = (gate * _sigmoid(gate) * up).astype(o_ref.dtype)


def norm_swiglu(x, gain, wg, wu, bm, bn):
    m, k = x.shape
    n = wg.cols
    bm, bn = _blk(m, bm, 8), _blk(n, bn, LANES)
    return pl.pallas_call(
        _norm_swiglu_kernel,
        grid=(m // bm, n // bn),
        in_specs=[pl.BlockSpec((bm, k), lambda i, j: (i, 0)),
                  pl.BlockSpec((1, k), lambda i, j: (0, 0)),
                  _w_spec(wg, bn, lambda i, j: j),
                  _w_spec(wu, bn, lambda i, j: j)],
        out_specs=pl.BlockSpec((bm, bn), lambda i, j: (i, j)),
        out_shape=jax.ShapeDtypeStruct((m, n), BF16),
        scratch_shapes=[pltpu.VMEM((bm, k), BF16)],
        compiler_params=_params("parallel", "arbitrary"),
        name="norm_swiglu",
    )(x, gain, wg.arr, wu.arr)


def _mm_res_kernel(*refs, n_pairs):
    xs = refs[:n_pairs]
    ws = refs[n_pairs:2 * n_pairs]
    res_ref = refs[2 * n_pairs]
    o_ref = refs[2 * n_pairs + 1]
    wb = refs[2 * n_pairs + 2:]

    @pl.when(pl.program_id(1) == 0)
    def _():
        for w_ref, wb_ref in zip([w for w in ws if w.dtype != BF16], wb):
            wb_ref[...] = w_ref[...].astype(BF16)

    wb = list(wb)
    acc = res_ref[...]
    for x_ref, w_ref in zip(xs, ws):
        w = w_ref[...] if w_ref.dtype == BF16 else wb.pop(0)[...]
        acc = acc + jnp.dot(x_ref[...], w, preferred_element_type=F32)
    o_ref[...] = acc


def matmul_residual(xs, wvs, res, bm, bn):
    m, n = res.shape
    bm, bn = _blk(m, bm, 8), _blk(n, bn, LANES)
    n_pairs = len(xs)
    in_specs = ([pl.BlockSpec((bm, x.shape[1]), lambda j, i: (i, 0)) for x in xs]
                + [_w_spec(wv, bn, lambda j, i: j) for wv in wvs]
                + [pl.BlockSpec((bm, bn), lambda j, i: (i, j))])
    return pl.pallas_call(
        functools.partial(_mm_res_kernel, n_pairs=n_pairs),
        grid=(n // bn, m // bm),
        in_specs=in_specs,
        out_specs=pl.BlockSpec((bm, bn), lambda j, i: (i, j)),
        out_shape=jax.ShapeDtypeStruct((m, n), F32),
        scratch_shapes=[pltpu.VMEM((wv.rows, bn), BF16) for wv in wvs if wv.arr.dtype != BF16],
        compiler_params=_params("parallel", "arbitrary"),
        name="matmul_residual",
    )(*xs, *[wv.arr for wv in wvs], res)


def _rmsnorm_kernel(x_ref, g_ref, o_ref):
    x = x_ref[...]
    ms = jnp.mean(x * x, axis=-1, keepdims=True)
    o_ref[...] = x * lax.rsqrt(ms + NORM_EPS) * g_ref[...]


def rmsnorm(x, gain, bm):
    m, k = x.shape
    bm = _blk(m, bm, 8)
    return pl.pallas_call(
        _rmsnorm_kernel,
        grid=(m // bm,),
        in_specs=[pl.BlockSpec((bm, k), lambda i: (i, 0)), pl.BlockSpec((1, k), lambda i: (0, 0))],
        out_specs=pl.BlockSpec((bm, k), lambda i: (i, 0)),
        out_shape=jax.ShapeDtypeStruct((m, k), F32),
        compiler_params=_params("parallel"),
        name="final_rmsnorm",
    )(x, gain)


def _prefix_sum_rows(x, period):
    row = lax.broadcasted_iota(jnp.int32, x.shape, 0) % period
    shift = 1
    while shift < period:
        x = x + jnp.where(row >= shift, pltpu.roll(x, shift, axis=0), 0.0)
        shift *= 2
    return x


def _fox_gates_kernel(f_ref, b_ref, c_ref):
    z = f_ref[...] + b_ref[...]
    log_f = jnp.minimum(z, 0.0) - jnp.log(1.0 + jnp.exp(-jnp.abs(z)))
    c_ref[...] = _prefix_sum_rows(log_f, log_f.shape[0])


def fox_gates(f_logit, f_bias, batch, seq):
    return pl.pallas_call(
        _fox_gates_kernel,
        grid=(batch,),
        in_specs=[pl.BlockSpec((seq, LANES), lambda b: (b, 0)), pl.BlockSpec((1, LANES), lambda b: (0, 0))],
        out_specs=pl.BlockSpec((seq, LANES), lambda b: (b, 0)),
        out_shape=jax.ShapeDtypeStruct((batch * seq, LANES), F32),
        compiler_params=_params("parallel"),
        name="fox_gates",
    )(f_logit, f_bias)


def _fox_attn_kernel(q_ref, k_ref, vt_ref, cq_ref, ck_ref, o_ref, m_ref, l_ref, acc_ref, *, tq, hb, scale):
    i = pl.program_id(2)
    d = HEAD_DIM
    heads = range(hb)
    q = [q_ref[:, h * d:(h + 1) * d] for h in heads]
    cq = [cq_ref[0, h] for h in heads]
    for h in heads:
        m_ref[h] = jnp.full((1, tq), -jnp.inf, F32)
        l_ref[h] = jnp.zeros((1, tq), F32)
        acc_ref[h] = jnp.zeros((d, tq), F32)
    key_after_query = (lax.broadcasted_iota(jnp.int32, (tq, tq), 0) > lax.broadcasted_iota(jnp.int32, (tq, tq), 1))

    def block(j, on_diagonal):
        start = pl.multiple_of(j * tq, tq)
        s = [lax.dot_general(k_ref[pl.ds(start, tq), h * d:(h + 1) * d], q[h], (((1,), (1,)), ((), ())),
                             preferred_element_type=F32) for h in heads]
        x = [s[h] * scale + (cq[h] - ck_ref[0, h, pl.ds(start, tq), :]) for h in heads]
        if on_diagonal:
            x = [jnp.where(key_after_query, -jnp.inf, x[h]) for h in heads]
        m_old = [m_ref[h] for h in heads]
        m_new = [jnp.maximum(m_old[h], jnp.max(x[h], axis=0, keepdims=True)) for h in heads]
        alpha = [jnp.exp(m_old[h] - m_new[h]) for h in heads]
        p = [jnp.exp(x[h] - m_new[h]) for h in heads]
        pv = [jnp.dot(vt_ref[0, j, h * d:(h + 1) * d, :], p[h].astype(BF16), preferred_element_type=F32)
              for h in heads]
        for h in heads:
            m_ref[h] = m_new[h]
            l_ref[h] = alpha[h] * l_ref[h] + jnp.sum(p[h], axis=0, keepdims=True)
            acc_ref[h] = alpha[h] * acc_ref[h] + pv[h]

    def body(j, carry):
        block(j, False)
        return carry

    lax.fori_loop(0, i, body, 0)
    block(i, True)
    for h in heads:
        o_ref[:, h * d:(h + 1) * d] = (acc_ref[h] / l_ref[h]).T.astype(o_ref.dtype)


def fox_attention(qkv, v_t, c_row, c_col, batch, seq, heads, tq, hb):
    nq = seq // tq
    d = HEAD_DIM
    groups = heads // hb
    kernel = functools.partial(_fox_attn_kernel, tq=tq, hb=hb, scale=d ** -0.5)
    return pl.pallas_call(
        kernel,
        grid=(batch, groups, nq),
        in_specs=[pl.BlockSpec((tq, hb * d), lambda b, g, i: (b * nq + i, g)),
                  pl.BlockSpec((seq, hb * d), lambda b, g, i: (b, groups + g)),
                  pl.BlockSpec((1, nq, hb * d, tq), lambda b, g, i: (b, 0, g, 0)),
                  pl.BlockSpec((1, hb, 1, tq), lambda b, g, i: (b, g, 0, i)),
                  pl.BlockSpec((1, hb, seq, 1), lambda b, g, i: (b, g, 0, 0))],
        out_specs=pl.BlockSpec((tq, hb * d), lambda b, g, i: (b * nq + i, g)),
        out_shape=jax.ShapeDtypeStruct((batch * seq, heads * d), BF16),
        scratch_shapes=[pltpu.VMEM((hb, 1, tq), F32), pltpu.VMEM((hb, 1, tq), F32), pltpu.VMEM((hb, d, tq), F32)],
        compiler_params=_params("parallel", "parallel", "arbitrary"),
        name="fox_attention",
    )(qkv, qkv, v_t, c_row, c_col)


def _conformer_kernel(val_ref, gate_ref, w_ref, cb_ref, lg_ref, lb_ref, o_ref, hist_ref, shift_ref, y_ref,
                      *, ts, rows):
    channels = val_ref.shape[1]

    @pl.when(pl.program_id(1) == 0)
    def _():
        hist_ref[0:CONV_HALO, :] = jnp.zeros((CONV_HALO, channels), F32)

    hist_ref[CONV_HALO:CONV_HALO + ts, :] = val_ref[...].astype(F32) * _sigmoid(gate_ref[...].astype(F32))

    base = CONV_HALO - (CONV_WIDTH - 1)
    hist_rows = ts + CONV_HALO
    for cb in range(channels // LANES):
        lanes = slice(cb * LANES, (cb + 1) * LANES)
        col = hist_ref[:, lanes]
        for b in range(1, SUBLANES):
            shift_ref[b - 1, :, lanes] = pltpu.roll(col, hist_rows - b, axis=0)
        for rb in range(ts // rows):
            acc = jnp.zeros((rows, LANES), F32)
            for j in range(CONV_WIDTH):
                a, b = divmod(base + j, SUBLANES)
                r0 = rb * rows + a * SUBLANES
                src = hist_ref[r0:r0 + rows, lanes] if b == 0 else shift_ref[b - 1, r0:r0 + rows, lanes]
                acc = acc + w_ref[j:j + 1, lanes] * src
            y_ref[rb * rows:(rb + 1) * rows, lanes] = acc + cb_ref[:, lanes]

    hist_ref[0:CONV_HALO, :] = hist_ref[ts:ts + CONV_HALO, :]

    y = y_ref[...]
    mu = jnp.mean(y, axis=-1, keepdims=True)
    yc = y - mu
    var = jnp.mean(yc * yc, axis=-1, keepdims=True)
    yn = yc * lax.rsqrt(var + NORM_EPS) * lg_ref[...] + lb_ref[...]
    o_ref[...] = (yn * _sigmoid(yn)).astype(o_ref.dtype)


def conformer_conv(glu, conv_w, conv_b, ln_g, ln_b, batch, seq, ts):
    channels = glu.shape[1] // 2
    ns = seq // ts
    kernel = functools.partial(_conformer_kernel, ts=ts, rows=64)
    vec = lambda: pl.BlockSpec((1, channels), lambda b, i: (0, 0))
    return pl.pallas_call(
        kernel,
        grid=(batch, ns),
        in_specs=[pl.BlockSpec((ts, channels), lambda b, i: (b * ns + i, 0)),
                  pl.BlockSpec((ts, channels), lambda b, i: (b * ns + i, 1)),
                  pl.BlockSpec((conv_w.shape[0], channels), lambda b, i: (0, 0)),
                  vec(), vec(), vec()],
        out_specs=pl.BlockSpec((ts, channels), lambda b, i: (b * ns + i, 0)),
        out_shape=jax.ShapeDtypeStruct((batch * seq, channels), BF16),
        scratch_shapes=[pltpu.VMEM((ts + CONV_HALO, channels), F32),
                        pltpu.VMEM((SUBLANES - 1, ts + CONV_HALO, channels), F32),
                        pltpu.VMEM((ts, channels), F32)],
        compiler_params=_params("arbitrary", "arbitrary"),
        name="conformer_conv",
    )(glu, glu, conv_w, conv_b, ln_g, ln_b)


def _gdn_prep_kernel(x_ref, ba_ref, w_ref, alog_ref, dtb_ref, q_ref, k_ref, v_ref, g_ref, hist_ref,
                     *, ts, qk_width, gate_heads):
    width = x_ref.shape[1]

    @pl.when(pl.program_id(1) == 0)
    def _():
        hist_ref[0:SHORT_HALO, :] = jnp.zeros((SHORT_HALO, width), F32)

    hist_ref[SHORT_HALO:SHORT_HALO + ts, :] = x_ref[...].astype(F32)
    base = SHORT_HALO - (SHORT_CONV - 1)
    hist_rows = ts + SHORT_HALO
    n_qk_heads = qk_width // HEAD_DIM
    for hb in range(width // HEAD_DIM):
        lanes = slice(hb * HEAD_DIM, (hb + 1) * HEAD_DIM)
        col = hist_ref[:, lanes]
        acc = jnp.zeros((ts, HEAD_DIM), F32)
        for j in range(SHORT_CONV):
            ofs = base + j
            src = col[ofs:ofs + ts] if ofs % SUBLANES == 0 else pltpu.roll(col, hist_rows - ofs, axis=0)[0:ts]
            acc = acc + w_ref[j:j + 1, lanes] * src
        y = acc * _sigmoid(acc)
        if hb < 2 * n_qk_heads:
            y = y * lax.rsqrt(jnp.sum(y * y, axis=-1, keepdims=True) + NORM_EPS)
            if hb < n_qk_heads:
                q_ref[:, lanes] = (y * HEAD_DIM ** -0.5).astype(q_ref.dtype)
            else:
                k_ref[:, (hb - n_qk_heads) * HEAD_DIM:(hb - n_qk_heads + 1) * HEAD_DIM] = y.astype(k_ref.dtype)
        else:
            c0 = (hb - 2 * n_qk_heads) * HEAD_DIM
            v_ref[:, c0:c0 + HEAD_DIM] = y.astype(v_ref.dtype)
    hist_ref[0:SHORT_HALO, :] = hist_ref[ts:ts + SHORT_HALO, :]

    ba = ba_ref[...]
    beta = _sigmoid(ba)
    g = -jnp.exp(alog_ref[...]) * _softplus(ba + dtb_ref[...])
    gcum = _prefix_sum_rows(g, CHUNK)
    lane = lax.broadcasted_iota(jnp.int32, ba.shape, 1)
    g_ref[...] = jnp.where(lane < gate_heads, beta, gcum)


def gdn_prep(qkv, ba, short_conv, alog_pad, dtb_pad, batch, seq, ts, qk_width, v_width, gate_heads):
    width = qkv.shape[1]
    ns = seq // ts
    m = batch * seq
    kernel = functools.partial(_gdn_prep_kernel, ts=ts, qk_width=qk_width, gate_heads=gate_heads)
    row = lambda b, i: (b * ns + i, 0)
    const = lambda b, i: (0, 0)
    return pl.pallas_call(
        kernel,
        grid=(batch, ns),
        in_specs=[pl.BlockSpec((ts, width), row),
                  pl.BlockSpec((ts, LANES), row),
                  pl.BlockSpec((short_conv.shape[0], width), const),
                  pl.BlockSpec((1, LANES), const),
                  pl.BlockSpec((1, LANES), const)],
        out_specs=[pl.BlockSpec((ts, qk_width), row), pl.BlockSpec((ts, qk_width), row),
                   pl.BlockSpec((ts, v_width), row), pl.BlockSpec((ts, LANES), row)],
        out_shape=[jax.ShapeDtypeStruct((m, qk_width), BF16), jax.ShapeDtypeStruct((m, qk_width), BF16),
                   jax.ShapeDtypeStruct((m, v_width), BF16), jax.ShapeDtypeStruct((m, LANES), F32)],
        scratch_shapes=[pltpu.VMEM((ts + SHORT_HALO, width), F32)],
        compiler_params=_params("arbitrary", "arbitrary"),
        name="gdn_prep",
    )(qkv, ba, short_conv, alog_pad, dtb_pad)


GROUP = MXU_TILE // CHUNK


def _gdn_solve_kernel(q_ref, k_ref, g_ref, gr_ref, bdm_ref, kkm_ref, t_ref, p_ref, *, v_heads, rep):
    c = CHUNK
    gw = MXU_TILE
    n_groups = v_heads // GROUP
    gates = g_ref[...]
    row = lax.broadcasted_iota(jnp.int32, (c, gw), 0)
    lane = lax.broadcasted_iota(jnp.int32, (c, gw), 1)
    blk = lane // c
    col = lane % c
    tril = row >= col
    strict = row > col
    eye = (row == col).astype(F32)
    bd_mask = bdm_ref[...]
    kk_mask = kkm_ref[...]

    def block_diag(x):
        xb = x.astype(BF16)
        return jnp.concatenate([xb] * GROUP, axis=0) * bd_mask

    def per_head_columns(first_lane, g):
        out = gates[:, first_lane + GROUP * g + GROUP - 1:first_lane + GROUP * g + GROUP]
        for mth in range(GROUP - 2, -1, -1):
            cm = gates[:, first_lane + GROUP * g + mth:first_lane + GROUP * g + mth + 1]
            out = jnp.where(blk == mth, cm, out)
        return out

    groups = range(n_groups)
    lanes = [slice(g * gw, (g + 1) * gw) for g in groups]
    kq = [jnp.concatenate([k_ref[:, lanes[g]], q_ref[:, lanes[g]]], axis=0) for g in groups]
    rhs = [jnp.concatenate([k_ref[:, lanes[g]]] * GROUP, axis=0) * kk_mask for g in groups]
    kkqk = [lax.dot_general(kq[g], rhs[g], (((1,), (1,)), ((), ())), preferred_element_type=F32) for g in groups]
    decay = [jnp.exp(jnp.where(tril, per_head_columns(v_heads, g) - gr_ref[0, :, lanes[g]], -jnp.inf))
             for g in groups]
    low = [jnp.where(strict, per_head_columns(0, g) * kkqk[g][:c] * decay[g], 0.0) for g in groups]
    for g in groups:
        p_ref[:, lanes[g]] = (kkqk[g][c:] * decay[g]).astype(p_ref.dtype)

    inv = [eye - low[g] for g in groups]
    bd = [block_diag(low[g]) for g in groups]
    power = [jnp.dot(low[g].astype(BF16), bd[g], preferred_element_type=F32) for g in groups]
    span = 2
    while span <= c // 2:
        bd = [block_diag(power[g]) for g in groups]
        if span < c // 2:
            both = [jnp.dot(jnp.concatenate([power[g], inv[g]], axis=0).astype(BF16), bd[g],
                            preferred_element_type=F32) for g in groups]
            power = [both[g][:c] for g in groups]
            inv = [inv[g] + both[g][c:] for g in groups]
        else:
            inv = [inv[g] + jnp.dot(inv[g].astype(BF16), bd[g], preferred_element_type=F32) for g in groups]
        span *= 2
    for g in groups:
        t_ref[:, lanes[g]] = inv[g].astype(t_ref.dtype)


def gdn_solve(q, k, gates, g_rows, v_heads, rep):
    m, qk_width = q.shape
    width = v_heads * CHUNK
    assert GROUP * CHUNK == MXU_TILE and rep * HEAD_DIM == MXU_TILE and v_heads % GROUP == 0
    assert qk_width * rep == v_heads * HEAD_DIM and qk_width == width
    rb = lax.broadcasted_iota(jnp.int32, (GROUP * CHUNK, MXU_TILE), 0) // CHUNK
    ln = lax.broadcasted_iota(jnp.int32, (GROUP * CHUNK, MXU_TILE), 1)
    bd_mask = (rb == ln // CHUNK).astype(BF16)
    kk_mask = (rb // rep == ln // HEAD_DIM).astype(BF16)
    kernel = functools.partial(_gdn_solve_kernel, v_heads=v_heads, rep=rep)
    row = lambda i: (i, 0)
    const = lambda i: (0, 0)
    return pl.pallas_call(
        kernel,
        grid=(m // CHUNK,),
        in_specs=[pl.BlockSpec((CHUNK, qk_width), row), pl.BlockSpec((CHUNK, qk_width), row),
                  pl.BlockSpec((CHUNK, LANES), row),
                  pl.BlockSpec((1, 1, width), lambda i: (i, 0, 0)),
                  pl.BlockSpec((GROUP * CHUNK, MXU_TILE), const), pl.BlockSpec((GROUP * CHUNK, MXU_TILE), const)],
        out_specs=[pl.BlockSpec((CHUNK, width), row), pl.BlockSpec((CHUNK, width), row)],
        out_shape=[jax.ShapeDtypeStruct((m, width), BF16), jax.ShapeDtypeStruct((m, width), BF16)],
        compiler_params=_params("parallel"),
        name="gdn_solve",
    )(q, k, gates, g_rows, bd_mask, kk_mask)


def _gdn_chunk_kernel(q_ref, k_ref, v_ref, z_ref, g_ref, t_ref, p_ref, on_ref, o_ref, state_ref, *, qk_heads, rep):
    c = CHUNK
    d = HEAD_DIM

    @pl.when(pl.program_id(1) == 0)
    def _():
        state_ref[...] = jnp.zeros(state_ref.shape, F32)

    gates = g_ref[...]
    n_v = qk_heads * rep
    pw = rep * d
    gain = on_ref[...]
    zeros = jnp.zeros((c, d), BF16)
    pairs = range(qk_heads)

    def on_diagonal(parts):
        return jnp.concatenate(
            [jnp.concatenate([parts[r] if s == r else zeros for s in range(rep)], axis=1) for r in range(rep)],
            axis=0)

    def head_cols(first_lane, p):
        return [gates[:, first_lane + p * rep + r:first_lane + p * rep + r + 1] for r in range(rep)]

    k = [k_ref[:, p * d:(p + 1) * d] for p in pairs]
    kq = [jnp.concatenate([k[p], q_ref[:, p * d:(p + 1) * d]], axis=0) for p in pairs]
    state = [state_ref[p] for p in pairs]
    kqs = [jnp.dot(kq[p], state[p].astype(BF16), preferred_element_type=F32) for p in pairs]

    beta = [head_cols(0, p) for p in pairs]
    gc = [head_cols(n_v, p) for p in pairs]
    eg = [[jnp.exp(g) for g in gc[p]] for p in pairs]
    rhs = [on_diagonal([(beta[p][r] * (v_ref[:, p * pw + r * d:p * pw + (r + 1) * d].astype(F32)
                                       - kqs[p][:c, r * d:(r + 1) * d] * eg[p][r])).astype(BF16)
                        for r in range(rep)]) for p in pairs]
    tw = rep * c
    v_new = [jnp.dot(t_ref[:, p * tw:(p + 1) * tw], rhs[p], preferred_element_type=F32) for p in pairs]
    vn_diag = [on_diagonal([v_new[p][:, r * d:(r + 1) * d].astype(BF16) for r in range(rep)]) for p in pairs]
    intra = [jnp.dot(p_ref[:, p * tw:(p + 1) * tw], vn_diag[p], preferred_element_type=F32) for p in pairs]

    g_last = [[g[c - 1:c, :] for g in gc[p]] for p in pairs]
    v_dec = [jnp.concatenate([jnp.exp(g_last[p][r] - gc[p][r]) * v_new[p][:, r * d:(r + 1) * d]
                              for r in range(rep)], axis=1).astype(BF16) for p in pairs]
    kt = [k[p].astype(F32).T.astype(BF16) for p in pairs]
    upd = [jnp.dot(kt[p], v_dec[p], preferred_element_type=F32) for p in pairs]
    for p in pairs:
        scale = jnp.concatenate([jnp.broadcast_to(jnp.exp(g_last[p][r]), (1, d)) for r in range(rep)], axis=1)
        state_ref[p] = state[p] * scale + upd[p]

    for p in pairs:
        for r in range(rep):
            vl = slice(p * pw + r * d, p * pw + (r + 1) * d)
            o = kqs[p][c:, r * d:(r + 1) * d] * eg[p][r] + intra[p][:, r * d:(r + 1) * d]
            on = o * lax.rsqrt(jnp.mean(o * o, axis=-1, keepdims=True) + NORM_EPS) * gain
            zf = z_ref[:, vl].astype(F32)
            o_ref[:, vl] = (on * (zf * _sigmoid(zf))).astype(o_ref.dtype)


def gdn_chunk(q, k, v, z, gates, t_inv, p_intra, o_norm, batch, seq, qk_heads, rep):
    m = batch * seq
    nc = seq // CHUNK
    qk_width = qk_heads * HEAD_DIM
    v_width = qk_width * rep
    kernel = functools.partial(_gdn_chunk_kernel, qk_heads=qk_heads, rep=rep)
    row = lambda b, i: (b * nc + i, 0)
    return pl.pallas_call(
        kernel,
        grid=(batch, nc),
        in_specs=[pl.BlockSpec((CHUNK, qk_width), row), pl.BlockSpec((CHUNK, qk_width), row),
                  pl.BlockSpec((CHUNK, v_width), row), pl.BlockSpec((CHUNK, v_width), row),
                  pl.BlockSpec((CHUNK, LANES), row),
                  pl.BlockSpec((CHUNK, t_inv.shape[1]), row), pl.BlockSpec((CHUNK, p_intra.shape[1]), row),
                  pl.BlockSpec((1, HEAD_DIM), lambda b, i: (0, 0))],
        out_specs=pl.BlockSpec((CHUNK, v_width), row),
        out_shape=jax.ShapeDtypeStruct((m, v_width), BF16),
        scratch_shapes=[pltpu.VMEM((qk_heads, HEAD_DIM, rep * HEAD_DIM), F32)],
        compiler_params=_params("arbitrary", "arbitrary"),
        name="gdn_chunk",
    )(q, k, v, z, gates, t_inv, p_intra, o_norm)


def _pad_cols(w, n):
    return jnp.pad(w, ((0, 0), (0, n - w.shape[1])))


def _row(v):
    return v.reshape(1, -1).astype(F32)


def kernel(x, norm_mix0, w_in0, fgate_bias0, conv_w0, conv_b0, conv_ln_g0, conv_ln_b0, w_out0, norm_mix1, w_in1,
           short_conv1, a_log1, dt_bias1, o_norm1, w_out1, norm_ffn, w_gate, w_up, w_down, final_norm):
    batch, seq, d_model = x.shape
    m = batch * seq
    fox_heads = fgate_bias0.shape[1]
    fox_width = fox_heads * HEAD_DIM
    conv_ch = conv_w0.shape[2]
    gdn_v_heads = a_log1.shape[1]
    gdn_v_width = gdn_v_heads * HEAD_DIM
    gdn_qkv_width = short_conv1.shape[2]
    gdn_qk_width = (gdn_qkv_width - gdn_v_width) // 2
    gdn_qk_heads = gdn_qk_width // HEAD_DIM
    rep = gdn_v_heads // gdn_qk_heads

    h = x.reshape(m, d_model)

    w_down_bf16 = w_down.astype(BF16)

    def ffn(h, layer):
        mid = norm_swiglu(h, _row(norm_ffn[layer]), wview(w_gate, layer), wview(w_up, layer), bm=1024, bn=512)
        return matmul_residual([mid], [wview(w_down_bf16, layer)], h, bm=512, bn=1024)

    w0 = w_in0[0]
    g0 = _row(norm_mix0[0])
    qkv = norm_matmul(h, g0, wview(w_in0, 0, cols=3 * fox_width), BF16, bm=1024, bn=1024)
    f_logit = norm_matmul(h, g0,
                          wview(_pad_cols(w0[:, 3 * fox_width:3 * fox_width + fox_heads], LANES).astype(BF16)),
                          F32, bm=1024, bn=LANES)
    glu = norm_matmul(h, g0, wview(w0[:, 3 * fox_width + fox_heads:].astype(BF16)), BF16, bm=1024, bn=1024)

    tq = 256
    c = fox_gates(f_logit, _pad_cols(_row(fgate_bias0[0]), LANES), batch, seq)
    c_heads = jnp.transpose(c.reshape(batch, seq, LANES)[:, :, :fox_heads], (0, 2, 1))
    v_t = jnp.transpose(qkv[:, 2 * fox_width:].reshape(batch, seq // tq, tq, fox_width), (0, 1, 3, 2))
    a_out = fox_attention(qkv, v_t, c_heads.reshape(batch, fox_heads, 1, seq),
                          c_heads.reshape(batch, fox_heads, seq, 1), batch, seq, fox_heads, tq,
                          hb=min(4, fox_heads))
    u = conformer_conv(glu, jnp.pad(conv_w0[0], ((0, CONV_HALO - CONV_WIDTH), (0, 0))), _row(conv_b0[0]),
                       _row(conv_ln_g0[0]), _row(conv_ln_b0[0]), batch, seq, ts=256)
    h = matmul_residual([a_out, u], [wview(w_out0, 0, row0=0, rows=fox_width),
                                     wview(w_out0, 0, row0=fox_width, rows=conv_ch)], h, bm=512, bn=1024)
    h = ffn(h, 0)

    w1 = w_in1[0]
    g1 = _row(norm_mix1[0])
    qkv1 = norm_matmul(h, g1, wview(w_in1, 0, cols=gdn_qkv_width), BF16, bm=1024, bn=1024)
    z1 = norm_matmul(h, g1, wview(w_in1, 0, col0=gdn_qkv_width, cols=gdn_v_width), BF16, bm=1024, bn=1024)
    ba1 = norm_matmul(h, g1, wview(_pad_cols(w1[:, gdn_qkv_width + gdn_v_width:], LANES).astype(BF16)), F32,
                      bm=1024, bn=LANES)
    alog_pad = jnp.pad(_row(a_log1[0]), ((0, 0), (gdn_v_heads, LANES - 2 * gdn_v_heads)))
    dtb_pad = jnp.pad(_row(dt_bias1[0]), ((0, 0), (gdn_v_heads, LANES - 2 * gdn_v_heads)))
    q1, k1, v1, gates1 = gdn_prep(qkv1, ba1, jnp.pad(short_conv1[0], ((0, 8 - SHORT_CONV), (0, 0))), alog_pad,
                                  dtb_pad, batch, seq, 256, gdn_qk_width, gdn_v_width, gdn_v_heads)
    g_rows = jnp.transpose(gates1[:, gdn_v_heads:2 * gdn_v_heads].reshape(m // CHUNK, CHUNK, gdn_v_heads),
                           (0, 2, 1)).reshape(m // CHUNK, 1, gdn_v_heads * CHUNK)
    t_inv, p_intra = gdn_solve(q1, k1, gates1, g_rows, gdn_v_heads, rep)
    o1 = gdn_chunk(q1, k1, v1, z1, gates1, t_inv, p_intra, _row(o_norm1[0]), batch, seq, gdn_qk_heads, rep)
    h = matmul_residual([o1], [wview(w_out1, 0)], h, bm=512, bn=512)
    h = ffn(h, 1)

    return rmsnorm(h, _row(final_norm), bm=512).reshape(batch, seq, d_model)
```

```python
import functools
from typing import NamedTuple, Optional

import jax
import jax.numpy as jnp
from jax import lax
from jax.experimental import pallas as pl
from jax.experimental.pallas import tpu as pltpu

F32 = jnp.float32
BF16 = jnp.bfloat16

NORM_EPS = 1e-6
LANES = 128
SUBLANES = 8
MXU_TILE = 256
HEAD_DIM = 128
CHUNK = 64
CONV_WIDTH = 31
CONV_HALO = 32
SHORT_CONV = 4
SHORT_HALO = 8
VMEM_LIMIT_BYTES = 56 * 1024 * 1024


def _params(*sem):
    return pltpu.CompilerParams(dimension_semantics=sem, vmem_limit_bytes=VMEM_LIMIT_BYTES)


def _blk(dim, pref, align):
    best = dim
    for cand in range(align, min(dim, pref) + 1, align):
        if dim % cand == 0:
            best = cand
    return best


def _sigmoid(x):
    return 1.0 / (1.0 + jnp.exp(-x))


def _softplus(x):
    return jnp.maximum(x, 0.0) + jnp.log(1.0 + jnp.exp(-jnp.abs(x)))


def _rms_to_scratch(x_ref, g_ref, xn_ref):
    x = x_ref[...].astype(F32)
    ms = jnp.mean(x * x, axis=-1, keepdims=True)
    xn_ref[...] = (x * lax.rsqrt(ms + NORM_EPS) * g_ref[...]).astype(xn_ref.dtype)


class WeightView(NamedTuple):
    arr: jax.Array
    lead: Optional[int]
    row0: int
    rows: int
    col0: int
    cols: int
    transposed: bool


def wview(arr, lead=None, row0=0, rows=None, col0=0, cols=None, transposed=False):
    k, n = arr.shape[-2:][::-1] if transposed else arr.shape[-2:]
    return WeightView(arr, lead, row0, k - row0 if rows is None else rows, col0, n - col0 if cols is None else cols,
                      transposed)


def _w_spec(wv, bn, col_block):
    assert wv.row0 % wv.rows == 0 and wv.col0 % bn == 0 and wv.cols % bn == 0
    rb, cb = wv.row0 // wv.rows, wv.col0 // bn
    shape = (bn, wv.rows) if wv.transposed else (wv.rows, bn)
    index = (lambda *g: (cb + col_block(*g), rb)) if wv.transposed else (lambda *g: (rb, cb + col_block(*g)))
    if wv.arr.ndim == 3:
        return pl.BlockSpec((None,) + shape, lambda *g: (wv.lead,) + index(*g))
    return pl.BlockSpec(shape, index)


def _dot_w(x, w, transposed):
    dims = (((1,), (1,)), ((), ())) if transposed else (((1,), (0,)), ((), ()))
    return lax.dot_general(x, w.astype(BF16), dims, preferred_element_type=F32)


def _norm_mm_kernel(x_ref, g_ref, w_ref, o_ref, xn_ref, *, transposed):
    @pl.when(pl.program_id(1) == 0)
    def _():
        _rms_to_scratch(x_ref, g_ref, xn_ref)

    o_ref[...] = _dot_w(xn_ref[...], w_ref[...], transposed).astype(o_ref.dtype)


def norm_matmul(x, gain, wv, out_dtype, bm, bn):
    m, k = x.shape
    n = wv.cols
    bm, bn = _blk(m, bm, 8), _blk(n, bn, LANES)
    return pl.pallas_call(
        functools.partial(_norm_mm_kernel, transposed=wv.transposed),
        grid=(m // bm, n // bn),
        in_specs=[pl.BlockSpec((bm, k), lambda i, j: (i, 0)),
                  pl.BlockSpec((1, k), lambda i, j: (0, 0)),
                  _w_spec(wv, bn, lambda i, j: j)],
        out_specs=pl.BlockSpec((bm, bn), lambda i, j: (i, j)),
        out_shape=jax.ShapeDtypeStruct((m, n), out_dtype),
        scratch_shapes=[pltpu.VMEM((bm, k), BF16)],
        compiler_params=_params("parallel", "arbitrary"),
        name="norm_matmul",
    )(x, gain, wv.arr)


def _norm_swiglu_kernel(x_ref, g_ref, wg_ref, wu_ref, o_ref, xn_ref):
    @pl.when(pl.program_id(1) == 0)
    def _():
        _rms_to_scratch(x_ref, g_ref, xn_ref)

    xn = xn_ref[...]
    gate = jnp.dot(xn, wg_ref[...].astype(BF16), preferred_element_type=F32)
    up = jnp.dot(xn, wu_ref[...].astype(BF16), preferred_element_type=F32)
    o_ref[...] = (gate * _sigmoid(gate) * up).astype(o_ref.dtype)


def norm_swiglu(x, gain, wg, wu, bm, bn):
    m, k = x.shape
    n = wg.cols
    bm, bn = _blk(m, bm, 8), _blk(n, bn, LANES)
    return pl.pallas_call(
        _norm_swiglu_kernel,
        grid=(m // bm, n // bn),
        in_specs=[pl.BlockSpec((bm, k), lambda i, j: (i, 0)),
                  pl.BlockSpec((1, k), lambda i, j: (0, 0)),
                  _w_spec(wg, bn, lambda i, j: j),
                  _w_spec(wu, bn, lambda i, j: j)],
        out_specs=pl.BlockSpec((bm, bn), lambda i, j: (i, j)),
        out_shape=jax.ShapeDtypeStruct((m, n), BF16),
        scratch_shapes=[pltpu.VMEM((bm, k), BF16)],
        compiler_params=_params("parallel", "arbitrary"),
        name="norm_swiglu",
    )(x, gain, wg.arr, wu.arr)


def _mm_res_kernel(*refs, n_pairs):
    xs = refs[:n_pairs]
    ws = refs[n_pairs:2 * n_pairs]
    res_ref = refs[2 * n_pairs]
    o_ref = refs[2 * n_pairs + 1]
    wb = refs[2 * n_pairs + 2:]

    @pl.when(pl.program_id(1) == 0)
    def _():
        for w_ref, wb_ref in zip([w for w in ws if w.dtype != BF16], wb):
            wb_ref[...] = w_ref[...].astype(BF16)

    wb = list(wb)
    acc = res_ref[...]
    for x_ref, w_ref in zip(xs, ws):
        w = w_ref[...] if w_ref.dtype == BF16 else wb.pop(0)[...]
        acc = acc + jnp.dot(x_ref[...], w, preferred_element_type=F32)
    o_ref[...] = acc


def matmul_residual(xs, wvs, res, bm, bn):
    m, n = res.shape
    bm, bn = _blk(m, bm, 8), _blk(n, bn, LANES)
    n_pairs = len(xs)
    in_specs = ([pl.BlockSpec((bm, x.shape[1]), lambda j, i: (i, 0)) for x in xs]
                + [_w_spec(wv, bn, lambda j, i: j) for wv in wvs]
                + [pl.BlockSpec((bm, bn), lambda j, i: (i, j))])
    return pl.pallas_call(
        functools.partial(_mm_res_kernel, n_pairs=n_pairs),
        grid=(n // bn, m // bm),
        in_specs=in_specs,
        out_specs=pl.BlockSpec((bm, bn), lambda j, i: (i, j)),
        out_shape=jax.ShapeDtypeStruct((m, n), F32),
        scratch_shapes=[pltpu.VMEM((wv.rows, bn), BF16) for wv in wvs if wv.arr.dtype != BF16],
        compiler_params=_params("parallel", "arbitrary"),
        name="matmul_residual",
    )(*xs, *[wv.arr for wv in wvs], res)


def _rmsnorm_kernel(x_ref, g_ref, o_ref):
    x = x_ref[...]
    ms = jnp.mean(x * x, axis=-1, keepdims=True)
    o_ref[...] = x * lax.rsqrt(ms + NORM_EPS) * g_ref[...]


def rmsnorm(x, gain, bm):
    m, k = x.shape
    bm = _blk(m, bm, 8)
    return pl.pallas_call(
        _rmsnorm_kernel,
        grid=(m // bm,),
        in_specs=[pl.BlockSpec((bm, k), lambda i: (i, 0)), pl.BlockSpec((1, k), lambda i: (0, 0))],
        out_specs=pl.BlockSpec((bm, k), lambda i: (i, 0)),
        out_shape=jax.ShapeDtypeStruct((m, k), F32),
        compiler_params=_params("parallel"),
        name="final_rmsnorm",
    )(x, gain)


def _prefix_sum_rows(x, period):
    row = lax.broadcasted_iota(jnp.int32, x.shape, 0) % period
    shift = 1
    while shift < period:
        x = x + jnp.where(row >= shift, pltpu.roll(x, shift, axis=0), 0.0)
        shift *= 2
    return x


def _fox_gates_kernel(f_ref, b_ref, c_ref):
    z = f_ref[...] + b_ref[...]
    log_f = jnp.minimum(z, 0.0) - jnp.log(1.0 + jnp.exp(-jnp.abs(z)))
    c_ref[...] = _prefix_sum_rows(log_f, log_f.shape[0])


def fox_gates(f_logit, f_bias, batch, seq):
    return pl.pallas_call(
        _fox_gates_kernel,
        grid=(batch,),
        in_specs=[pl.BlockSpec((seq, LANES), lambda b: (b, 0)), pl.BlockSpec((1, LANES), lambda b: (0, 0))],
        out_specs=pl.BlockSpec((seq, LANES), lambda b: (b, 0)),
        out_shape=jax.ShapeDtypeStruct((batch * seq, LANES), F32),
        compiler_params=_params("parallel"),
        name="fox_gates",
    )(f_logit, f_bias)


def _fox_attn_kernel(q_ref, k_ref, vt_ref, cq_ref, ck_ref, o_ref, m_ref, l_ref, acc_ref, *, tq, hb, scale):
    i = pl.program_id(2)
    d = HEAD_DIM
    heads = range(hb)
    q = [q_ref[:, h * d:(h + 1) * d] for h in heads]
    cq = [cq_ref[0, h] for h in heads]
    for h in heads:
        m_ref[h] = jnp.full((1, tq), -jnp.inf, F32)
        l_ref[h] = jnp.zeros((1, tq), F32)
        acc_ref[h] = jnp.zeros((d, tq), F32)
    key_after_query = (lax.broadcasted_iota(jnp.int32, (tq, tq), 0) > lax.broadcasted_iota(jnp.int32, (tq, tq), 1))

    def block(j, on_diagonal):
        start = pl.multiple_of(j * tq, tq)
        s = [lax.dot_general(k_ref[pl.ds(start, tq), h * d:(h + 1) * d], q[h], (((1,), (1,)), ((), ())),
                             preferred_element_type=F32) for h in heads]
        x = [s[h] * scale + (cq[h] - ck_ref[0, h, pl.ds(start, tq), :]) for h in heads]
        if on_diagonal:
            x = [jnp.where(key_after_query, -jnp.inf, x[h]) for h in heads]
        m_old = [m_ref[h] for h in heads]
        m_new = [jnp.maximum(m_old[h], jnp.max(x[h], axis=0, keepdims=True)) for h in heads]
        alpha = [jnp.exp(m_old[h] - m_new[h]) for h in heads]
        p = [jnp.exp(x[h] - m_new[h]) for h in heads]
        pv = [jnp.dot(vt_ref[0, j, h * d:(h + 1) * d, :], p[h].astype(BF16), preferred_element_type=F32)
              for h in heads]
        for h in heads:
            m_ref[h] = m_new[h]
            l_ref[h] = alpha[h] * l_ref[h] + jnp.sum(p[h], axis=0, keepdims=True)
            acc_ref[h] = alpha[h] * acc_ref[h] + pv[h]

    def body(j, carry):
        block(j, False)
        return carry

    lax.fori_loop(0, i, body, 0)
    block(i, True)
    for h in heads:
        o_ref[:, h * d:(h + 1) * d] = (acc_ref[h] / l_ref[h]).T.astype(o_ref.dtype)


def fox_attention(qkv, v_t, c_row, c_col, batch, seq, heads, tq, hb):
    nq = seq // tq
    d = HEAD_DIM
    groups = heads // hb
    kernel = functools.partial(_fox_attn_kernel, tq=tq, hb=hb, scale=d ** -0.5)
    return pl.pallas_call(
        kernel,
        grid=(batch, groups, nq),
        in_specs=[pl.BlockSpec((tq, hb * d), lambda b, g, i: (b * nq + i, g)),
                  pl.BlockSpec((seq, hb * d), lambda b, g, i: (b, groups + g)),
                  pl.BlockSpec((1, nq, hb * d, tq), lambda b, g, i: (b, 0, g, 0)),
                  pl.BlockSpec((1, hb, 1, tq), lambda b, g, i: (b, g, 0, i)),
                  pl.BlockSpec((1, hb, seq, 1), lambda b, g, i: (b, g, 0, 0))],
        out_specs=pl.BlockSpec((tq, hb * d), lambda b, g, i: (b * nq + i, g)),
        out_shape=jax.ShapeDtypeStruct((batch * seq, heads * d), BF16),
        scratch_shapes=[pltpu.VMEM((hb, 1, tq), F32), pltpu.VMEM((hb, 1, tq), F32), pltpu.VMEM((hb, d, tq), F32)],
        compiler_params=_params("parallel", "parallel", "arbitrary"),
        name="fox_attention",
    )(qkv, qkv, v_t, c_row, c_col)


def _conformer_kernel(val_ref, gate_ref, w_ref, cb_ref, lg_ref, lb_ref, o_ref, hist_ref, shift_ref, y_ref,
                      *, ts, rows):
    channels = val_ref.shape[1]

    @pl.when(pl.program_id(1) == 0)
    def _():
        hist_ref[0:CONV_HALO, :] = jnp.zeros((CONV_HALO, channels), F32)

    hist_ref[CONV_HALO:CONV_HALO + ts, :] = val_ref[...].astype(F32) * _sigmoid(gate_ref[...].astype(F32))

    base = CONV_HALO - (CONV_WIDTH - 1)
    hist_rows = ts + CONV_HALO
    for cb in range(channels // LANES):
        lanes = slice(cb * LANES, (cb + 1) * LANES)
        col = hist_ref[:, lanes]
        for b in range(1, SUBLANES):
            shift_ref[b - 1, :, lanes] = pltpu.roll(col, hist_rows - b, axis=0)
        for rb in range(ts // rows):
            acc = jnp.zeros((rows, LANES), F32)
            for j in range(CONV_WIDTH):
                a, b = divmod(base + j, SUBLANES)
                r0 = rb * rows + a * SUBLANES
                src = hist_ref[r0:r0 + rows, lanes] if b == 0 else shift_ref[b - 1, r0:r0 + rows, lanes]
                acc = acc + w_ref[j:j + 1, lanes] * src
            y_ref[rb * rows:(rb + 1) * rows, lanes] = acc + cb_ref[:, lanes]

    hist_ref[0:CONV_HALO, :] = hist_ref[ts:ts + CONV_HALO, :]

    y = y_ref[...]
    mu = jnp.mean(y, axis=-1, keepdims=True)
    yc = y - mu
    var = jnp.mean(yc * yc, axis=-1, keepdims=True)
    yn = yc * lax.rsqrt(var + NORM_EPS) * lg_ref[...] + lb_ref[...]
    o_ref[...] = (yn * _sigmoid(yn)).astype(o_ref.dtype)


def conformer_conv(glu, conv_w, conv_b, ln_g, ln_b, batch, seq, ts):
    channels = glu.shape[1] // 2
    ns = seq // ts
    kernel = functools.partial(_conformer_kernel, ts=ts, rows=64)
    vec = lambda: pl.BlockSpec((1, channels), lambda b, i: (0, 0))
    return pl.pallas_call(
        kernel,
        grid=(batch, ns),
        in_specs=[pl.BlockSpec((ts, channels), lambda b, i: (b * ns + i, 0)),
                  pl.BlockSpec((ts, channels), lambda b, i: (b * ns + i, 1)),
                  pl.BlockSpec((conv_w.shape[0], channels), lambda b, i: (0, 0)),
                  vec(), vec(), vec()],
        out_specs=pl.BlockSpec((ts, channels), lambda b, i: (b * ns + i, 0)),
        out_shape=jax.ShapeDtypeStruct((batch * seq, channels), BF16),
        scratch_shapes=[pltpu.VMEM((ts + CONV_HALO, channels), F32),
                        pltpu.VMEM((SUBLANES - 1, ts + CONV_HALO, channels), F32),
                        pltpu.VMEM((ts, channels), F32)],
        compiler_params=_params("arbitrary", "arbitrary"),
        name="conformer_conv",
    )(glu, glu, conv_w, conv_b, ln_g, ln_b)


def _gdn_prep_kernel(x_ref, ba_ref, w_ref, alog_ref, dtb_ref, q_ref, k_ref, v_ref, g_ref, hist_ref,
                     *, ts, qk_width, gate_heads):
    width = x_ref.shape[1]

    @pl.when(pl.program_id(1) == 0)
    def _():
        hist_ref[0:SHORT_HALO, :] = jnp.zeros((SHORT_HALO, width), F32)

    hist_ref[SHORT_HALO:SHORT_HALO + ts, :] = x_ref[...].astype(F32)
    base = SHORT_HALO - (SHORT_CONV - 1)
    hist_rows = ts + SHORT_HALO
    n_qk_heads = qk_width // HEAD_DIM
    for hb in range(width // HEAD_DIM):
        lanes = slice(hb * HEAD_DIM, (hb + 1) * HEAD_DIM)
        col = hist_ref[:, lanes]
        acc = jnp.zeros((ts, HEAD_DIM), F32)
        for j in range(SHORT_CONV):
            ofs = base + j
            src = col[ofs:ofs + ts] if ofs % SUBLANES == 0 else pltpu.roll(col, hist_rows - ofs, axis=0)[0:ts]
            acc = acc + w_ref[j:j + 1, lanes] * src
        y = acc * _sigmoid(acc)
        if hb < 2 * n_qk_heads:
            y = y * lax.rsqrt(jnp.sum(y * y, axis=-1, keepdims=True) + NORM_EPS)
            if hb < n_qk_heads:
                q_ref[:, lanes] = (y * HEAD_DIM ** -0.5).astype(q_ref.dtype)
            else:
                k_ref[:, (hb - n_qk_heads) * HEAD_DIM:(hb - n_qk_heads + 1) * HEAD_DIM] = y.astype(k_ref.dtype)
        else:
            c0 = (hb - 2 * n_qk_heads) * HEAD_DIM
            v_ref[:, c0:c0 + HEAD_DIM] = y.astype(v_ref.dtype)
    hist_ref[0:SHORT_HALO, :] = hist_ref[ts:ts + SHORT_HALO, :]

    ba = ba_ref[...]
    beta = _sigmoid(ba)
    g = -jnp.exp(alog_ref[...]) * _softplus(ba + dtb_ref[...])
    gcum = _prefix_sum_rows(g, CHUNK)
    lane = lax.broadcasted_iota(jnp.int32, ba.shape, 1)
    g_ref[...] = jnp.where(lane < gate_heads, beta, gcum)


def gdn_prep(qkv, ba, short_conv, alog_pad, dtb_pad, batch, seq, ts, qk_width, v_width, gate_heads):
    width = qkv.shape[1]
    ns = seq // ts
    m = batch * seq
    kernel = functools.partial(_gdn_prep_kernel, ts=ts, qk_width=qk_width, gate_heads=gate_heads)
    row = lambda b, i: (b * ns + i, 0)
    const = lambda b, i: (0, 0)
    return pl.pallas_call(
        kernel,
        grid=(batch, ns),
        in_specs=[pl.BlockSpec((ts, width), row),
                  pl.BlockSpec((ts, LANES), row),
                  pl.BlockSpec((short_conv.shape[0], width), const),
                  pl.BlockSpec((1, LANES), const),
                  pl.BlockSpec((1, LANES), const)],
        out_specs=[pl.BlockSpec((ts, qk_width), row), pl.BlockSpec((ts, qk_width), row),
                   pl.BlockSpec((ts, v_width), row), pl.BlockSpec((ts, LANES), row)],
        out_shape=[jax.ShapeDtypeStruct((m, qk_width), BF16), jax.ShapeDtypeStruct((m, qk_width), BF16),
                   jax.ShapeDtypeStruct((m, v_width), BF16), jax.ShapeDtypeStruct((m, LANES), F32)],
        scratch_shapes=[pltpu.VMEM((ts + SHORT_HALO, width), F32)],
        compiler_params=_params("arbitrary", "arbitrary"),
        name="gdn_prep",
    )(qkv, ba, short_conv, alog_pad, dtb_pad)


GROUP = MXU_TILE // CHUNK


def _gdn_solve_kernel(q_ref, k_ref, g_ref, gr_ref, bdm_ref, kkm_ref, t_ref, p_ref, *, v_heads, rep):
    c = CHUNK
    gw = MXU_TILE
    n_groups = v_heads // GROUP
    n_chunks = g_ref.shape[0] // c
    row = lax.broadcasted_iota(jnp.int32, (c, gw), 0)
    lane = lax.broadcasted_iota(jnp.int32, (c, gw), 1)
    blk = lane // c
    col = lane % c
    tril = row >= col
    strict = row > col
    eye = (row == col).astype(F32)
    bd_mask = bdm_ref[...]
    kk_mask = kkm_ref[...]

    def block_diag(x):
        xb = x.astype(BF16)
        return jnp.concatenate([xb] * GROUP, axis=0) * bd_mask

    gates = [g_ref[n * c:(n + 1) * c, :] for n in range(n_chunks)]

    def per_head_columns(first_lane, u):
        n, g = units[u]
        out = gates[n][:, first_lane + GROUP * g + GROUP - 1:first_lane + GROUP * g + GROUP]
        for mth in range(GROUP - 2, -1, -1):
            cm = gates[n][:, first_lane + GROUP * g + mth:first_lane + GROUP * g + mth + 1]
            out = jnp.where(blk == mth, cm, out)
        return out

    units = [(n, g) for n in range(n_chunks) for g in range(n_groups)]
    groups = range(len(units))
    rows = [slice(n * c, (n + 1) * c) for n, _ in units]
    lanes = [slice(g * gw, (g + 1) * gw) for _, g in units]
    kq = [jnp.concatenate([k_ref[rows[u], lanes[u]], q_ref[rows[u], lanes[u]]], axis=0) for u in groups]
    rhs = [jnp.concatenate([k_ref[rows[u], lanes[u]]] * GROUP, axis=0) * kk_mask for u in groups]
    kkqk = [lax.dot_general(kq[g], rhs[g], (((1,), (1,)), ((), ())), preferred_element_type=F32) for g in groups]
    decay = [jnp.exp(jnp.where(tril, per_head_columns(v_heads, u) - gr_ref[units[u][0], :, lanes[u]], -jnp.inf))
             for u in groups]
    low = [jnp.where(strict, per_head_columns(0, g) * kkqk[g][:c] * decay[g], 0.0) for g in groups]
    for g in groups:
        p_ref[rows[g], lanes[g]] = (kkqk[g][c:] * decay[g]).astype(p_ref.dtype)

    inv = [eye - low[g] for g in groups]
    bd = [block_diag(low[g]) for g in groups]
    power = [jnp.dot(low[g].astype(BF16), bd[g], preferred_element_type=F32) for g in groups]
    span = 2
    while span <= c // 2:
        bd = [block_diag(power[g]) for g in groups]
        if span < c // 2:
            both = [jnp.dot(jnp.concatenate([power[g], inv[g]], axis=0).astype(BF16), bd[g],
                            preferred_element_type=F32) for g in groups]
            power = [both[g][:c] for g in groups]
            inv = [inv[g] + both[g][c:] for g in groups]
        else:
            inv = [inv[g] + jnp.dot(inv[g].astype(BF16), bd[g], preferred_element_type=F32) for g in groups]
        span *= 2
    for g in groups:
        t_ref[rows[g], lanes[g]] = inv[g].astype(t_ref.dtype)


def gdn_solve(q, k, gates, g_rows, v_heads, rep, chunks_per_step):
    m, qk_width = q.shape
    width = v_heads * CHUNK
    assert GROUP * CHUNK == MXU_TILE and rep * HEAD_DIM == MXU_TILE and v_heads % GROUP == 0
    assert qk_width * rep == v_heads * HEAD_DIM and qk_width == width
    rb = lax.broadcasted_iota(jnp.int32, (GROUP * CHUNK, MXU_TILE), 0) // CHUNK
    ln = lax.broadcasted_iota(jnp.int32, (GROUP * CHUNK, MXU_TILE), 1)
    bd_mask = (rb == ln // CHUNK).astype(BF16)
    kk_mask = (rb // rep == ln // HEAD_DIM).astype(BF16)
    kernel = functools.partial(_gdn_solve_kernel, v_heads=v_heads, rep=rep)
    row = lambda i: (i, 0)
    const = lambda i: (0, 0)
    rows = chunks_per_step * CHUNK
    assert m % rows == 0
    return pl.pallas_call(
        kernel,
        grid=(m // rows,),
        in_specs=[pl.BlockSpec((rows, qk_width), row), pl.BlockSpec((rows, qk_width), row),
                  pl.BlockSpec((rows, LANES), row),
                  pl.BlockSpec((chunks_per_step, 1, width), lambda i: (i, 0, 0)),
                  pl.BlockSpec((GROUP * CHUNK, MXU_TILE), const), pl.BlockSpec((GROUP * CHUNK, MXU_TILE), const)],
        out_specs=[pl.BlockSpec((rows, width), row), pl.BlockSpec((rows, width), row)],
        out_shape=[jax.ShapeDtypeStruct((m, width), BF16), jax.ShapeDtypeStruct((m, width), BF16)],
        compiler_params=_params("parallel"),
        name="gdn_solve",
    )(q, k, gates, g_rows, bd_mask, kk_mask)


def _gdn_chunk_kernel(q_ref, k_ref, v_ref, z_ref, g_ref, t_ref, p_ref, on_ref, o_ref, state_ref, *, qk_heads, rep):
    c = CHUNK
    d = HEAD_DIM

    @pl.when(pl.program_id(1) == 0)
    def _():
        state_ref[...] = jnp.zeros(state_ref.shape, F32)

    n_v = qk_heads * rep
    pw = rep * d
    tw = rep * c
    gain = on_ref[...]
    zeros = jnp.zeros((c, d), BF16)
    pairs = range(qk_heads)

    def on_diagonal(parts):
        return jnp.concatenate(
            [jnp.concatenate([parts[r] if s == r else zeros for s in range(rep)], axis=1) for r in range(rep)],
            axis=0)

    def chunk_step(n, state):
        rows = slice(n * c, (n + 1) * c)
        gates = g_ref[rows, :]

        def head_cols(first_lane, p):
            return [gates[:, first_lane + p * rep + r:first_lane + p * rep + r + 1] for r in range(rep)]

        k = [k_ref[rows, p * d:(p + 1) * d] for p in pairs]
        kq = [jnp.concatenate([k[p], q_ref[rows, p * d:(p + 1) * d]], axis=0) for p in pairs]
        kqs = [jnp.dot(kq[p], state[p].astype(BF16), preferred_element_type=F32) for p in pairs]

        beta = [head_cols(0, p) for p in pairs]
        gc = [head_cols(n_v, p) for p in pairs]
        eg = [[jnp.exp(g) for g in gc[p]] for p in pairs]
        rhs = [on_diagonal([(beta[p][r] * (v_ref[rows, p * pw + r * d:p * pw + (r + 1) * d].astype(F32)
                                           - kqs[p][:c, r * d:(r + 1) * d] * eg[p][r])).astype(BF16)
                            for r in range(rep)]) for p in pairs]
        v_new = [jnp.dot(t_ref[rows, p * tw:(p + 1) * tw], rhs[p], preferred_element_type=F32)
                 for p in pairs]
        vn_diag = [on_diagonal([v_new[p][:, r * d:(r + 1) * d].astype(BF16) for r in range(rep)]) for p in pairs]
        intra = [jnp.dot(p_ref[rows, p * tw:(p + 1) * tw], vn_diag[p], preferred_element_type=F32) for p in pairs]

        g_last = [[g[c - 1:c, :] for g in gc[p]] for p in pairs]
        v_dec = [jnp.concatenate([jnp.exp(g_last[p][r] - gc[p][r]) * v_new[p][:, r * d:(r + 1) * d]
                                  for r in range(rep)], axis=1).astype(BF16) for p in pairs]
        kt = [k[p].astype(F32).T.astype(BF16) for p in pairs]
        upd = [jnp.dot(kt[p], v_dec[p], preferred_element_type=F32) for p in pairs]
        scale = [jnp.concatenate([jnp.broadcast_to(jnp.exp(g_last[p][r]), (1, d)) for r in range(rep)], axis=1)
                 for p in pairs]
        new_state = [state[p] * scale[p] + upd[p] for p in pairs]

        for p in pairs:
            for r in range(rep):
                vl = slice(p * pw + r * d, p * pw + (r + 1) * d)
                o = kqs[p][c:, r * d:(r + 1) * d] * eg[p][r] + intra[p][:, r * d:(r + 1) * d]
                on = o * lax.rsqrt(jnp.mean(o * o, axis=-1, keepdims=True) + NORM_EPS) * gain
                zf = z_ref[rows, vl].astype(F32)
                o_ref[rows, vl] = (on * (zf * _sigmoid(zf))).astype(o_ref.dtype)
        return new_state

    state = [state_ref[p] for p in pairs]
    for n in range(g_ref.shape[0] // c):
        state = chunk_step(n, state)
    for p in pairs:
        state_ref[p] = state[p]


def gdn_chunk(q, k, v, z, gates, t_inv, p_intra, o_norm, batch, seq, qk_heads, rep, chunks_per_step):
    m = batch * seq
    nc = seq // CHUNK
    qk_width = qk_heads * HEAD_DIM
    v_width = qk_width * rep
    kernel = functools.partial(_gdn_chunk_kernel, qk_heads=qk_heads, rep=rep)
    rows = chunks_per_step * CHUNK
    assert seq % rows == 0
    nc = seq // rows
    row = lambda b, i: (b * nc + i, 0)
    return pl.pallas_call(
        kernel,
        grid=(batch, nc),
        in_specs=[pl.BlockSpec((rows, qk_width), row), pl.BlockSpec((rows, qk_width), row),
                  pl.BlockSpec((rows, v_width), row), pl.BlockSpec((rows, v_width), row),
                  pl.BlockSpec((rows, LANES), row),
                  pl.BlockSpec((rows, t_inv.shape[1]), row), pl.BlockSpec((rows, p_intra.shape[1]), row),
                  pl.BlockSpec((1, HEAD_DIM), lambda b, i: (0, 0))],
        out_specs=pl.BlockSpec((rows, v_width), row),
        out_shape=jax.ShapeDtypeStruct((m, v_width), BF16),
        scratch_shapes=[pltpu.VMEM((qk_heads, HEAD_DIM, rep * HEAD_DIM), F32)],
        compiler_params=_params("arbitrary", "arbitrary"),
        name="gdn_chunk",
    )(q, k, v, z, gates, t_inv, p_intra, o_norm)


def _pad_cols(w, n):
    return jnp.pad(w, ((0, 0), (0, n - w.shape[1])))


def _row(v):
    return v.reshape(1, -1).astype(F32)


def kernel(x, norm_mix0, w_in0, fgate_bias0, conv_w0, conv_b0, conv_ln_g0, conv_ln_b0, w_out0, norm_mix1, w_in1,
           short_conv1, a_log1, dt_bias1, o_norm1, w_out1, norm_ffn, w_gate, w_up, w_down, final_norm):
    batch, seq, d_model = x.shape
    m = batch * seq
    fox_heads = fgate_bias0.shape[1]
    fox_width = fox_heads * HEAD_DIM
    conv_ch = conv_w0.shape[2]
    gdn_v_heads = a_log1.shape[1]
    gdn_v_width = gdn_v_heads * HEAD_DIM
    gdn_qkv_width = short_conv1.shape[2]
    gdn_qk_width = (gdn_qkv_width - gdn_v_width) // 2
    gdn_qk_heads = gdn_qk_width // HEAD_DIM
    rep = gdn_v_heads // gdn_qk_heads

    h = x.reshape(m, d_model)

    w_gate_b, w_up_b, w_down_b = w_gate.astype(BF16), w_up.astype(BF16), w_down.astype(BF16)
    w_in0_t = jnp.swapaxes(w_in0, 1, 2).astype(BF16)
    w_in1_t = jnp.swapaxes(w_in1, 1, 2).astype(BF16)

    def pad_rows(w, n):
        return jnp.pad(w, ((0, n - w.shape[0]), (0, 0)))

    def ffn(h, layer):
        mid = norm_swiglu(h, _row(norm_ffn[layer]), wview(w_gate_b, layer), wview(w_up_b, layer), bm=1024, bn=512)
        return matmul_residual([mid], [wview(w_down_b, layer)], h, bm=512, bn=1024)

    g0 = _row(norm_mix0[0])
    f0, f1 = 3 * fox_width, 3 * fox_width + fox_heads
    qkv = norm_matmul(h, g0, wview(w_in0_t, 0, cols=f0, transposed=True), BF16, bm=1024, bn=1024)
    f_logit = norm_matmul(h, g0, wview(pad_rows(w_in0_t[0, f0:f1], LANES), transposed=True), F32, bm=1024, bn=LANES)
    glu = norm_matmul(h, g0, wview(w_in0_t[0, f1:], transposed=True), BF16, bm=1024, bn=1024)

    tq = 256
    c = fox_gates(f_logit, _pad_cols(_row(fgate_bias0[0]), LANES), batch, seq)
    c_heads = jnp.transpose(c.reshape(batch, seq, LANES)[:, :, :fox_heads], (0, 2, 1))
    v_t = jnp.transpose(qkv[:, 2 * fox_width:].reshape(batch, seq // tq, tq, fox_width), (0, 1, 3, 2))
    a_out = fox_attention(qkv, v_t, c_heads.reshape(batch, fox_heads, 1, seq),
                          c_heads.reshape(batch, fox_heads, seq, 1), batch, seq, fox_heads, tq,
                          hb=min(8, fox_heads))
    u = conformer_conv(glu, jnp.pad(conv_w0[0], ((0, CONV_HALO - CONV_WIDTH), (0, 0))), _row(conv_b0[0]),
                       _row(conv_ln_g0[0]), _row(conv_ln_b0[0]), batch, seq, ts=256)
    h = matmul_residual([a_out, u], [wview(w_out0, 0, row0=0, rows=fox_width),
                                     wview(w_out0, 0, row0=fox_width, rows=conv_ch)], h, bm=512, bn=1024)
    h = ffn(h, 0)

    g1 = _row(norm_mix1[0])
    z0 = gdn_qkv_width
    qkv1 = norm_matmul(h, g1, wview(w_in1_t, 0, cols=z0, transposed=True), BF16, bm=1024, bn=1024)
    z1 = norm_matmul(h, g1, wview(w_in1_t, 0, col0=z0, cols=gdn_v_width, transposed=True), BF16, bm=1024, bn=1024)
    ba1 = norm_matmul(h, g1, wview(pad_rows(w_in1_t[0, z0 + gdn_v_width:], LANES), transposed=True), F32,
                      bm=1024, bn=LANES)
    alog_pad = jnp.pad(_row(a_log1[0]), ((0, 0), (gdn_v_heads, LANES - 2 * gdn_v_heads)))
    dtb_pad = jnp.pad(_row(dt_bias1[0]), ((0, 0), (gdn_v_heads, LANES - 2 * gdn_v_heads)))
    q1, k1, v1, gates1 = gdn_prep(qkv1, ba1, jnp.pad(short_conv1[0], ((0, 8 - SHORT_CONV), (0, 0))), alog_pad,
                                  dtb_pad, batch, seq, 256, gdn_qk_width, gdn_v_width, gdn_v_heads)
    g_rows = jnp.transpose(gates1[:, gdn_v_heads:2 * gdn_v_heads].reshape(m // CHUNK, CHUNK, gdn_v_heads),
                           (0, 2, 1)).reshape(m // CHUNK, 1, gdn_v_heads * CHUNK)
    t_inv, p_intra = gdn_solve(q1, k1, gates1, g_rows, gdn_v_heads, rep, chunks_per_step=2)
    o1 = gdn_chunk(q1, k1, v1, z1, gates1, t_inv, p_intra, _row(o_norm1[0]), batch, seq, gdn_qk_heads, rep,
                   chunks_per_step=2)
    h = matmul_residual([o1], [wview(w_out1.astype(BF16), 0)], h, bm=512, bn=1024)
    h = ffn(h, 1)

    return rmsnorm(h, _row(final_norm), bm=512).reshape(batch, seq, d_model)
```

```python
import functools
from typing import NamedTuple, Optional

import jax
import jax.numpy as jnp
from jax import lax
from jax.experimental import pallas as pl
from jax.experimental.pallas import tpu as pltpu

F32 = jnp.float32
BF16 = jnp.bfloat16

NORM_EPS = 1e-6
LANES = 128
SUBLANES = 8
MXU_TILE = 256
HEAD_DIM = 128
CHUNK = 64
CONV_WIDTH = 31
CONV_HALO = 32
SHORT_CONV = 4
SHORT_HALO = 8
VMEM_LIMIT_BYTES = 56 * 1024 * 1024


def _params(*sem):
    return pltpu.CompilerParams(dimension_semantics=sem, vmem_limit_bytes=VMEM_LIMIT_BYTES)


def _blk(dim, pref, align):
    best = dim
    for cand in range(align, min(dim, pref) + 1, align):
        if dim % cand == 0:
            best = cand
    return best


def _sigmoid(x):
    return 1.0 / (1.0 + jnp.exp(-x))


def _softplus(x):
    return jnp.maximum(x, 0.0) + jnp.log(1.0 + jnp.exp(-jnp.abs(x)))


def _rms_to_scratch(x_ref, g_ref, xn_ref):
    x = x_ref[...].astype(F32)
    ms = jnp.mean(x * x, axis=-1, keepdims=True)
    xn_ref[...] = (x * lax.rsqrt(ms + NORM_EPS) * g_ref[...]).astype(xn_ref.dtype)


class WeightView(NamedTuple):
    arr: jax.Array
    lead: Optional[int]
    row0: int
    rows: int
    col0: int
    cols: int
    transposed: bool


def wview(arr, lead=None, row0=0, rows=None, col0=0, cols=None, transposed=False):
    k, n = arr.shape[-2:][::-1] if transposed else arr.shape[-2:]
    return WeightView(arr, lead, row0, k - row0 if rows is None else rows, col0, n - col0 if cols is None else cols,
                      transposed)


def _w_spec(wv, bn, col_block):
    assert wv.row0 % wv.rows == 0 and wv.col0 % bn == 0 and wv.cols % bn == 0
    rb, cb = wv.row0 // wv.rows, wv.col0 // bn
    shape = (bn, wv.rows) if wv.transposed else (wv.rows, bn)
    index = (lambda *g: (cb + col_block(*g), rb)) if wv.transposed else (lambda *g: (rb, cb + col_block(*g)))
    if wv.arr.ndim == 3:
        return pl.BlockSpec((None,) + shape, lambda *g: (wv.lead,) + index(*g))
    return pl.BlockSpec(shape, index)


def _dot_w(x, w, transposed):
    dims = (((1,), (1,)), ((), ())) if transposed else (((1,), (0,)), ((), ()))
    return lax.dot_general(x, w.astype(BF16), dims, preferred_element_type=F32)


def _norm_mm_kernel(x_ref, g_ref, w_ref, o_ref, xn_ref, *, transposed):
    @pl.when(pl.program_id(1) == 0)
    def _():
        _rms_to_scratch(x_ref, g_ref, xn_ref)

    o_ref[...] = _dot_w(xn_ref[...], w_ref[...], transposed).astype(o_ref.dtype)


def norm_matmul(x, gain, wv, out_dtype, bm, bn):
    m, k = x.shape
    n = wv.cols
    bm, bn = _blk(m, bm, 8), _blk(n, bn, LANES)
    return pl.pallas_call(
        functools.partial(_norm_mm_kernel, transposed=wv.transposed),
        grid=(m // bm, n // bn),
        in_specs=[pl.BlockSpec((bm, k), lambda i, j: (i, 0)),
                  pl.BlockSpec((1, k), lambda i, j: (0, 0)),
                  _w_spec(wv, bn, lambda i, j: j)],
        out_specs=pl.BlockSpec((bm, bn), lambda i, j: (i, j)),
        out_shape=jax.ShapeDtypeStruct((m, n), out_dtype),
        scratch_shapes=[pltpu.VMEM((bm, k), BF16)],
        compiler_params=_params("parallel", "arbitrary"),
        name="norm_matmul",
    )(x, gain, wv.arr)


def _norm_swiglu_kernel(x_ref, g_ref, wg_ref, wu_ref, o_ref, xn_ref):
    @pl.when(pl.program_id(1) == 0)
    def _():
        _rms_to_scratch(x_ref, g_ref, xn_ref)

    xn = xn_ref[...]
    gate = jnp.dot(xn, wg_ref[...].astype(BF16), preferred_element_type=F32)
    up = jnp.dot(xn, wu_ref[...].astype(BF16), preferred_element_type=F32)
    o_ref[...] = (gate * _sigmoid(gate) * up).astype(o_ref.dtype)


def norm_swiglu(x, gain, wg, wu, bm, bn):
    m, k = x.shape
    n = wg.cols
    bm, bn = _blk(m, bm, 8), _blk(n, bn, LANES)
    return pl.pallas_call(
        _norm_swiglu_kernel,
        grid=(m // bm, n // bn),
        in_specs=[pl.BlockSpec((bm, k), lambda i, j: (i, 0)),
                  pl.BlockSpec((1, k), lambda i, j: (0, 0)),
                  _w_spec(wg, bn, lambda i, j: j),
                  _w_spec(wu, bn, lambda i, j: j)],
        out_specs=pl.BlockSpec((bm, bn), lambda i, j: (i, j)),
        out_shape=jax.ShapeDtypeStruct((m, n), BF16),
        scratch_shapes=[pltpu.VMEM((bm, k), BF16)],
        compiler_params=_params("parallel", "arbitrary"),
        name="norm_swiglu",
    )(x, gain, wg.arr, wu.arr)


def _mm_res_kernel(*refs, n_pairs):
    xs = refs[:n_pairs]
    ws = refs[n_pairs:2 * n_pairs]
    res_ref = refs[2 * n_pairs]
    o_ref = refs[2 * n_pairs + 1]
    wb = refs[2 * n_pairs + 2:]

    @pl.when(pl.program_id(1) == 0)
    def _():
        for w_ref, wb_ref in zip([w for w in ws if w.dtype != BF16], wb):
            wb_ref[...] = w_ref[...].astype(BF16)

    wb = list(wb)
    acc = res_ref[...]
    for x_ref, w_ref in zip(xs, ws):
        w = w_ref[...] if w_ref.dtype == BF16 else wb.pop(0)[...]
        acc = acc + jnp.dot(x_ref[...], w, preferred_element_type=F32)
    o_ref[...] = acc


def matmul_residual(xs, wvs, res, bm, bn):
    m, n = res.shape
    bm, bn = _blk(m, bm, 8), _blk(n, bn, LANES)
    n_pairs = len(xs)
    in_specs = ([pl.BlockSpec((bm, x.shape[1]), lambda j, i: (i, 0)) for x in xs]
                + [_w_spec(wv, bn, lambda j, i: j) for wv in wvs]
                + [pl.BlockSpec((bm, bn), lambda j, i: (i, j))])
    return pl.pallas_call(
        functools.partial(_mm_res_kernel, n_pairs=n_pairs),
        grid=(n // bn, m // bm),
        in_specs=in_specs,
        out_specs=pl.BlockSpec((bm, bn), lambda j, i: (i, j)),
        out_shape=jax.ShapeDtypeStruct((m, n), F32),
        scratch_shapes=[pltpu.VMEM((wv.rows, bn), BF16) for wv in wvs if wv.arr.dtype != BF16],
        compiler_params=_params("parallel", "arbitrary"),
        name="matmul_residual",
    )(*xs, *[wv.arr for wv in wvs], res)


def _mm_res_norm_kernel(x_ref, w_ref, res_ref, g_ref, o_ref, row_ref):
    j = pl.program_id(1)
    nj, _, bn = row_ref.shape
    row_ref[j] = res_ref[...] + jnp.dot(x_ref[...], w_ref[...], preferred_element_type=F32)

    @pl.when(j == nj - 1)
    def _():
        parts = [row_ref[jb] for jb in range(nj)]
        ms = sum(jnp.sum(p * p, axis=-1, keepdims=True) for p in parts) / (nj * bn)
        inv = lax.rsqrt(ms + NORM_EPS)
        for jb in range(nj):
            o_ref[:, jb * bn:(jb + 1) * bn] = parts[jb] * inv * g_ref[:, jb * bn:(jb + 1) * bn]


def matmul_residual_rmsnorm(x, wv, res, gain, bm, bn):
    m, n = res.shape
    bm, bn = _blk(m, bm, 8), _blk(n, bn, LANES)
    assert wv.arr.dtype == BF16 and not wv.transposed
    return pl.pallas_call(
        _mm_res_norm_kernel,
        grid=(m // bm, n // bn),
        in_specs=[pl.BlockSpec((bm, x.shape[1]), lambda i, j: (i, 0)),
                  _w_spec(wv, bn, lambda i, j: j),
                  pl.BlockSpec((bm, bn), lambda i, j: (i, j)),
                  pl.BlockSpec((1, n), lambda i, j: (0, 0))],
        out_specs=pl.BlockSpec((bm, n), lambda i, j: (i, 0)),
        out_shape=jax.ShapeDtypeStruct((m, n), F32),
        scratch_shapes=[pltpu.VMEM((n // bn, bm, bn), F32)],
        compiler_params=_params("parallel", "arbitrary"),
        name="matmul_residual_rmsnorm",
    )(x, wv.arr, res, gain)


def _rmsnorm_kernel(x_ref, g_ref, o_ref):
    x = x_ref[...]
    ms = jnp.mean(x * x, axis=-1, keepdims=True)
    o_ref[...] = x * lax.rsqrt(ms + NORM_EPS) * g_ref[...]


def rmsnorm(x, gain, bm):
    m, k = x.shape
    bm = _blk(m, bm, 8)
    return pl.pallas_call(
        _rmsnorm_kernel,
        grid=(m // bm,),
        in_specs=[pl.BlockSpec((bm, k), lambda i: (i, 0)), pl.BlockSpec((1, k), lambda i: (0, 0))],
        out_specs=pl.BlockSpec((bm, k), lambda i: (i, 0)),
        out_shape=jax.ShapeDtypeStruct((m, k), F32),
        compiler_params=_params("parallel"),
        name="final_rmsnorm",
    )(x, gain)


def _prefix_sum_rows(x, period):
    row = lax.broadcasted_iota(jnp.int32, x.shape, 0) % period
    shift = 1
    while shift < period:
        x = x + jnp.where(row >= shift, pltpu.roll(x, shift, axis=0), 0.0)
        shift *= 2
    return x


def _fox_gates_kernel(f_ref, b_ref, c_ref):
    z = f_ref[...] + b_ref[...]
    log_f = jnp.minimum(z, 0.0) - jnp.log(1.0 + jnp.exp(-jnp.abs(z)))
    c_ref[...] = _prefix_sum_rows(log_f, log_f.shape[0])


def fox_gates(f_logit, f_bias, batch, seq):
    return pl.pallas_call(
        _fox_gates_kernel,
        grid=(batch,),
        in_specs=[pl.BlockSpec((seq, LANES), lambda b: (b, 0)), pl.BlockSpec((1, LANES), lambda b: (0, 0))],
        out_specs=pl.BlockSpec((seq, LANES), lambda b: (b, 0)),
        out_shape=jax.ShapeDtypeStruct((batch * seq, LANES), F32),
        compiler_params=_params("parallel"),
        name="fox_gates",
    )(f_logit, f_bias)


def _fox_attn_kernel(q_ref, k_ref, vt_ref, cq_ref, ck_ref, o_ref, m_ref, l_ref, acc_ref, *, tq, hb, scale):
    i = pl.program_id(2)
    d = HEAD_DIM
    heads = range(hb)
    q = [q_ref[:, h * d:(h + 1) * d] for h in heads]
    cq = [cq_ref[0, h] for h in heads]
    for h in heads:
        m_ref[h] = jnp.full((1, tq), -jnp.inf, F32)
        l_ref[h] = jnp.zeros((1, tq), F32)
        acc_ref[h] = jnp.zeros((d, tq), F32)
    key_after_query = (lax.broadcasted_iota(jnp.int32, (tq, tq), 0) > lax.broadcasted_iota(jnp.int32, (tq, tq), 1))

    def block(j, on_diagonal):
        start = pl.multiple_of(j * tq, tq)
        s = [lax.dot_general(k_ref[pl.ds(start, tq), h * d:(h + 1) * d], q[h], (((1,), (1,)), ((), ())),
                             preferred_element_type=F32) for h in heads]
        x = [s[h] * scale + (cq[h] - ck_ref[0, h, pl.ds(start, tq), :]) for h in heads]
        if on_diagonal:
            x = [jnp.where(key_after_query, -jnp.inf, x[h]) for h in heads]
        m_old = [m_ref[h] for h in heads]
        m_new = [jnp.maximum(m_old[h], jnp.max(x[h], axis=0, keepdims=True)) for h in heads]
        alpha = [jnp.exp(m_old[h] - m_new[h]) for h in heads]
        p = [jnp.exp(x[h] - m_new[h]) for h in heads]
        pv = [jnp.dot(vt_ref[0, j, h * d:(h + 1) * d, :], p[h].astype(BF16), preferred_element_type=F32)
              for h in heads]
        for h in heads:
            m_ref[h] = m_new[h]
            l_ref[h] = alpha[h] * l_ref[h] + jnp.sum(p[h], axis=0, keepdims=True)
            acc_ref[h] = alpha[h] * acc_ref[h] + pv[h]

    def body(j, carry):
        block(j, False)
        return carry

    lax.fori_loop(0, i, body, 0)
    block(i, True)
    for h in heads:
        o_ref[:, h * d:(h + 1) * d] = (acc_ref[h] / l_ref[h]).T.astype(o_ref.dtype)


def fox_attention(qkv, v_t, c_row, c_col, batch, seq, heads, tq, hb):
    nq = seq // tq
    d = HEAD_DIM
    groups = heads // hb
    kernel = functools.partial(_fox_attn_kernel, tq=tq, hb=hb, scale=d ** -0.5)
    return pl.pallas_call(
        kernel,
        grid=(batch, groups, nq),
        in_specs=[pl.BlockSpec((tq, hb * d), lambda b, g, i: (b * nq + i, g)),
                  pl.BlockSpec((seq, hb * d), lambda b, g, i: (b, groups + g)),
                  pl.BlockSpec((1, nq, hb * d, tq), lambda b, g, i: (b, 0, g, 0)),
                  pl.BlockSpec((1, hb, 1, tq), lambda b, g, i: (b, g, 0, i)),
                  pl.BlockSpec((1, hb, seq, 1), lambda b, g, i: (b, g, 0, 0))],
        out_specs=pl.BlockSpec((tq, hb * d), lambda b, g, i: (b * nq + i, g)),
        out_shape=jax.ShapeDtypeStruct((batch * seq, heads * d), BF16),
        scratch_shapes=[pltpu.VMEM((hb, 1, tq), F32), pltpu.VMEM((hb, 1, tq), F32), pltpu.VMEM((hb, d, tq), F32)],
        compiler_params=_params("parallel", "parallel", "arbitrary"),
        name="fox_attention",
    )(qkv, qkv, v_t, c_row, c_col)


def _conformer_kernel(val_ref, gate_ref, w_ref, cb_ref, lg_ref, lb_ref, o_ref, hist_ref, shift_ref, y_ref,
                      *, ts, rows):
    channels = val_ref.shape[1]

    @pl.when(pl.program_id(1) == 0)
    def _():
        hist_ref[0:CONV_HALO, :] = jnp.zeros((CONV_HALO, channels), F32)

    hist_ref[CONV_HALO:CONV_HALO + ts, :] = val_ref[...].astype(F32) * _sigmoid(gate_ref[...].astype(F32))

    base = CONV_HALO - (CONV_WIDTH - 1)
    hist_rows = ts + CONV_HALO
    for cb in range(channels // LANES):
        lanes = slice(cb * LANES, (cb + 1) * LANES)
        col = hist_ref[:, lanes]
        for b in range(1, SUBLANES):
            shift_ref[b - 1, :, lanes] = pltpu.roll(col, hist_rows - b, axis=0)
        for rb in range(ts // rows):
            acc = jnp.zeros((rows, LANES), F32)
            for j in range(CONV_WIDTH):
                a, b = divmod(base + j, SUBLANES)
                r0 = rb * rows + a * SUBLANES
                src = hist_ref[r0:r0 + rows, lanes] if b == 0 else shift_ref[b - 1, r0:r0 + rows, lanes]
                acc = acc + w_ref[j:j + 1, lanes] * src
            y_ref[rb * rows:(rb + 1) * rows, lanes] = acc + cb_ref[:, lanes]

    hist_ref[0:CONV_HALO, :] = hist_ref[ts:ts + CONV_HALO, :]

    y = y_ref[...]
    mu = jnp.mean(y, axis=-1, keepdims=True)
    yc = y - mu
    var = jnp.mean(yc * yc, axis=-1, keepdims=True)
    yn = yc * lax.rsqrt(var + NORM_EPS) * lg_ref[...] + lb_ref[...]
    o_ref[...] = (yn * _sigmoid(yn)).astype(o_ref.dtype)


def conformer_conv(glu, glu_col0, conv_w, conv_b, ln_g, ln_b, batch, seq, ts):
    channels = conv_w.shape[1]
    assert glu_col0 % channels == 0
    cb = glu_col0 // channels
    ns = seq // ts
    kernel = functools.partial(_conformer_kernel, ts=ts, rows=64)
    vec = lambda: pl.BlockSpec((1, channels), lambda b, i: (0, 0))
    return pl.pallas_call(
        kernel,
        grid=(batch, ns),
        in_specs=[pl.BlockSpec((ts, channels), lambda b, i: (b * ns + i, cb)),
                  pl.BlockSpec((ts, channels), lambda b, i: (b * ns + i, cb + 1)),
                  pl.BlockSpec((conv_w.shape[0], channels), lambda b, i: (0, 0)),
                  vec(), vec(), vec()],
        out_specs=pl.BlockSpec((ts, channels), lambda b, i: (b * ns + i, 0)),
        out_shape=jax.ShapeDtypeStruct((batch * seq, channels), BF16),
        scratch_shapes=[pltpu.VMEM((ts + CONV_HALO, channels), F32),
                        pltpu.VMEM((SUBLANES - 1, ts + CONV_HALO, channels), F32),
                        pltpu.VMEM((ts, channels), F32)],
        compiler_params=_params("arbitrary", "arbitrary"),
        name="conformer_conv",
    )(glu, glu, conv_w, conv_b, ln_g, ln_b)


def _gdn_qkv_kernel(x_ref, g_ref, w_ref, cw_ref, o_ref, xn_ref, hist_ref, halo_ref, *, bm, blocks_per_seq, qk_blocks):
    i = pl.program_id(0)
    j = pl.program_id(1)
    bn = o_ref.shape[1]

    @pl.when(jnp.logical_and(i == 0, j == 0))
    def _():
        halo_ref[...] = jnp.zeros(halo_ref.shape, F32)

    @pl.when(j == 0)
    def _():
        _rms_to_scratch(x_ref, g_ref, xn_ref)

    starts_sequence = (i % blocks_per_seq) == 0
    hist_ref[0:SHORT_HALO, :] = jnp.where(starts_sequence, 0.0, halo_ref[j])

    use_norm = j < 2 * qk_blocks
    out_scale = jnp.where(j < qk_blocks, HEAD_DIM ** -0.5, 1.0)
    base = SHORT_HALO - (SHORT_CONV - 1)
    hist_rows = bm + SHORT_HALO

    def project(s):
        cols = slice(s * MXU_TILE, (s + 1) * MXU_TILE)
        hist_ref[SHORT_HALO:SHORT_HALO + bm, cols] = _dot_w(xn_ref[...], w_ref[cols, :], True)

    def conv_slab(s):
        for hb in range(s * MXU_TILE // HEAD_DIM, (s + 1) * MXU_TILE // HEAD_DIM):
            lanes = slice(hb * HEAD_DIM, (hb + 1) * HEAD_DIM)
            col = hist_ref[:, lanes]
            acc = jnp.zeros((bm, HEAD_DIM), F32)
            for tap in range(SHORT_CONV):
                ofs = base + tap
                src = col[ofs:ofs + bm] if ofs % SUBLANES == 0 else pltpu.roll(col, hist_rows - ofs, axis=0)[0:bm]
                acc = acc + cw_ref[tap:tap + 1, lanes] * src
            y = acc * _sigmoid(acc)
            inv_norm = lax.rsqrt(jnp.sum(y * y, axis=-1, keepdims=True) + NORM_EPS)
            o_ref[:, lanes] = (y * (jnp.where(use_norm, inv_norm, 1.0) * out_scale)).astype(o_ref.dtype)

    n_slabs = bn // MXU_TILE
    project(0)
    for s in range(n_slabs):
        if s + 1 < n_slabs:
            project(s + 1)
        conv_slab(s)
    halo_ref[j] = hist_ref[bm:bm + SHORT_HALO, :]


def gdn_qkv_proj(x, gain, wv, short_conv, seq, qk_width, bm, bn):
    m, k = x.shape
    n = wv.cols
    bm, bn = _blk(seq, bm, 8), _blk(qk_width, bn, LANES)
    assert wv.transposed and seq % bm == 0 and n % bn == 0
    kernel = functools.partial(_gdn_qkv_kernel, bm=bm, blocks_per_seq=seq // bm, qk_blocks=qk_width // bn)
    return pl.pallas_call(
        kernel,
        grid=(m // bm, n // bn),
        in_specs=[pl.BlockSpec((bm, k), lambda i, j: (i, 0)),
                  pl.BlockSpec((1, k), lambda i, j: (0, 0)),
                  _w_spec(wv, bn, lambda i, j: j),
                  pl.BlockSpec((short_conv.shape[0], bn), lambda i, j: (0, j))],
        out_specs=pl.BlockSpec((bm, bn), lambda i, j: (i, j)),
        out_shape=jax.ShapeDtypeStruct((m, n), BF16),
        scratch_shapes=[pltpu.VMEM((bm, k), BF16), pltpu.VMEM((bm + SHORT_HALO, bn), F32),
                        pltpu.VMEM((n // bn, SHORT_HALO, bn), F32)],
        compiler_params=_params("arbitrary", "arbitrary"),
        name="gdn_qkv_proj",
    )(x, gain, wv.arr, short_conv)


def _gdn_gates_kernel(ba_ref, alog_ref, dtb_ref, g_ref, *, gate_heads):
    ba = ba_ref[...]
    beta = _sigmoid(ba)
    g = -jnp.exp(alog_ref[...]) * _softplus(ba + dtb_ref[...])
    gcum = _prefix_sum_rows(g, CHUNK)
    lane = lax.broadcasted_iota(jnp.int32, ba.shape, 1)
    g_ref[...] = jnp.where(lane < gate_heads, beta, gcum)


def gdn_gates(ba, alog_pad, dtb_pad, gate_heads, ts):
    m = ba.shape[0]
    ts = _blk(m, ts, CHUNK)
    row = lambda i: (i, 0)
    const = lambda i: (0, 0)
    return pl.pallas_call(
        functools.partial(_gdn_gates_kernel, gate_heads=gate_heads),
        grid=(m // ts,),
        in_specs=[pl.BlockSpec((ts, LANES), row), pl.BlockSpec((1, LANES), const), pl.BlockSpec((1, LANES), const)],
        out_specs=pl.BlockSpec((ts, LANES), row),
        out_shape=jax.ShapeDtypeStruct((m, LANES), F32),
        compiler_params=_params("parallel"),
        name="gdn_gates",
    )(ba, alog_pad, dtb_pad)


def _gdn_prep_kernel(x_ref, ba_ref, w_ref, alog_ref, dtb_ref, q_ref, k_ref, v_ref, g_ref, hist_ref,
                     *, ts, qk_width, gate_heads):
    width = x_ref.shape[1]

    @pl.when(pl.program_id(1) == 0)
    def _():
        hist_ref[0:SHORT_HALO, :] = jnp.zeros((SHORT_HALO, width), F32)

    hist_ref[SHORT_HALO:SHORT_HALO + ts, :] = x_ref[...].astype(F32)
    base = SHORT_HALO - (SHORT_CONV - 1)
    hist_rows = ts + SHORT_HALO
    n_qk_heads = qk_width // HEAD_DIM
    for hb in range(width // HEAD_DIM):
        lanes = slice(hb * HEAD_DIM, (hb + 1) * HEAD_DIM)
        col = hist_ref[:, lanes]
        acc = jnp.zeros((ts, HEAD_DIM), F32)
        for j in range(SHORT_CONV):
            ofs = base + j
            src = col[ofs:ofs + ts] if ofs % SUBLANES == 0 else pltpu.roll(col, hist_rows - ofs, axis=0)[0:ts]
            acc = acc + w_ref[j:j + 1, lanes] * src
        y = acc * _sigmoid(acc)
        if hb < 2 * n_qk_heads:
            y = y * lax.rsqrt(jnp.sum(y * y, axis=-1, keepdims=True) + NORM_EPS)
            if hb < n_qk_heads:
                q_ref[:, lanes] = (y * HEAD_DIM ** -0.5).astype(q_ref.dtype)
            else:
                k_ref[:, (hb - n_qk_heads) * HEAD_DIM:(hb - n_qk_heads + 1) * HEAD_DIM] = y.astype(k_ref.dtype)
        else:
            c0 = (hb - 2 * n_qk_heads) * HEAD_DIM
            v_ref[:, c0:c0 + HEAD_DIM] = y.astype(v_ref.dtype)
    hist_ref[0:SHORT_HALO, :] = hist_ref[ts:ts + SHORT_HALO, :]

    ba = ba_ref[...]
    beta = _sigmoid(ba)
    g = -jnp.exp(alog_ref[...]) * _softplus(ba + dtb_ref[...])
    gcum = _prefix_sum_rows(g, CHUNK)
    lane = lax.broadcasted_iota(jnp.int32, ba.shape, 1)
    g_ref[...] = jnp.where(lane < gate_heads, beta, gcum)


def gdn_prep(qkv, ba, short_conv, alog_pad, dtb_pad, batch, seq, ts, qk_width, v_width, gate_heads):
    width = qkv.shape[1]
    ns = seq // ts
    m = batch * seq
    kernel = functools.partial(_gdn_prep_kernel, ts=ts, qk_width=qk_width, gate_heads=gate_heads)
    row = lambda b, i: (b * ns + i, 0)
    const = lambda b, i: (0, 0)
    return pl.pallas_call(
        kernel,
        grid=(batch, ns),
        in_specs=[pl.BlockSpec((ts, width), row),
                  pl.BlockSpec((ts, LANES), row),
                  pl.BlockSpec((short_conv.shape[0], width), const),
                  pl.BlockSpec((1, LANES), const),
                  pl.BlockSpec((1, LANES), const)],
        out_specs=[pl.BlockSpec((ts, qk_width), row), pl.BlockSpec((ts, qk_width), row),
                   pl.BlockSpec((ts, v_width), row), pl.BlockSpec((ts, LANES), row)],
        out_shape=[jax.ShapeDtypeStruct((m, qk_width), BF16), jax.ShapeDtypeStruct((m, qk_width), BF16),
                   jax.ShapeDtypeStruct((m, v_width), BF16), jax.ShapeDtypeStruct((m, LANES), F32)],
        scratch_shapes=[pltpu.VMEM((ts + SHORT_HALO, width), F32)],
        compiler_params=_params("arbitrary", "arbitrary"),
        name="gdn_prep",
    )(qkv, ba, short_conv, alog_pad, dtb_pad)


GROUP = MXU_TILE // CHUNK


def _gdn_solve_kernel(q_ref, k_ref, g_ref, gr_ref, bdm_ref, kkm_ref, t_ref, p_ref, *, v_heads, rep):
    c = CHUNK
    gw = MXU_TILE
    n_groups = v_heads // GROUP
    n_chunks = g_ref.shape[0] // c
    row = lax.broadcasted_iota(jnp.int32, (c, gw), 0)
    lane = lax.broadcasted_iota(jnp.int32, (c, gw), 1)
    blk = lane // c
    col = lane % c
    tril = row >= col
    strict = row > col
    eye = (row == col).astype(F32)
    bd_mask = bdm_ref[...]
    kk_mask = kkm_ref[...]

    def block_diag(x):
        xb = x.astype(BF16)
        return jnp.concatenate([xb] * GROUP, axis=0) * bd_mask

    gates = [g_ref[n * c:(n + 1) * c, :] for n in range(n_chunks)]

    def per_head_columns(first_lane, u):
        n, g = units[u]
        out = gates[n][:, first_lane + GROUP * g + GROUP - 1:first_lane + GROUP * g + GROUP]
        for mth in range(GROUP - 2, -1, -1):
            cm = gates[n][:, first_lane + GROUP * g + mth:first_lane + GROUP * g + mth + 1]
            out = jnp.where(blk == mth, cm, out)
        return out

    units = [(n, g) for n in range(n_chunks) for g in range(n_groups)]
    groups = range(len(units))
    rows = [slice(n * c, (n + 1) * c) for n, _ in units]
    lanes = [slice(g * gw, (g + 1) * gw) for _, g in units]
    kq = [jnp.concatenate([k_ref[rows[u], lanes[u]], q_ref[rows[u], lanes[u]]], axis=0) for u in groups]
    rhs = [jnp.concatenate([k_ref[rows[u], lanes[u]]] * GROUP, axis=0) * kk_mask for u in groups]
    kkqk = [lax.dot_general(kq[g], rhs[g], (((1,), (1,)), ((), ())), preferred_element_type=F32) for g in groups]
    decay = [jnp.exp(jnp.where(tril, per_head_columns(v_heads, u) - gr_ref[units[u][0], :, lanes[u]], -jnp.inf))
             for u in groups]
    low = [jnp.where(strict, per_head_columns(0, g) * kkqk[g][:c] * decay[g], 0.0) for g in groups]
    for g in groups:
        p_ref[rows[g], lanes[g]] = (kkqk[g][c:] * decay[g]).astype(p_ref.dtype)

    inv = [eye - low[g] for g in groups]
    bd = [block_diag(low[g]) for g in groups]
    power = [jnp.dot(low[g].astype(BF16), bd[g], preferred_element_type=F32) for g in groups]
    span = 2
    while span <= c // 2:
        bd = [block_diag(power[g]) for g in groups]
        if span < c // 2:
            both = [jnp.dot(jnp.concatenate([power[g], inv[g]], axis=0).astype(BF16), bd[g],
                            preferred_element_type=F32) for g in groups]
            power = [both[g][:c] for g in groups]
            inv = [inv[g] + both[g][c:] for g in groups]
        else:
            inv = [inv[g] + jnp.dot(inv[g].astype(BF16), bd[g], preferred_element_type=F32) for g in groups]
        span *= 2
    for g in groups:
        t_ref[rows[g], lanes[g]] = inv[g].astype(t_ref.dtype)


def gdn_solve(qkv, gates, g_rows, v_heads, rep, chunks_per_step):
    m = qkv.shape[0]
    width = v_heads * CHUNK
    qk_width = v_heads * HEAD_DIM // rep
    assert GROUP * CHUNK == MXU_TILE and rep * HEAD_DIM == MXU_TILE and v_heads % GROUP == 0
    assert qk_width * rep == v_heads * HEAD_DIM and qk_width == width
    rb = lax.broadcasted_iota(jnp.int32, (GROUP * CHUNK, MXU_TILE), 0) // CHUNK
    ln = lax.broadcasted_iota(jnp.int32, (GROUP * CHUNK, MXU_TILE), 1)
    bd_mask = (rb == ln // CHUNK).astype(BF16)
    kk_mask = (rb // rep == ln // HEAD_DIM).astype(BF16)
    kernel = functools.partial(_gdn_solve_kernel, v_heads=v_heads, rep=rep)
    row = lambda i: (i, 0)
    const = lambda i: (0, 0)
    rows = chunks_per_step * CHUNK
    assert m % rows == 0
    return pl.pallas_call(
        kernel,
        grid=(m // rows,),
        in_specs=[pl.BlockSpec((rows, qk_width), row), pl.BlockSpec((rows, qk_width), lambda i: (i, 1)),
                  pl.BlockSpec((rows, LANES), row),
                  pl.BlockSpec((chunks_per_step, 1, width), lambda i: (i, 0, 0)),
                  pl.BlockSpec((GROUP * CHUNK, MXU_TILE), const), pl.BlockSpec((GROUP * CHUNK, MXU_TILE), const)],
        out_specs=[pl.BlockSpec((rows, width), row), pl.BlockSpec((rows, width), row)],
        out_shape=[jax.ShapeDtypeStruct((m, width), BF16), jax.ShapeDtypeStruct((m, width), BF16)],
        compiler_params=_params("parallel"),
        name="gdn_solve",
    )(qkv, qkv, gates, g_rows, bd_mask, kk_mask)


def _gdn_chunk_kernel(q_ref, k_ref, v_ref, z_ref, g_ref, t_ref, p_ref, on_ref, o_ref, state_ref, *, qk_heads, rep):
    c = CHUNK
    d = HEAD_DIM

    @pl.when(pl.program_id(1) == 0)
    def _():
        state_ref[...] = jnp.zeros(state_ref.shape, F32)

    n_v = qk_heads * rep
    pw = rep * d
    tw = rep * c
    gain = on_ref[...]
    zeros = jnp.zeros((c, d), BF16)
    pairs = range(qk_heads)

    def on_diagonal(parts):
        return jnp.concatenate(
            [jnp.concatenate([parts[r] if s == r else zeros for s in range(rep)], axis=1) for r in range(rep)],
            axis=0)

    def chunk_step(n, state):
        rows = slice(n * c, (n + 1) * c)
        gates = g_ref[rows, :]

        def head_cols(first_lane, p):
            return [gates[:, first_lane + p * rep + r:first_lane + p * rep + r + 1] for r in range(rep)]

        k = [k_ref[rows, p * d:(p + 1) * d] for p in pairs]
        kq = [jnp.concatenate([k[p], q_ref[rows, p * d:(p + 1) * d]], axis=0) for p in pairs]
        kqs = [jnp.dot(kq[p], state[p].astype(BF16), preferred_element_type=F32) for p in pairs]

        beta = [head_cols(0, p) for p in pairs]
        gc = [head_cols(n_v, p) for p in pairs]
        eg = [[jnp.exp(g) for g in gc[p]] for p in pairs]
        rhs = [on_diagonal([(beta[p][r] * (v_ref[rows, p * pw + r * d:p * pw + (r + 1) * d].astype(F32)
                                           - kqs[p][:c, r * d:(r + 1) * d] * eg[p][r])).astype(BF16)
                            for r in range(rep)]) for p in pairs]
        v_new = [jnp.dot(t_ref[rows, p * tw:(p + 1) * tw], rhs[p], preferred_element_type=F32)
                 for p in pairs]
        vn_diag = [on_diagonal([v_new[p][:, r * d:(r + 1) * d].astype(BF16) for r in range(rep)]) for p in pairs]
        intra = [jnp.dot(p_ref[rows, p * tw:(p + 1) * tw], vn_diag[p], preferred_element_type=F32) for p in pairs]

        g_last = [[g[c - 1:c, :] for g in gc[p]] for p in pairs]
        v_dec = [jnp.concatenate([jnp.exp(g_last[p][r] - gc[p][r]) * v_new[p][:, r * d:(r + 1) * d]
                                  for r in range(rep)], axis=1).astype(BF16) for p in pairs]
        kt = [k[p].astype(F32).T.astype(BF16) for p in pairs]
        upd = [jnp.dot(kt[p], v_dec[p], preferred_element_type=F32) for p in pairs]
        scale = [jnp.concatenate([jnp.broadcast_to(jnp.exp(g_last[p][r]), (1, d)) for r in range(rep)], axis=1)
                 for p in pairs]
        new_state = [state[p] * scale[p] + upd[p] for p in pairs]

        for p in pairs:
            for r in range(rep):
                vl = slice(p * pw + r * d, p * pw + (r + 1) * d)
                o = kqs[p][c:, r * d:(r + 1) * d] * eg[p][r] + intra[p][:, r * d:(r + 1) * d]
                on = o * lax.rsqrt(jnp.mean(o * o, axis=-1, keepdims=True) + NORM_EPS) * gain
                zf = z_ref[rows, vl].astype(F32)
                o_ref[rows, vl] = (on * (zf * _sigmoid(zf))).astype(o_ref.dtype)
        return new_state

    state = [state_ref[p] for p in pairs]
    for n in range(g_ref.shape[0] // c):
        state = chunk_step(n, state)
    for p in pairs:
        state_ref[p] = state[p]


def gdn_chunk(qkv, z, gates, t_inv, p_intra, o_norm, batch, seq, qk_heads, rep, chunks_per_step):
    m = batch * seq
    qk_width = qk_heads * HEAD_DIM
    v_width = qk_width * rep
    assert v_width == 2 * qk_width
    kernel = functools.partial(_gdn_chunk_kernel, qk_heads=qk_heads, rep=rep)
    rows = chunks_per_step * CHUNK
    assert seq % rows == 0
    nc = seq // rows
    row = lambda b, i: (b * nc + i, 0)
    col1 = lambda b, i: (b * nc + i, 1)
    return pl.pallas_call(
        kernel,
        grid=(batch, nc),
        in_specs=[pl.BlockSpec((rows, qk_width), row), pl.BlockSpec((rows, qk_width), col1),
                  pl.BlockSpec((rows, v_width), col1), pl.BlockSpec((rows, v_width), row),
                  pl.BlockSpec((rows, LANES), row),
                  pl.BlockSpec((rows, t_inv.shape[1]), row), pl.BlockSpec((rows, p_intra.shape[1]), row),
                  pl.BlockSpec((1, HEAD_DIM), lambda b, i: (0, 0))],
        out_specs=pl.BlockSpec((rows, v_width), row),
        out_shape=jax.ShapeDtypeStruct((m, v_width), BF16),
        scratch_shapes=[pltpu.VMEM((qk_heads, HEAD_DIM, rep * HEAD_DIM), F32)],
        compiler_params=_params("arbitrary", "arbitrary"),
        name="gdn_chunk",
    )(qkv, qkv, qkv, z, gates, t_inv, p_intra, o_norm)


def _pad_cols(w, n):
    return jnp.pad(w, ((0, 0), (0, n - w.shape[1])))


def _row(v):
    return v.reshape(1, -1).astype(F32)


def kernel(x, norm_mix0, w_in0, fgate_bias0, conv_w0, conv_b0, conv_ln_g0, conv_ln_b0, w_out0, norm_mix1, w_in1,
           short_conv1, a_log1, dt_bias1, o_norm1, w_out1, norm_ffn, w_gate, w_up, w_down, final_norm):
    batch, seq, d_model = x.shape
    m = batch * seq
    fox_heads = fgate_bias0.shape[1]
    fox_width = fox_heads * HEAD_DIM
    conv_ch = conv_w0.shape[2]
    gdn_v_heads = a_log1.shape[1]
    gdn_v_width = gdn_v_heads * HEAD_DIM
    gdn_qkv_width = short_conv1.shape[2]
    gdn_qk_width = (gdn_qkv_width - gdn_v_width) // 2
    gdn_qk_heads = gdn_qk_width // HEAD_DIM
    rep = gdn_v_heads // gdn_qk_heads

    h = x.reshape(m, d_model)

    w_gate_b, w_up_b, w_down_b = w_gate.astype(BF16), w_up.astype(BF16), w_down.astype(BF16)
    w_in0_t = jnp.swapaxes(w_in0, 1, 2).astype(BF16)
    w_in1_t = jnp.swapaxes(w_in1, 1, 2).astype(BF16)

    def pad_rows(w, n):
        return jnp.pad(w, ((0, n - w.shape[0]), (0, 0)))

    def ffn(h, layer, final_gain=None):
        mid = norm_swiglu(h, _row(norm_ffn[layer]), wview(w_gate_b, layer), wview(w_up_b, layer), bm=1024, bn=512)
        if final_gain is not None:
            return matmul_residual_rmsnorm(mid, wview(w_down_b, layer), h, final_gain, bm=512, bn=1024)
        return matmul_residual([mid], [wview(w_down_b, layer)], h, bm=512, bn=1024)

    g0 = _row(norm_mix0[0])
    f0, f1 = 3 * fox_width, 3 * fox_width + fox_heads
    w_qkvg = jnp.concatenate([w_in0_t[0, :f0], w_in0_t[0, f1:]], axis=0)
    qkv = norm_matmul(h, g0, wview(w_qkvg, transposed=True), BF16, bm=1024, bn=1024)
    f_logit = norm_matmul(h, g0, wview(pad_rows(w_in0_t[0, f0:f1], LANES), transposed=True), F32, bm=1024, bn=LANES)

    tq = 256
    c = fox_gates(f_logit, _pad_cols(_row(fgate_bias0[0]), LANES), batch, seq)
    c_heads = jnp.transpose(c.reshape(batch, seq, LANES)[:, :, :fox_heads], (0, 2, 1))
    v_t = jnp.transpose(qkv[:, 2 * fox_width:f0].reshape(batch, seq // tq, tq, fox_width), (0, 1, 3, 2))
    a_out = fox_attention(qkv, v_t, c_heads.reshape(batch, fox_heads, 1, seq),
                          c_heads.reshape(batch, fox_heads, seq, 1), batch, seq, fox_heads, tq,
                          hb=min(8, fox_heads))
    u = conformer_conv(qkv, f0, jnp.pad(conv_w0[0], ((0, CONV_HALO - CONV_WIDTH), (0, 0))), _row(conv_b0[0]),
                       _row(conv_ln_g0[0]), _row(conv_ln_b0[0]), batch, seq, ts=256)
    h = matmul_residual([a_out, u], [wview(w_out0, 0, row0=0, rows=fox_width),
                                     wview(w_out0, 0, row0=fox_width, rows=conv_ch)], h, bm=512, bn=1024)
    h = ffn(h, 0)

    g1 = _row(norm_mix1[0])
    z0 = gdn_qkv_width
    qkv1 = gdn_qkv_proj(h, g1, wview(w_in1_t, 0, cols=z0, transposed=True),
                        jnp.pad(short_conv1[0], ((0, SUBLANES - SHORT_CONV), (0, 0))), seq, gdn_qk_width,
                        bm=1024, bn=1024)
    z1 = norm_matmul(h, g1, wview(w_in1_t, 0, col0=z0, cols=gdn_v_width, transposed=True), BF16, bm=1024, bn=1024)
    ba1 = norm_matmul(h, g1, wview(pad_rows(w_in1_t[0, z0 + gdn_v_width:], LANES), transposed=True), F32,
                      bm=1024, bn=LANES)
    alog_pad = jnp.pad(_row(a_log1[0]), ((0, 0), (gdn_v_heads, LANES - 2 * gdn_v_heads)))
    dtb_pad = jnp.pad(_row(dt_bias1[0]), ((0, 0), (gdn_v_heads, LANES - 2 * gdn_v_heads)))
    gates1 = gdn_gates(ba1, alog_pad, dtb_pad, gdn_v_heads, ts=1024)
    g_rows = jnp.transpose(gates1[:, gdn_v_heads:2 * gdn_v_heads].reshape(m // CHUNK, CHUNK, gdn_v_heads),
                           (0, 2, 1)).reshape(m // CHUNK, 1, gdn_v_heads * CHUNK)
    t_inv, p_intra = gdn_solve(qkv1, gates1, g_rows, gdn_v_heads, rep, chunks_per_step=2)
    o1 = gdn_chunk(qkv1, z1, gates1, t_inv, p_intra, _row(o_norm1[0]), batch, seq, gdn_qk_heads, rep,
                   chunks_per_step=2)
    h = matmul_residual([o1], [wview(w_out1.astype(BF16), 0)], h, bm=512, bn=1024)
    return ffn(h, 1, final_gain=_row(final_norm)).reshape(batch, seq, d_model)
```

```python
import functools
from typing import NamedTuple, Optional

import jax
import jax.numpy as jnp
from jax import lax
from jax.experimental import pallas as pl
from jax.experimental.pallas import tpu as pltpu

F32 = jnp.float32
BF16 = jnp.bfloat16

NORM_EPS = 1e-6
LANES = 128
SUBLANES = 8
MXU_TILE = 256
HEAD_DIM = 128
CHUNK = 64
CONV_WIDTH = 31
CONV_HALO = 32
SHORT_CONV = 4
SHORT_HALO = 8
VMEM_LIMIT_BYTES = 56 * 1024 * 1024


def _params(*sem):
    return pltpu.CompilerParams(dimension_semantics=sem, vmem_limit_bytes=VMEM_LIMIT_BYTES)


def _blk(dim, pref, align):
    best = dim
    for cand in range(align, min(dim, pref) + 1, align):
        if dim % cand == 0:
            best = cand
    return best


def _sigmoid(x):
    return 1.0 / (1.0 + jnp.exp(-x))


def _softplus(x):
    return jnp.maximum(x, 0.0) + jnp.log(1.0 + jnp.exp(-jnp.abs(x)))


def _rms_to_scratch(x_ref, g_ref, xn_ref):
    x = x_ref[...].astype(F32)
    ms = jnp.mean(x * x, axis=-1, keepdims=True)
    xn_ref[...] = (x * lax.rsqrt(ms + NORM_EPS) * g_ref[...]).astype(xn_ref.dtype)


class WeightView(NamedTuple):
    arr: jax.Array
    lead: Optional[int]
    row0: int
    rows: int
    col0: int
    cols: int
    transposed: bool


def wview(arr, lead=None, row0=0, rows=None, col0=0, cols=None, transposed=False):
    k, n = arr.shape[-2:][::-1] if transposed else arr.shape[-2:]
    return WeightView(arr, lead, row0, k - row0 if rows is None else rows, col0, n - col0 if cols is None else cols,
                      transposed)


def _w_spec(wv, bn, col_block):
    assert wv.row0 % wv.rows == 0 and wv.col0 % bn == 0 and wv.cols % bn == 0
    rb, cb = wv.row0 // wv.rows, wv.col0 // bn
    shape = (bn, wv.rows) if wv.transposed else (wv.rows, bn)
    index = (lambda *g: (cb + col_block(*g), rb)) if wv.transposed else (lambda *g: (rb, cb + col_block(*g)))
    if wv.arr.ndim == 3:
        return pl.BlockSpec((None,) + shape, lambda *g: (wv.lead,) + index(*g))
    return pl.BlockSpec(shape, index)


def _dot_w(x, w, transposed):
    dims = (((1,), (1,)), ((), ())) if transposed else (((1,), (0,)), ((), ()))
    return lax.dot_general(x, w.astype(BF16), dims, preferred_element_type=F32)


def _norm_mm_kernel(x_ref, g_ref, w_ref, wl_ref, o_ref, logit_ref, xn_ref, *, transposed):
    @pl.when(pl.program_id(1) == 0)
    def _():
        _rms_to_scratch(x_ref, g_ref, xn_ref)
        logit_ref[...] = _dot_w(xn_ref[...], wl_ref[...], True)

    o_ref[...] = _dot_w(xn_ref[...], w_ref[...], transposed).astype(o_ref.dtype)


def norm_matmul(x, gain, wv, w_logit_t, bm, bn):
    m, k = x.shape
    n = wv.cols
    bm, bn = _blk(m, bm, 8), _blk(n, bn, LANES)
    return pl.pallas_call(
        functools.partial(_norm_mm_kernel, transposed=wv.transposed),
        grid=(m // bm, n // bn),
        in_specs=[pl.BlockSpec((bm, k), lambda i, j: (i, 0)),
                  pl.BlockSpec((1, k), lambda i, j: (0, 0)),
                  _w_spec(wv, bn, lambda i, j: j),
                  pl.BlockSpec((LANES, k), lambda i, j: (0, 0))],
        out_specs=[pl.BlockSpec((bm, bn), lambda i, j: (i, j)), pl.BlockSpec((bm, LANES), lambda i, j: (i, 0))],
        out_shape=[jax.ShapeDtypeStruct((m, n), BF16), jax.ShapeDtypeStruct((m, LANES), F32)],
        scratch_shapes=[pltpu.VMEM((bm, k), BF16)],
        compiler_params=_params("parallel", "arbitrary"),
        name="norm_matmul",
    )(x, gain, wv.arr, w_logit_t)


def _norm_swiglu_kernel(x_ref, g_ref, wg_ref, wu_ref, o_ref, xn_ref):
    @pl.when(pl.program_id(1) == 0)
    def _():
        _rms_to_scratch(x_ref, g_ref, xn_ref)

    xn = xn_ref[...]
    gate = jnp.dot(xn, wg_ref[...].astype(BF16), preferred_element_type=F32)
    up = jnp.dot(xn, wu_ref[...].astype(BF16), preferred_element_type=F32)
    o_ref[...] = (gate * _sigmoid(gate) * up).astype(o_ref.dtype)


def norm_swiglu(x, gain, wg, wu, bm, bn):
    m, k = x.shape
    n = wg.cols
    bm, bn = _blk(m, bm, 8), _blk(n, bn, LANES)
    return pl.pallas_call(
        _norm_swiglu_kernel,
        grid=(m // bm, n // bn),
        in_specs=[pl.BlockSpec((bm, k), lambda i, j: (i, 0)),
                  pl.BlockSpec((1, k), lambda i, j: (0, 0)),
                  _w_spec(wg, bn, lambda i, j: j),
                  _w_spec(wu, bn, lambda i, j: j)],
        out_specs=pl.BlockSpec((bm, bn), lambda i, j: (i, j)),
        out_shape=jax.ShapeDtypeStruct((m, n), BF16),
        scratch_shapes=[pltpu.VMEM((bm, k), BF16)],
        compiler_params=_params("parallel", "arbitrary"),
        name="norm_swiglu",
    )(x, gain, wg.arr, wu.arr)


def _mm_res_kernel(*refs, n_pairs):
    xs = refs[:n_pairs]
    ws = refs[n_pairs:2 * n_pairs]
    res_ref = refs[2 * n_pairs]
    o_ref = refs[2 * n_pairs + 1]
    wb = refs[2 * n_pairs + 2:]

    @pl.when(pl.program_id(1) == 0)
    def _():
        for w_ref, wb_ref in zip([w for w in ws if w.dtype != BF16], wb):
            wb_ref[...] = w_ref[...].astype(BF16)

    wb = list(wb)
    acc = res_ref[...]
    for x_ref, w_ref in zip(xs, ws):
        w = w_ref[...] if w_ref.dtype == BF16 else wb.pop(0)[...]
        acc = acc + jnp.dot(x_ref[...], w, preferred_element_type=F32)
    o_ref[...] = acc


def matmul_residual(xs, wvs, res, bm, bn):
    m, n = res.shape
    bm, bn = _blk(m, bm, 8), _blk(n, bn, LANES)
    n_pairs = len(xs)
    in_specs = ([pl.BlockSpec((bm, x.shape[1]), lambda j, i: (i, 0)) for x in xs]
                + [_w_spec(wv, bn, lambda j, i: j) for wv in wvs]
                + [pl.BlockSpec((bm, bn), lambda j, i: (i, j))])
    return pl.pallas_call(
        functools.partial(_mm_res_kernel, n_pairs=n_pairs),
        grid=(n // bn, m // bm),
        in_specs=in_specs,
        out_specs=pl.BlockSpec((bm, bn), lambda j, i: (i, j)),
        out_shape=jax.ShapeDtypeStruct((m, n), F32),
        scratch_shapes=[pltpu.VMEM((wv.rows, bn), BF16) for wv in wvs if wv.arr.dtype != BF16],
        compiler_params=_params("parallel", "arbitrary"),
        name="matmul_residual",
    )(*xs, *[wv.arr for wv in wvs], res)


def _mm_res_norm_kernel(x_ref, w_ref, res_ref, g_ref, o_ref, row_ref):
    j = pl.program_id(1)
    nj, _, bn = row_ref.shape
    row_ref[j] = res_ref[...] + jnp.dot(x_ref[...], w_ref[...], preferred_element_type=F32)

    @pl.when(j == nj - 1)
    def _():
        parts = [row_ref[jb] for jb in range(nj)]
        ms = sum(jnp.sum(p * p, axis=-1, keepdims=True) for p in parts) / (nj * bn)
        inv = lax.rsqrt(ms + NORM_EPS)
        for jb in range(nj):
            o_ref[:, jb * bn:(jb + 1) * bn] = parts[jb] * inv * g_ref[:, jb * bn:(jb + 1) * bn]


def matmul_residual_rmsnorm(x, wv, res, gain, bm, bn):
    m, n = res.shape
    bm, bn = _blk(m, bm, 8), _blk(n, bn, LANES)
    assert wv.arr.dtype == BF16 and not wv.transposed
    return pl.pallas_call(
        _mm_res_norm_kernel,
        grid=(m // bm, n // bn),
        in_specs=[pl.BlockSpec((bm, x.shape[1]), lambda i, j: (i, 0)),
                  _w_spec(wv, bn, lambda i, j: j),
                  pl.BlockSpec((bm, bn), lambda i, j: (i, j)),
                  pl.BlockSpec((1, n), lambda i, j: (0, 0))],
        out_specs=pl.BlockSpec((bm, n), lambda i, j: (i, 0)),
        out_shape=jax.ShapeDtypeStruct((m, n), F32),
        scratch_shapes=[pltpu.VMEM((n // bn, bm, bn), F32)],
        compiler_params=_params("parallel", "arbitrary"),
        name="matmul_residual_rmsnorm",
    )(x, wv.arr, res, gain)


def _prefix_sum_rows(x, period):
    row = lax.broadcasted_iota(jnp.int32, x.shape, 0) % period
    shift = 1
    while shift < period:
        x = x + jnp.where(row >= shift, pltpu.roll(x, shift, axis=0), 0.0)
        shift *= 2
    return x


def _fox_gates_kernel(f_ref, b_ref, c_ref):
    z = f_ref[...] + b_ref[...]
    log_f = jnp.minimum(z, 0.0) - jnp.log(1.0 + jnp.exp(-jnp.abs(z)))
    c_ref[...] = _prefix_sum_rows(log_f, log_f.shape[0])


def fox_gates(f_logit, f_bias, batch, seq):
    return pl.pallas_call(
        _fox_gates_kernel,
        grid=(batch,),
        in_specs=[pl.BlockSpec((seq, LANES), lambda b: (b, 0)), pl.BlockSpec((1, LANES), lambda b: (0, 0))],
        out_specs=pl.BlockSpec((seq, LANES), lambda b: (b, 0)),
        out_shape=jax.ShapeDtypeStruct((batch * seq, LANES), F32),
        compiler_params=_params("parallel"),
        name="fox_gates",
    )(f_logit, f_bias)


def _fox_attn_kernel(q_ref, k_ref, vt_ref, cq_ref, ck_ref, o_ref, m_ref, l_ref, acc_ref, *, tq, hb, scale):
    i = pl.program_id(2)
    d = HEAD_DIM
    heads = range(hb)
    q = [q_ref[:, h * d:(h + 1) * d] for h in heads]
    cq = [cq_ref[0, h] for h in heads]
    for h in heads:
        m_ref[h] = jnp.full((1, tq), -jnp.inf, F32)
        l_ref[h] = jnp.zeros((1, tq), F32)
        acc_ref[h] = jnp.zeros((d, tq), F32)
    key_after_query = (lax.broadcasted_iota(jnp.int32, (tq, tq), 0) > lax.broadcasted_iota(jnp.int32, (tq, tq), 1))

    def block(j, on_diagonal):
        start = pl.multiple_of(j * tq, tq)
        s = [lax.dot_general(k_ref[pl.ds(start, tq), h * d:(h + 1) * d], q[h], (((1,), (1,)), ((), ())),
                             preferred_element_type=F32) for h in heads]
        x = [s[h] * scale + (cq[h] - ck_ref[0, h, pl.ds(start, tq), :]) for h in heads]
        if on_diagonal:
            x = [jnp.where(key_after_query, -jnp.inf, x[h]) for h in heads]
        m_old = [m_ref[h] for h in heads]
        m_new = [jnp.maximum(m_old[h], jnp.max(x[h], axis=0, keepdims=True)) for h in heads]
        alpha = [jnp.exp(m_old[h] - m_new[h]) for h in heads]
        p = [jnp.exp(x[h] - m_new[h]) for h in heads]
        pv = [jnp.dot(vt_ref[0, j, h * d:(h + 1) * d, :], p[h].astype(BF16), preferred_element_type=F32)
              for h in heads]
        for h in heads:
            m_ref[h] = m_new[h]
            l_ref[h] = alpha[h] * l_ref[h] + jnp.sum(p[h], axis=0, keepdims=True)
            acc_ref[h] = alpha[h] * acc_ref[h] + pv[h]

    def body(j, carry):
        block(j, False)
        return carry

    lax.fori_loop(0, i, body, 0)
    block(i, True)
    for h in heads:
        o_ref[:, h * d:(h + 1) * d] = (acc_ref[h] / l_ref[h]).T.astype(o_ref.dtype)


def fox_attention(qkv, v_t, c_row, c_col, batch, seq, heads, tq, hb):
    nq = seq // tq
    d = HEAD_DIM
    groups = heads // hb
    kernel = functools.partial(_fox_attn_kernel, tq=tq, hb=hb, scale=d ** -0.5)
    return pl.pallas_call(
        kernel,
        grid=(batch, groups, nq),
        in_specs=[pl.BlockSpec((tq, hb * d), lambda b, g, i: (b * nq + i, g)),
                  pl.BlockSpec((seq, hb * d), lambda b, g, i: (b, groups + g)),
                  pl.BlockSpec((1, nq, hb * d, tq), lambda b, g, i: (b, 0, g, 0)),
                  pl.BlockSpec((1, hb, 1, tq), lambda b, g, i: (b, g, 0, i)),
                  pl.BlockSpec((1, hb, seq, 1), lambda b, g, i: (b, g, 0, 0))],
        out_specs=pl.BlockSpec((tq, hb * d), lambda b, g, i: (b * nq + i, g)),
        out_shape=jax.ShapeDtypeStruct((batch * seq, heads * d), BF16),
        scratch_shapes=[pltpu.VMEM((hb, 1, tq), F32), pltpu.VMEM((hb, 1, tq), F32), pltpu.VMEM((hb, d, tq), F32)],
        compiler_params=_params("parallel", "parallel", "arbitrary"),
        name="fox_attention",
    )(qkv, qkv, v_t, c_row, c_col)


def _conformer_kernel(val_ref, gate_ref, w_ref, cb_ref, lg_ref, lb_ref, o_ref, hist_ref, shift_ref, y_ref,
                      *, ts, rows):
    channels = val_ref.shape[1]

    @pl.when(pl.program_id(1) == 0)
    def _():
        hist_ref[0:CONV_HALO, :] = jnp.zeros((CONV_HALO, channels), F32)

    hist_ref[CONV_HALO:CONV_HALO + ts, :] = val_ref[...].astype(F32) * _sigmoid(gate_ref[...].astype(F32))

    base = CONV_HALO - (CONV_WIDTH - 1)
    hist_rows = ts + CONV_HALO
    for cb in range(channels // LANES):
        lanes = slice(cb * LANES, (cb + 1) * LANES)
        col = hist_ref[:, lanes]
        for b in range(1, SUBLANES):
            shift_ref[b - 1, :, lanes] = pltpu.roll(col, hist_rows - b, axis=0)
        for rb in range(ts // rows):
            acc = jnp.zeros((rows, LANES), F32)
            for j in range(CONV_WIDTH):
                a, b = divmod(base + j, SUBLANES)
                r0 = rb * rows + a * SUBLANES
                src = hist_ref[r0:r0 + rows, lanes] if b == 0 else shift_ref[b - 1, r0:r0 + rows, lanes]
                acc = acc + w_ref[j:j + 1, lanes] * src
            y_ref[rb * rows:(rb + 1) * rows, lanes] = acc + cb_ref[:, lanes]

    hist_ref[0:CONV_HALO, :] = hist_ref[ts:ts + CONV_HALO, :]

    y = y_ref[...]
    mu = jnp.mean(y, axis=-1, keepdims=True)
    yc = y - mu
    var = jnp.mean(yc * yc, axis=-1, keepdims=True)
    yn = yc * lax.rsqrt(var + NORM_EPS) * lg_ref[...] + lb_ref[...]
    o_ref[...] = (yn * _sigmoid(yn)).astype(o_ref.dtype)


def conformer_conv(glu, glu_col0, conv_w, conv_b, ln_g, ln_b, batch, seq, ts):
    channels = conv_w.shape[1]
    assert glu_col0 % channels == 0
    cb = glu_col0 // channels
    ns = seq // ts
    kernel = functools.partial(_conformer_kernel, ts=ts, rows=64)
    vec = lambda: pl.BlockSpec((1, channels), lambda b, i: (0, 0))
    return pl.pallas_call(
        kernel,
        grid=(batch, ns),
        in_specs=[pl.BlockSpec((ts, channels), lambda b, i: (b * ns + i, cb)),
                  pl.BlockSpec((ts, channels), lambda b, i: (b * ns + i, cb + 1)),
                  pl.BlockSpec((conv_w.shape[0], channels), lambda b, i: (0, 0)),
                  vec(), vec(), vec()],
        out_specs=pl.BlockSpec((ts, channels), lambda b, i: (b * ns + i, 0)),
        out_shape=jax.ShapeDtypeStruct((batch * seq, channels), BF16),
        scratch_shapes=[pltpu.VMEM((ts + CONV_HALO, channels), F32),
                        pltpu.VMEM((SUBLANES - 1, ts + CONV_HALO, channels), F32),
                        pltpu.VMEM((ts, channels), F32)],
        compiler_params=_params("arbitrary", "arbitrary"),
        name="conformer_conv",
    )(glu, glu, conv_w, conv_b, ln_g, ln_b)


def _gdn_in_kernel(x_ref, g_ref, w_ref, cw_ref, wl_ref, o_ref, logit_ref, xn_ref, hist_ref, halo_ref,
                   *, bm, blocks_per_seq, qk_blocks, conv_blocks):
    i = pl.program_id(0)
    j = pl.program_id(1)

    @pl.when(jnp.logical_and(i == 0, j == 0))
    def _():
        halo_ref[...] = jnp.zeros(halo_ref.shape, F32)

    @pl.when(j == 0)
    def _():
        _rms_to_scratch(x_ref, g_ref, xn_ref)
        logit_ref[...] = _dot_w(xn_ref[...], wl_ref[...], True)

    @pl.when(j >= conv_blocks)
    def _():
        o_ref[...] = _dot_w(xn_ref[...], w_ref[...], True).astype(o_ref.dtype)

    @pl.when(j < conv_blocks)
    def _():
        _gdn_conv_block(i, j, w_ref, cw_ref, o_ref, xn_ref, hist_ref, halo_ref, bm=bm,
                        blocks_per_seq=blocks_per_seq, qk_blocks=qk_blocks)


def _gdn_conv_block(i, j, w_ref, cw_ref, o_ref, xn_ref, hist_ref, halo_ref, *, bm, blocks_per_seq, qk_blocks):
    bn = o_ref.shape[1]
    starts_sequence = (i % blocks_per_seq) == 0
    hist_ref[0:SHORT_HALO, :] = jnp.where(starts_sequence, 0.0, halo_ref[j])

    use_norm = j < 2 * qk_blocks
    out_scale = jnp.where(j < qk_blocks, HEAD_DIM ** -0.5, 1.0)
    base = SHORT_HALO - (SHORT_CONV - 1)
    conv_rows = _blk(bm, 128, SUBLANES)

    def project(s):
        cols = slice(s * MXU_TILE, (s + 1) * MXU_TILE)
        hist_ref[SHORT_HALO:SHORT_HALO + bm, cols] = _dot_w(xn_ref[...], w_ref[cols, :], True)

    def conv_slab(s):
        for hb in range(s * MXU_TILE // HEAD_DIM, (s + 1) * MXU_TILE // HEAD_DIM):
            lanes = slice(hb * HEAD_DIM, (hb + 1) * HEAD_DIM)
            for r0 in range(0, bm, conv_rows):
                col = hist_ref[r0:r0 + conv_rows + SHORT_HALO, lanes]
                acc = jnp.zeros((conv_rows, HEAD_DIM), F32)
                for tap in range(SHORT_CONV):
                    ofs = base + tap
                    src = (col[ofs:ofs + conv_rows] if ofs % SUBLANES == 0
                           else pltpu.roll(col, conv_rows + SHORT_HALO - ofs, axis=0)[0:conv_rows])
                    acc = acc + cw_ref[tap:tap + 1, lanes] * src
                y = acc * _sigmoid(acc)
                inv_norm = lax.rsqrt(jnp.sum(y * y, axis=-1, keepdims=True) + NORM_EPS)
                o_ref[r0:r0 + conv_rows, lanes] = (
                    y * (jnp.where(use_norm, inv_norm, 1.0) * out_scale)).astype(o_ref.dtype)

    n_slabs = bn // MXU_TILE
    project(0)
    for s in range(n_slabs):
        if s + 1 < n_slabs:
            project(s + 1)
        conv_slab(s)
    halo_ref[j] = hist_ref[bm:bm + SHORT_HALO, :]


def gdn_in_proj(x, gain, wv, short_conv, w_logit_t, seq, qk_width, bm, bn):
    m, k = x.shape
    n = wv.cols
    conv_width = short_conv.shape[1]
    bm, bn = _blk(seq, bm, 8), _blk(qk_width, bn, LANES)
    assert wv.transposed and seq % bm == 0 and n % bn == 0 and conv_width % bn == 0
    conv_blocks = conv_width // bn
    kernel = functools.partial(_gdn_in_kernel, bm=bm, blocks_per_seq=seq // bm, qk_blocks=qk_width // bn,
                               conv_blocks=conv_blocks)
    return pl.pallas_call(
        kernel,
        grid=(m // bm, n // bn),
        in_specs=[pl.BlockSpec((bm, k), lambda i, j: (i, 0)),
                  pl.BlockSpec((1, k), lambda i, j: (0, 0)),
                  _w_spec(wv, bn, lambda i, j: j),
                  pl.BlockSpec((short_conv.shape[0], bn), lambda i, j: (0, jnp.minimum(j, conv_blocks - 1))),
                  pl.BlockSpec((LANES, k), lambda i, j: (0, 0))],
        out_specs=[pl.BlockSpec((bm, bn), lambda i, j: (i, j)), pl.BlockSpec((bm, LANES), lambda i, j: (i, 0))],
        out_shape=[jax.ShapeDtypeStruct((m, n), BF16), jax.ShapeDtypeStruct((m, LANES), F32)],
        scratch_shapes=[pltpu.VMEM((bm, k), BF16), pltpu.VMEM((bm + SHORT_HALO, bn), F32),
                        pltpu.VMEM((conv_blocks, SHORT_HALO, bn), F32)],
        compiler_params=_params("arbitrary", "arbitrary"),
        name="gdn_in_proj",
    )(x, gain, wv.arr, short_conv, w_logit_t)


def _gdn_gates_kernel(ba_ref, alog_ref, dtb_ref, g_ref, *, gate_heads):
    ba = ba_ref[...]
    beta = _sigmoid(ba)
    g = -jnp.exp(alog_ref[...]) * _softplus(ba + dtb_ref[...])
    gcum = _prefix_sum_rows(g, CHUNK)
    lane = lax.broadcasted_iota(jnp.int32, ba.shape, 1)
    g_ref[...] = jnp.where(lane < gate_heads, beta, gcum)


def gdn_gates(ba, alog_pad, dtb_pad, gate_heads, ts):
    m = ba.shape[0]
    ts = _blk(m, ts, CHUNK)
    row = lambda i: (i, 0)
    const = lambda i: (0, 0)
    return pl.pallas_call(
        functools.partial(_gdn_gates_kernel, gate_heads=gate_heads),
        grid=(m // ts,),
        in_specs=[pl.BlockSpec((ts, LANES), row), pl.BlockSpec((1, LANES), const), pl.BlockSpec((1, LANES), const)],
        out_specs=pl.BlockSpec((ts, LANES), row),
        out_shape=jax.ShapeDtypeStruct((m, LANES), F32),
        compiler_params=_params("parallel"),
        name="gdn_gates",
    )(ba, alog_pad, dtb_pad)


GROUP = MXU_TILE // CHUNK


def _gdn_solve_kernel(q_ref, k_ref, g_ref, gr_ref, bdm_ref, kkm_ref, t_ref, p_ref, *, v_heads, rep):
    c = CHUNK
    gw = MXU_TILE
    n_groups = v_heads // GROUP
    n_chunks = g_ref.shape[0] // c
    row = lax.broadcasted_iota(jnp.int32, (c, gw), 0)
    lane = lax.broadcasted_iota(jnp.int32, (c, gw), 1)
    blk = lane // c
    col = lane % c
    tril = row >= col
    strict = row > col
    eye = (row == col).astype(F32)
    bd_mask = bdm_ref[...]
    kk_mask = kkm_ref[...]

    def block_diag(x):
        xb = x.astype(BF16)
        return jnp.concatenate([xb] * GROUP, axis=0) * bd_mask

    gates = [g_ref[n * c:(n + 1) * c, :] for n in range(n_chunks)]

    def per_head_columns(first_lane, u):
        n, g = units[u]
        out = gates[n][:, first_lane + GROUP * g + GROUP - 1:first_lane + GROUP * g + GROUP]
        for mth in range(GROUP - 2, -1, -1):
            cm = gates[n][:, first_lane + GROUP * g + mth:first_lane + GROUP * g + mth + 1]
            out = jnp.where(blk == mth, cm, out)
        return out

    units = [(n, g) for n in range(n_chunks) for g in range(n_groups)]
    groups = range(len(units))
    rows = [slice(n * c, (n + 1) * c) for n, _ in units]
    lanes = [slice(g * gw, (g + 1) * gw) for _, g in units]
    kq = [jnp.concatenate([k_ref[rows[u], lanes[u]], q_ref[rows[u], lanes[u]]], axis=0) for u in groups]
    rhs = [jnp.concatenate([k_ref[rows[u], lanes[u]]] * GROUP, axis=0) * kk_mask for u in groups]
    kkqk = [lax.dot_general(kq[g], rhs[g], (((1,), (1,)), ((), ())), preferred_element_type=F32) for g in groups]
    decay = [jnp.exp(jnp.where(tril, per_head_columns(v_heads, u) - gr_ref[units[u][0], :, lanes[u]], -jnp.inf))
             for u in groups]
    low = [jnp.where(strict, per_head_columns(0, g) * kkqk[g][:c] * decay[g], 0.0) for g in groups]
    for g in groups:
        p_ref[rows[g], lanes[g]] = (kkqk[g][c:] * decay[g]).astype(p_ref.dtype)

    inv = [eye - low[g] for g in groups]
    bd = [block_diag(low[g]) for g in groups]
    power = [jnp.dot(low[g].astype(BF16), bd[g], preferred_element_type=F32) for g in groups]
    span = 2
    while span <= c // 2:
        bd = [block_diag(power[g]) for g in groups]
        if span < c // 2:
            both = [jnp.dot(jnp.concatenate([power[g], inv[g]], axis=0).astype(BF16), bd[g],
                            preferred_element_type=F32) for g in groups]
            power = [both[g][:c] for g in groups]
            inv = [inv[g] + both[g][c:] for g in groups]
        else:
            inv = [inv[g] + jnp.dot(inv[g].astype(BF16), bd[g], preferred_element_type=F32) for g in groups]
        span *= 2
    for g in groups:
        t_ref[rows[g], lanes[g]] = inv[g].astype(t_ref.dtype)


def gdn_solve(qkv, gates, g_rows, v_heads, rep, chunks_per_step):
    m = qkv.shape[0]
    width = v_heads * CHUNK
    qk_width = v_heads * HEAD_DIM // rep
    assert GROUP * CHUNK == MXU_TILE and rep * HEAD_DIM == MXU_TILE and v_heads % GROUP == 0
    assert qk_width * rep == v_heads * HEAD_DIM and qk_width == width
    rb = lax.broadcasted_iota(jnp.int32, (GROUP * CHUNK, MXU_TILE), 0) // CHUNK
    ln = lax.broadcasted_iota(jnp.int32, (GROUP * CHUNK, MXU_TILE), 1)
    bd_mask = (rb == ln // CHUNK).astype(BF16)
    kk_mask = (rb // rep == ln // HEAD_DIM).astype(BF16)
    kernel = functools.partial(_gdn_solve_kernel, v_heads=v_heads, rep=rep)
    row = lambda i: (i, 0)
    const = lambda i: (0, 0)
    rows = chunks_per_step * CHUNK
    assert m % rows == 0
    return pl.pallas_call(
        kernel,
        grid=(m // rows,),
        in_specs=[pl.BlockSpec((rows, qk_width), row), pl.BlockSpec((rows, qk_width), lambda i: (i, 1)),
                  pl.BlockSpec((rows, LANES), row),
                  pl.BlockSpec((chunks_per_step, 1, width), lambda i: (i, 0, 0)),
                  pl.BlockSpec((GROUP * CHUNK, MXU_TILE), const), pl.BlockSpec((GROUP * CHUNK, MXU_TILE), const)],
        out_specs=[pl.BlockSpec((rows, width), row), pl.BlockSpec((rows, width), row)],
        out_shape=[jax.ShapeDtypeStruct((m, width), BF16), jax.ShapeDtypeStruct((m, width), BF16)],
        compiler_params=_params("parallel"),
        name="gdn_solve",
    )(qkv, qkv, gates, g_rows, bd_mask, kk_mask)


def _gdn_chunk_kernel(q_ref, k_ref, v_ref, z_ref, g_ref, t_ref, p_ref, on_ref, o_ref, state_ref, *, qk_heads, rep):
    c = CHUNK
    d = HEAD_DIM

    @pl.when(pl.program_id(1) == 0)
    def _():
        state_ref[...] = jnp.zeros(state_ref.shape, F32)

    n_v = qk_heads * rep
    pw = rep * d
    tw = rep * c
    gain = on_ref[...]
    zeros = jnp.zeros((c, d), BF16)
    pairs = range(qk_heads)

    def on_diagonal(parts):
        return jnp.concatenate(
            [jnp.concatenate([parts[r] if s == r else zeros for s in range(rep)], axis=1) for r in range(rep)],
            axis=0)

    def chunk_step(n, state):
        rows = slice(n * c, (n + 1) * c)
        gates = g_ref[rows, :]

        def head_cols(first_lane, p):
            return [gates[:, first_lane + p * rep + r:first_lane + p * rep + r + 1] for r in range(rep)]

        k = [k_ref[rows, p * d:(p + 1) * d] for p in pairs]
        kq = [jnp.concatenate([k[p], q_ref[rows, p * d:(p + 1) * d]], axis=0) for p in pairs]
        kqs = [jnp.dot(kq[p], state[p].astype(BF16), preferred_element_type=F32) for p in pairs]

        beta = [head_cols(0, p) for p in pairs]
        gc = [head_cols(n_v, p) for p in pairs]
        eg = [[jnp.exp(g) for g in gc[p]] for p in pairs]
        rhs = [on_diagonal([(beta[p][r] * (v_ref[rows, p * pw + r * d:p * pw + (r + 1) * d].astype(F32)
                                           - kqs[p][:c, r * d:(r + 1) * d] * eg[p][r])).astype(BF16)
                            for r in range(rep)]) for p in pairs]
        v_new = [jnp.dot(t_ref[rows, p * tw:(p + 1) * tw], rhs[p], preferred_element_type=F32)
                 for p in pairs]
        vn_diag = [on_diagonal([v_new[p][:, r * d:(r + 1) * d].astype(BF16) for r in range(rep)]) for p in pairs]
        intra = [jnp.dot(p_ref[rows, p * tw:(p + 1) * tw], vn_diag[p], preferred_element_type=F32) for p in pairs]

        g_last = [[g[c - 1:c, :] for g in gc[p]] for p in pairs]
        v_dec = [jnp.concatenate([jnp.exp(g_last[p][r] - gc[p][r]) * v_new[p][:, r * d:(r + 1) * d]
                                  for r in range(rep)], axis=1).astype(BF16) for p in pairs]
        kt = [k[p].astype(F32).T.astype(BF16) for p in pairs]
        upd = [jnp.dot(kt[p], v_dec[p], preferred_element_type=F32) for p in pairs]
        scale = [jnp.concatenate([jnp.broadcast_to(jnp.exp(g_last[p][r]), (1, d)) for r in range(rep)], axis=1)
                 for p in pairs]
        new_state = [state[p] * scale[p] + upd[p] for p in pairs]

        for p in pairs:
            for r in range(rep):
                vl = slice(p * pw + r * d, p * pw + (r + 1) * d)
                o = kqs[p][c:, r * d:(r + 1) * d] * eg[p][r] + intra[p][:, r * d:(r + 1) * d]
                on = o * lax.rsqrt(jnp.mean(o * o, axis=-1, keepdims=True) + NORM_EPS) * gain
                zf = z_ref[rows, vl].astype(F32)
                o_ref[rows, vl] = (on * (zf * _sigmoid(zf))).astype(o_ref.dtype)
        return new_state

    state = [state_ref[p] for p in pairs]
    for n in range(g_ref.shape[0] // c):
        state = chunk_step(n, state)
    for p in pairs:
        state_ref[p] = state[p]


def gdn_chunk(qkvz, gates, t_inv, p_intra, o_norm, batch, seq, qk_heads, rep, chunks_per_step):
    m = batch * seq
    qk_width = qk_heads * HEAD_DIM
    v_width = qk_width * rep
    assert v_width == 2 * qk_width
    kernel = functools.partial(_gdn_chunk_kernel, qk_heads=qk_heads, rep=rep)
    rows = chunks_per_step * CHUNK
    assert seq % rows == 0
    nc = seq // rows
    row = lambda b, i: (b * nc + i, 0)
    col1 = lambda b, i: (b * nc + i, 1)
    col2 = lambda b, i: (b * nc + i, 2)
    return pl.pallas_call(
        kernel,
        grid=(batch, nc),
        in_specs=[pl.BlockSpec((rows, qk_width), row), pl.BlockSpec((rows, qk_width), col1),
                  pl.BlockSpec((rows, v_width), col1), pl.BlockSpec((rows, v_width), col2),
                  pl.BlockSpec((rows, LANES), row),
                  pl.BlockSpec((rows, t_inv.shape[1]), row), pl.BlockSpec((rows, p_intra.shape[1]), row),
                  pl.BlockSpec((1, HEAD_DIM), lambda b, i: (0, 0))],
        out_specs=pl.BlockSpec((rows, v_width), row),
        out_shape=jax.ShapeDtypeStruct((m, v_width), BF16),
        scratch_shapes=[pltpu.VMEM((qk_heads, HEAD_DIM, rep * HEAD_DIM), F32)],
        compiler_params=_params("arbitrary", "arbitrary"),
        name="gdn_chunk",
    )(qkvz, qkvz, qkvz, qkvz, gates, t_inv, p_intra, o_norm)


def _pad_cols(w, n):
    return jnp.pad(w, ((0, 0), (0, n - w.shape[1])))


def _row(v):
    return v.reshape(1, -1).astype(F32)


def kernel(x, norm_mix0, w_in0, fgate_bias0, conv_w0, conv_b0, conv_ln_g0, conv_ln_b0, w_out0, norm_mix1, w_in1,
           short_conv1, a_log1, dt_bias1, o_norm1, w_out1, norm_ffn, w_gate, w_up, w_down, final_norm):
    batch, seq, d_model = x.shape
    m = batch * seq
    fox_heads = fgate_bias0.shape[1]
    fox_width = fox_heads * HEAD_DIM
    conv_ch = conv_w0.shape[2]
    gdn_v_heads = a_log1.shape[1]
    gdn_v_width = gdn_v_heads * HEAD_DIM
    gdn_qkv_width = short_conv1.shape[2]
    gdn_qk_width = (gdn_qkv_width - gdn_v_width) // 2
    gdn_qk_heads = gdn_qk_width // HEAD_DIM
    rep = gdn_v_heads // gdn_qk_heads

    h = x.reshape(m, d_model)

    w_gate_b, w_up_b, w_down_b = w_gate.astype(BF16), w_up.astype(BF16), w_down.astype(BF16)
    w_in0_t = jnp.swapaxes(w_in0, 1, 2).astype(BF16)
    w_in1_t = jnp.swapaxes(w_in1, 1, 2).astype(BF16)

    def pad_rows(w, n):
        return jnp.pad(w, ((0, n - w.shape[0]), (0, 0)))

    def ffn(h, layer, final_gain=None):
        mid = norm_swiglu(h, _row(norm_ffn[layer]), wview(w_gate_b, layer), wview(w_up_b, layer), bm=1024, bn=512)
        if final_gain is not None:
            return matmul_residual_rmsnorm(mid, wview(w_down_b, layer), h, final_gain, bm=512, bn=1024)
        return matmul_residual([mid], [wview(w_down_b, layer)], h, bm=512, bn=1024)

    g0 = _row(norm_mix0[0])
    f0, f1 = 3 * fox_width, 3 * fox_width + fox_heads
    w_qkvg = jnp.concatenate([w_in0_t[0, :f0], w_in0_t[0, f1:]], axis=0)
    qkv, f_logit = norm_matmul(h, g0, wview(w_qkvg, transposed=True), pad_rows(w_in0_t[0, f0:f1], LANES),
                               bm=1024, bn=1024)

    tq = 256
    c = fox_gates(f_logit, _pad_cols(_row(fgate_bias0[0]), LANES), batch, seq)
    c_heads = jnp.transpose(c.reshape(batch, seq, LANES)[:, :, :fox_heads], (0, 2, 1))
    v_t = jnp.transpose(qkv[:, 2 * fox_width:f0].reshape(batch, seq // tq, tq, fox_width), (0, 1, 3, 2))
    a_out = fox_attention(qkv, v_t, c_heads.reshape(batch, fox_heads, 1, seq),
                          c_heads.reshape(batch, fox_heads, seq, 1), batch, seq, fox_heads, tq,
                          hb=min(8, fox_heads))
    u = conformer_conv(qkv, f0, jnp.pad(conv_w0[0], ((0, CONV_HALO - CONV_WIDTH), (0, 0))), _row(conv_b0[0]),
                       _row(conv_ln_g0[0]), _row(conv_ln_b0[0]), batch, seq, ts=256)
    h = matmul_residual([a_out, u], [wview(w_out0, 0, row0=0, rows=fox_width),
                                     wview(w_out0, 0, row0=fox_width, rows=conv_ch)], h, bm=1024, bn=1024)
    h = ffn(h, 0)

    g1 = _row(norm_mix1[0])
    z0 = gdn_qkv_width
    qkvz1, ba1 = gdn_in_proj(h, g1, wview(w_in1_t, 0, cols=z0 + gdn_v_width, transposed=True),
                             jnp.pad(short_conv1[0], ((0, SUBLANES - SHORT_CONV), (0, 0))),
                             pad_rows(w_in1_t[0, z0 + gdn_v_width:], LANES), seq, gdn_qk_width, bm=1024, bn=1024)
    alog_pad = jnp.pad(_row(a_log1[0]), ((0, 0), (gdn_v_heads, LANES - 2 * gdn_v_heads)))
    dtb_pad = jnp.pad(_row(dt_bias1[0]), ((0, 0), (gdn_v_heads, LANES - 2 * gdn_v_heads)))
    gates1 = gdn_gates(ba1, alog_pad, dtb_pad, gdn_v_heads, ts=1024)
    g_rows = jnp.transpose(gates1[:, gdn_v_heads:2 * gdn_v_heads].reshape(m // CHUNK, CHUNK, gdn_v_heads),
                           (0, 2, 1)).reshape(m // CHUNK, 1, gdn_v_heads * CHUNK)
    t_inv, p_intra = gdn_solve(qkvz1, gates1, g_rows, gdn_v_heads, rep, chunks_per_step=4)
    o1 = gdn_chunk(qkvz1, gates1, t_inv, p_intra, _row(o_norm1[0]), batch, seq, gdn_qk_heads, rep,
                   chunks_per_step=2)
    h = matmul_residual([o1], [wview(w_out1.astype(BF16), 0)], h, bm=512, bn=1024)
    return ffn(h, 1, final_gain=_row(final_norm)).reshape(batch, seq, d_model)
```

```python
import functools
from typing import NamedTuple, Optional

import jax
import jax.numpy as jnp
from jax import lax
from jax.experimental import pallas as pl
from jax.experimental.pallas import tpu as pltpu

F32 = jnp.float32
BF16 = jnp.bfloat16

NORM_EPS = 1e-6
LANES = 128
SUBLANES = 8
MXU_TILE = 256
HEAD_DIM = 128
CHUNK = 64
CONV_WIDTH = 31
CONV_HALO = 32
SHORT_CONV = 4
SHORT_HALO = 8
VMEM_LIMIT_BYTES = 56 * 1024 * 1024

TILE_PROJ = (1024, 1024)
TILE_SWIGLU = (1024, 512)
TILE_DOWN = (512, 1024)
TILE_OUT_FOX = (1024, 1024)
TILE_OUT_GDN = (512, 1024)
ATTN_BLOCK = 256
ATTN_HEADS_PER_STEP = 8
CONFORMER_ROWS = 256
GATES_ROWS = 1024
SOLVE_CHUNKS_PER_STEP = 4
SCAN_CHUNKS_PER_STEP = 2


def _params(*sem):
    return pltpu.CompilerParams(dimension_semantics=sem, vmem_limit_bytes=VMEM_LIMIT_BYTES)


def _blk(dim, pref, align):
    best = dim
    for cand in range(align, min(dim, pref) + 1, align):
        if dim % cand == 0:
            best = cand
    return best


def _sigmoid(x):
    return 1.0 / (1.0 + jnp.exp(-x))


def _softplus(x):
    return jnp.maximum(x, 0.0) + jnp.log(1.0 + jnp.exp(-jnp.abs(x)))


def _rms_to_scratch(x_ref, g_ref, xn_ref):
    x = x_ref[...].astype(F32)
    ms = jnp.mean(x * x, axis=-1, keepdims=True)
    xn_ref[...] = (x * lax.rsqrt(ms + NORM_EPS) * g_ref[...]).astype(xn_ref.dtype)


class WeightView(NamedTuple):
    arr: jax.Array
    lead: Optional[int]
    row0: int
    rows: int
    col0: int
    cols: int
    transposed: bool


def wview(arr, lead=None, row0=0, rows=None, col0=0, cols=None, transposed=False):
    k, n = arr.shape[-2:][::-1] if transposed else arr.shape[-2:]
    return WeightView(arr, lead, row0, k - row0 if rows is None else rows, col0, n - col0 if cols is None else cols,
                      transposed)


def _w_spec(wv, bn, col_block):
    assert wv.row0 % wv.rows == 0 and wv.col0 % bn == 0 and wv.cols % bn == 0
    rb, cb = wv.row0 // wv.rows, wv.col0 // bn
    shape = (bn, wv.rows) if wv.transposed else (wv.rows, bn)
    index = (lambda *g: (cb + col_block(*g), rb)) if wv.transposed else (lambda *g: (rb, cb + col_block(*g)))
    if wv.arr.ndim == 3:
        return pl.BlockSpec((None,) + shape, lambda *g: (wv.lead,) + index(*g))
    return pl.BlockSpec(shape, index)


def _dot_w(x, w, transposed):
    dims = (((1,), (1,)), ((), ())) if transposed else (((1,), (0,)), ((), ()))
    return lax.dot_general(x, w.astype(BF16), dims, preferred_element_type=F32)


def _norm_mm_kernel(x_ref, g_ref, w_ref, wl_ref, o_ref, logit_ref, xn_ref, *, transposed):
    @pl.when(pl.program_id(1) == 0)
    def _():
        _rms_to_scratch(x_ref, g_ref, xn_ref)
        logit_ref[...] = _dot_w(xn_ref[...], wl_ref[...], True)

    o_ref[...] = _dot_w(xn_ref[...], w_ref[...], transposed).astype(o_ref.dtype)


def norm_matmul(x, gain, wv, w_logit_t, bm, bn):
    m, k = x.shape
    n = wv.cols
    bm, bn = _blk(m, bm, 8), _blk(n, bn, LANES)
    return pl.pallas_call(
        functools.partial(_norm_mm_kernel, transposed=wv.transposed),
        grid=(m // bm, n // bn),
        in_specs=[pl.BlockSpec((bm, k), lambda i, j: (i, 0)),
                  pl.BlockSpec((1, k), lambda i, j: (0, 0)),
                  _w_spec(wv, bn, lambda i, j: j),
                  pl.BlockSpec((LANES, k), lambda i, j: (0, 0))],
        out_specs=[pl.BlockSpec((bm, bn), lambda i, j: (i, j)), pl.BlockSpec((bm, LANES), lambda i, j: (i, 0))],
        out_shape=[jax.ShapeDtypeStruct((m, n), BF16), jax.ShapeDtypeStruct((m, LANES), F32)],
        scratch_shapes=[pltpu.VMEM((bm, k), BF16)],
        compiler_params=_params("parallel", "arbitrary"),
        name="norm_matmul",
    )(x, gain, wv.arr, w_logit_t)


def _norm_swiglu_kernel(x_ref, g_ref, wg_ref, wu_ref, o_ref, xn_ref):
    @pl.when(pl.program_id(1) == 0)
    def _():
        _rms_to_scratch(x_ref, g_ref, xn_ref)

    xn = xn_ref[...]
    gate = jnp.dot(xn, wg_ref[...].astype(BF16), preferred_element_type=F32)
    up = jnp.dot(xn, wu_ref[...].astype(BF16), preferred_element_type=F32)
    o_ref[...] = (gate * _sigmoid(gate) * up).astype(o_ref.dtype)


def norm_swiglu(x, gain, wg, wu, bm, bn):
    m, k = x.shape
    n = wg.cols
    bm, bn = _blk(m, bm, 8), _blk(n, bn, LANES)
    return pl.pallas_call(
        _norm_swiglu_kernel,
        grid=(m // bm, n // bn),
        in_specs=[pl.BlockSpec((bm, k), lambda i, j: (i, 0)),
                  pl.BlockSpec((1, k), lambda i, j: (0, 0)),
                  _w_spec(wg, bn, lambda i, j: j),
                  _w_spec(wu, bn, lambda i, j: j)],
        out_specs=pl.BlockSpec((bm, bn), lambda i, j: (i, j)),
        out_shape=jax.ShapeDtypeStruct((m, n), BF16),
        scratch_shapes=[pltpu.VMEM((bm, k), BF16)],
        compiler_params=_params("parallel", "arbitrary"),
        name="norm_swiglu",
    )(x, gain, wg.arr, wu.arr)


def _mm_res_kernel(*refs, n_pairs):
    xs = refs[:n_pairs]
    ws = refs[n_pairs:2 * n_pairs]
    res_ref = refs[2 * n_pairs]
    o_ref = refs[2 * n_pairs + 1]
    wb = refs[2 * n_pairs + 2:]

    @pl.when(pl.program_id(1) == 0)
    def _():
        for w_ref, wb_ref in zip([w for w in ws if w.dtype != BF16], wb):
            wb_ref[...] = w_ref[...].astype(BF16)

    wb = list(wb)
    acc = res_ref[...]
    for x_ref, w_ref in zip(xs, ws):
        w = w_ref[...] if w_ref.dtype == BF16 else wb.pop(0)[...]
        acc = acc + jnp.dot(x_ref[...], w, preferred_element_type=F32)
    o_ref[...] = acc


def matmul_residual(xs, wvs, res, bm, bn):
    m, n = res.shape
    bm, bn = _blk(m, bm, 8), _blk(n, bn, LANES)
    n_pairs = len(xs)
    in_specs = ([pl.BlockSpec((bm, x.shape[1]), lambda j, i: (i, 0)) for x in xs]
                + [_w_spec(wv, bn, lambda j, i: j) for wv in wvs]
                + [pl.BlockSpec((bm, bn), lambda j, i: (i, j))])
    return pl.pallas_call(
        functools.partial(_mm_res_kernel, n_pairs=n_pairs),
        grid=(n // bn, m // bm),
        in_specs=in_specs,
        out_specs=pl.BlockSpec((bm, bn), lambda j, i: (i, j)),
        out_shape=jax.ShapeDtypeStruct((m, n), F32),
        scratch_shapes=[pltpu.VMEM((wv.rows, bn), BF16) for wv in wvs if wv.arr.dtype != BF16],
        compiler_params=_params("parallel", "arbitrary"),
        name="matmul_residual",
    )(*xs, *[wv.arr for wv in wvs], res)


def _mm_res_norm_kernel(x_ref, w_ref, res_ref, g_ref, o_ref, row_ref):
    j = pl.program_id(1)
    nj, _, bn = row_ref.shape
    row_ref[j] = res_ref[...] + jnp.dot(x_ref[...], w_ref[...], preferred_element_type=F32)

    @pl.when(j == nj - 1)
    def _():
        parts = [row_ref[jb] for jb in range(nj)]
        ms = sum(jnp.sum(p * p, axis=-1, keepdims=True) for p in parts) / (nj * bn)
        inv = lax.rsqrt(ms + NORM_EPS)
        for jb in range(nj):
            o_ref[:, jb * bn:(jb + 1) * bn] = parts[jb] * inv * g_ref[:, jb * bn:(jb + 1) * bn]


def matmul_residual_rmsnorm(x, wv, res, gain, bm, bn):
    m, n = res.shape
    bm, bn = _blk(m, bm, 8), _blk(n, bn, LANES)
    assert wv.arr.dtype == BF16 and not wv.transposed
    return pl.pallas_call(
        _mm_res_norm_kernel,
        grid=(m // bm, n // bn),
        in_specs=[pl.BlockSpec((bm, x.shape[1]), lambda i, j: (i, 0)),
                  _w_spec(wv, bn, lambda i, j: j),
                  pl.BlockSpec((bm, bn), lambda i, j: (i, j)),
                  pl.BlockSpec((1, n), lambda i, j: (0, 0))],
        out_specs=pl.BlockSpec((bm, n), lambda i, j: (i, 0)),
        out_shape=jax.ShapeDtypeStruct((m, n), F32),
        scratch_shapes=[pltpu.VMEM((n // bn, bm, bn), F32)],
        compiler_params=_params("parallel", "arbitrary"),
        name="matmul_residual_rmsnorm",
    )(x, wv.arr, res, gain)


def _prefix_sum_rows(x, period):
    row = lax.broadcasted_iota(jnp.int32, x.shape, 0) % period
    shift = 1
    while shift < period:
        x = x + jnp.where(row >= shift, pltpu.roll(x, shift, axis=0), 0.0)
        shift *= 2
    return x


def _fox_gates_kernel(f_ref, b_ref, c_ref):
    z = f_ref[...] + b_ref[...]
    log_f = jnp.minimum(z, 0.0) - jnp.log(1.0 + jnp.exp(-jnp.abs(z)))
    c_ref[...] = _prefix_sum_rows(log_f, log_f.shape[0])


def fox_gates(f_logit, f_bias, batch, seq):
    return pl.pallas_call(
        _fox_gates_kernel,
        grid=(batch,),
        in_specs=[pl.BlockSpec((seq, LANES), lambda b: (b, 0)), pl.BlockSpec((1, LANES), lambda b: (0, 0))],
        out_specs=pl.BlockSpec((seq, LANES), lambda b: (b, 0)),
        out_shape=jax.ShapeDtypeStruct((batch * seq, LANES), F32),
        compiler_params=_params("parallel"),
        name="fox_gates",
    )(f_logit, f_bias)


def _fox_attn_kernel(q_ref, k_ref, vt_ref, cq_ref, ck_ref, o_ref, m_ref, l_ref, acc_ref, *, tq, hb, scale):
    i = pl.program_id(2)
    d = HEAD_DIM
    heads = range(hb)
    q = [q_ref[:, h * d:(h + 1) * d] for h in heads]
    cq = [cq_ref[0, h] for h in heads]
    for h in heads:
        m_ref[h] = jnp.full((1, tq), -jnp.inf, F32)
        l_ref[h] = jnp.zeros((1, tq), F32)
        acc_ref[h] = jnp.zeros((d, tq), F32)
    key_after_query = (lax.broadcasted_iota(jnp.int32, (tq, tq), 0) > lax.broadcasted_iota(jnp.int32, (tq, tq), 1))

    def block(j, on_diagonal):
        start = pl.multiple_of(j * tq, tq)
        s = [lax.dot_general(k_ref[pl.ds(start, tq), h * d:(h + 1) * d], q[h], (((1,), (1,)), ((), ())),
                             preferred_element_type=F32) for h in heads]
        x = [s[h] * scale + (cq[h] - ck_ref[0, h, pl.ds(start, tq), :]) for h in heads]
        if on_diagonal:
            x = [jnp.where(key_after_query, -jnp.inf, x[h]) for h in heads]
        m_old = [m_ref[h] for h in heads]
        m_new = [jnp.maximum(m_old[h], jnp.max(x[h], axis=0, keepdims=True)) for h in heads]
        alpha = [jnp.exp(m_old[h] - m_new[h]) for h in heads]
        p = [jnp.exp(x[h] - m_new[h]) for h in heads]
        pv = [jnp.dot(vt_ref[0, j, h * d:(h + 1) * d, :], p[h].astype(BF16), preferred_element_type=F32)
              for h in heads]
        for h in heads:
            m_ref[h] = m_new[h]
            l_ref[h] = alpha[h] * l_ref[h] + jnp.sum(p[h], axis=0, keepdims=True)
            acc_ref[h] = alpha[h] * acc_ref[h] + pv[h]

    def body(j, carry):
        block(j, False)
        return carry

    lax.fori_loop(0, i, body, 0)
    block(i, True)
    for h in heads:
        o_ref[:, h * d:(h + 1) * d] = (acc_ref[h] / l_ref[h]).T.astype(o_ref.dtype)


def fox_attention(qkv, v_t, c_row, c_col, batch, seq, heads, tq, hb):
    nq = seq // tq
    d = HEAD_DIM
    groups = heads // hb
    kernel = functools.partial(_fox_attn_kernel, tq=tq, hb=hb, scale=d ** -0.5)
    return pl.pallas_call(
        kernel,
        grid=(batch, groups, nq),
        in_specs=[pl.BlockSpec((tq, hb * d), lambda b, g, i: (b * nq + i, g)),
                  pl.BlockSpec((seq, hb * d), lambda b, g, i: (b, groups + g)),
                  pl.BlockSpec((1, nq, hb * d, tq), lambda b, g, i: (b, 0, g, 0)),
                  pl.BlockSpec((1, hb, 1, tq), lambda b, g, i: (b, g, 0, i)),
                  pl.BlockSpec((1, hb, seq, 1), lambda b, g, i: (b, g, 0, 0))],
        out_specs=pl.BlockSpec((tq, hb * d), lambda b, g, i: (b * nq + i, g)),
        out_shape=jax.ShapeDtypeStruct((batch * seq, heads * d), BF16),
        scratch_shapes=[pltpu.VMEM((hb, 1, tq), F32), pltpu.VMEM((hb, 1, tq), F32), pltpu.VMEM((hb, d, tq), F32)],
        compiler_params=_params("parallel", "parallel", "arbitrary"),
        name="fox_attention",
    )(qkv, qkv, v_t, c_row, c_col)


def _conformer_kernel(val_ref, gate_ref, w_ref, cb_ref, lg_ref, lb_ref, o_ref, hist_ref, shift_ref, y_ref,
                      *, ts, rows):
    channels = val_ref.shape[1]

    @pl.when(pl.program_id(1) == 0)
    def _():
        hist_ref[0:CONV_HALO, :] = jnp.zeros((CONV_HALO, channels), F32)

    hist_ref[CONV_HALO:CONV_HALO + ts, :] = val_ref[...].astype(F32) * _sigmoid(gate_ref[...].astype(F32))

    base = CONV_HALO - (CONV_WIDTH - 1)
    hist_rows = ts + CONV_HALO
    for cb in range(channels // LANES):
        lanes = slice(cb * LANES, (cb + 1) * LANES)
        col = hist_ref[:, lanes]
        for b in range(1, SUBLANES):
            shift_ref[b - 1, :, lanes] = pltpu.roll(col, hist_rows - b, axis=0)
        for rb in range(ts // rows):
            acc = jnp.zeros((rows, LANES), F32)
            for j in range(CONV_WIDTH):
                a, b = divmod(base + j, SUBLANES)
                r0 = rb * rows + a * SUBLANES
                src = hist_ref[r0:r0 + rows, lanes] if b == 0 else shift_ref[b - 1, r0:r0 + rows, lanes]
                acc = acc + w_ref[j:j + 1, lanes] * src
            y_ref[rb * rows:(rb + 1) * rows, lanes] = acc + cb_ref[:, lanes]

    hist_ref[0:CONV_HALO, :] = hist_ref[ts:ts + CONV_HALO, :]

    y = y_ref[...]
    mu = jnp.mean(y, axis=-1, keepdims=True)
    yc = y - mu
    var = jnp.mean(yc * yc, axis=-1, keepdims=True)
    yn = yc * lax.rsqrt(var + NORM_EPS) * lg_ref[...] + lb_ref[...]
    o_ref[...] = (yn * _sigmoid(yn)).astype(o_ref.dtype)


def conformer_conv(glu, glu_col0, conv_w, conv_b, ln_g, ln_b, batch, seq, ts):
    channels = conv_w.shape[1]
    assert glu_col0 % channels == 0
    cb = glu_col0 // channels
    ns = seq // ts
    kernel = functools.partial(_conformer_kernel, ts=ts, rows=64)
    vec = lambda: pl.BlockSpec((1, channels), lambda b, i: (0, 0))
    return pl.pallas_call(
        kernel,
        grid=(batch, ns),
        in_specs=[pl.BlockSpec((ts, channels), lambda b, i: (b * ns + i, cb)),
                  pl.BlockSpec((ts, channels), lambda b, i: (b * ns + i, cb + 1)),
                  pl.BlockSpec((conv_w.shape[0], channels), lambda b, i: (0, 0)),
                  vec(), vec(), vec()],
        out_specs=pl.BlockSpec((ts, channels), lambda b, i: (b * ns + i, 0)),
        out_shape=jax.ShapeDtypeStruct((batch * seq, channels), BF16),
        scratch_shapes=[pltpu.VMEM((ts + CONV_HALO, channels), F32),
                        pltpu.VMEM((SUBLANES - 1, ts + CONV_HALO, channels), F32),
                        pltpu.VMEM((ts, channels), F32)],
        compiler_params=_params("arbitrary", "arbitrary"),
        name="conformer_conv",
    )(glu, glu, conv_w, conv_b, ln_g, ln_b)


def _gdn_in_kernel(x_ref, g_ref, w_ref, cw_ref, wl_ref, o_ref, logit_ref, xn_ref, hist_ref, halo_ref,
                   *, bm, blocks_per_seq, qk_blocks, conv_blocks):
    i = pl.program_id(0)
    j = pl.program_id(1)

    @pl.when(jnp.logical_and(i == 0, j == 0))
    def _():
        halo_ref[...] = jnp.zeros(halo_ref.shape, F32)

    @pl.when(j == 0)
    def _():
        _rms_to_scratch(x_ref, g_ref, xn_ref)
        logit_ref[...] = _dot_w(xn_ref[...], wl_ref[...], True)

    @pl.when(j >= conv_blocks)
    def _():
        o_ref[...] = _dot_w(xn_ref[...], w_ref[...], True).astype(o_ref.dtype)

    conv_args = dict(bm=bm, blocks_per_seq=blocks_per_seq, qk_blocks=qk_blocks)

    @pl.when(j < 2 * qk_blocks)
    def _():
        _gdn_conv_block(i, j, w_ref, cw_ref, o_ref, xn_ref, hist_ref, halo_ref, normalize=True, **conv_args)

    @pl.when(jnp.logical_and(j >= 2 * qk_blocks, j < conv_blocks))
    def _():
        _gdn_conv_block(i, j, w_ref, cw_ref, o_ref, xn_ref, hist_ref, halo_ref, normalize=False, **conv_args)


def _gdn_conv_block(i, j, w_ref, cw_ref, o_ref, xn_ref, hist_ref, halo_ref, *, bm, blocks_per_seq, qk_blocks,
                    normalize):
    bn = o_ref.shape[1]
    starts_sequence = (i % blocks_per_seq) == 0
    hist_ref[0:SHORT_HALO, :] = jnp.where(starts_sequence, 0.0, halo_ref[j])

    out_scale = jnp.where(j < qk_blocks, HEAD_DIM ** -0.5, 1.0)
    base = SHORT_HALO - (SHORT_CONV - 1)
    conv_rows = _blk(bm, 128, SUBLANES)

    def project(s):
        cols = slice(s * MXU_TILE, (s + 1) * MXU_TILE)
        hist_ref[SHORT_HALO:SHORT_HALO + bm, cols] = _dot_w(xn_ref[...], w_ref[cols, :], True)

    def conv_slab(s):
        for hb in range(s * MXU_TILE // HEAD_DIM, (s + 1) * MXU_TILE // HEAD_DIM):
            lanes = slice(hb * HEAD_DIM, (hb + 1) * HEAD_DIM)
            for r0 in range(0, bm, conv_rows):
                col = hist_ref[r0:r0 + conv_rows + SHORT_HALO, lanes]
                acc = jnp.zeros((conv_rows, HEAD_DIM), F32)
                for tap in range(SHORT_CONV):
                    ofs = base + tap
                    src = (col[ofs:ofs + conv_rows] if ofs % SUBLANES == 0
                           else pltpu.roll(col, conv_rows + SHORT_HALO - ofs, axis=0)[0:conv_rows])
                    acc = acc + cw_ref[tap:tap + 1, lanes] * src
                y = acc * _sigmoid(acc)
                if normalize:
                    y = y * (lax.rsqrt(jnp.sum(y * y, axis=-1, keepdims=True) + NORM_EPS) * out_scale)
                o_ref[r0:r0 + conv_rows, lanes] = y.astype(o_ref.dtype)

    n_slabs = bn // MXU_TILE
    project(0)
    for s in range(n_slabs):
        if s + 1 < n_slabs:
            project(s + 1)
        conv_slab(s)
    halo_ref[j] = hist_ref[bm:bm + SHORT_HALO, :]


def gdn_in_proj(x, gain, wv, short_conv, w_logit_t, seq, qk_width, bm, bn):
    m, k = x.shape
    n = wv.cols
    conv_width = short_conv.shape[1]
    bm, bn = _blk(seq, bm, 8), _blk(qk_width, bn, LANES)
    assert wv.transposed and seq % bm == 0 and n % bn == 0 and conv_width % bn == 0
    conv_blocks = conv_width // bn
    kernel = functools.partial(_gdn_in_kernel, bm=bm, blocks_per_seq=seq // bm, qk_blocks=qk_width // bn,
                               conv_blocks=conv_blocks)
    return pl.pallas_call(
        kernel,
        grid=(m // bm, n // bn),
        in_specs=[pl.BlockSpec((bm, k), lambda i, j: (i, 0)),
                  pl.BlockSpec((1, k), lambda i, j: (0, 0)),
                  _w_spec(wv, bn, lambda i, j: j),
                  pl.BlockSpec((short_conv.shape[0], bn), lambda i, j: (0, jnp.minimum(j, conv_blocks - 1))),
                  pl.BlockSpec((LANES, k), lambda i, j: (0, 0))],
        out_specs=[pl.BlockSpec((bm, bn), lambda i, j: (i, j)), pl.BlockSpec((bm, LANES), lambda i, j: (i, 0))],
        out_shape=[jax.ShapeDtypeStruct((m, n), BF16), jax.ShapeDtypeStruct((m, LANES), F32)],
        scratch_shapes=[pltpu.VMEM((bm, k), BF16), pltpu.VMEM((bm + SHORT_HALO, bn), F32),
                        pltpu.VMEM((conv_blocks, SHORT_HALO, bn), F32)],
        compiler_params=_params("arbitrary", "arbitrary"),
        name="gdn_in_proj",
    )(x, gain, wv.arr, short_conv, w_logit_t)


def _gdn_gates_kernel(ba_ref, alog_ref, dtb_ref, g_ref, *, gate_heads):
    ba = ba_ref[...]
    beta = _sigmoid(ba)
    g = -jnp.exp(alog_ref[...]) * _softplus(ba + dtb_ref[...])
    gcum = _prefix_sum_rows(g, CHUNK)
    lane = lax.broadcasted_iota(jnp.int32, ba.shape, 1)
    g_ref[...] = jnp.where(lane < gate_heads, beta, gcum)


def gdn_gates(ba, alog_pad, dtb_pad, gate_heads, ts):
    m = ba.shape[0]
    ts = _blk(m, ts, CHUNK)
    row = lambda i: (i, 0)
    const = lambda i: (0, 0)
    return pl.pallas_call(
        functools.partial(_gdn_gates_kernel, gate_heads=gate_heads),
        grid=(m // ts,),
        in_specs=[pl.BlockSpec((ts, LANES), row), pl.BlockSpec((1, LANES), const), pl.BlockSpec((1, LANES), const)],
        out_specs=pl.BlockSpec((ts, LANES), row),
        out_shape=jax.ShapeDtypeStruct((m, LANES), F32),
        compiler_params=_params("parallel"),
        name="gdn_gates",
    )(ba, alog_pad, dtb_pad)


GROUP = MXU_TILE // CHUNK


def _gdn_solve_kernel(q_ref, k_ref, g_ref, gr_ref, bdm_ref, kkm_ref, t_ref, p_ref, *, v_heads, rep):
    c = CHUNK
    gw = MXU_TILE
    n_groups = v_heads // GROUP
    n_chunks = g_ref.shape[0] // c
    row = lax.broadcasted_iota(jnp.int32, (c, gw), 0)
    lane = lax.broadcasted_iota(jnp.int32, (c, gw), 1)
    blk = lane // c
    col = lane % c
    tril = row >= col
    strict = row > col
    eye = (row == col).astype(F32)
    bd_mask = bdm_ref[...]
    kk_mask = kkm_ref[...]

    def block_diag(x):
        xb = x.astype(BF16)
        return jnp.concatenate([xb] * GROUP, axis=0) * bd_mask

    gates = [g_ref[n * c:(n + 1) * c, :] for n in range(n_chunks)]

    def per_head_columns(first_lane, u):
        n, g = units[u]
        out = gates[n][:, first_lane + GROUP * g + GROUP - 1:first_lane + GROUP * g + GROUP]
        for mth in range(GROUP - 2, -1, -1):
            cm = gates[n][:, first_lane + GROUP * g + mth:first_lane + GROUP * g + mth + 1]
            out = jnp.where(blk == mth, cm, out)
        return out

    units = [(n, g) for n in range(n_chunks) for g in range(n_groups)]
    groups = range(len(units))
    rows = [slice(n * c, (n + 1) * c) for n, _ in units]
    lanes = [slice(g * gw, (g + 1) * gw) for _, g in units]
    kq = [jnp.concatenate([k_ref[rows[u], lanes[u]], q_ref[rows[u], lanes[u]]], axis=0) for u in groups]
    rhs = [jnp.concatenate([k_ref[rows[u], lanes[u]]] * GROUP, axis=0) * kk_mask for u in groups]
    kkqk = [lax.dot_general(kq[g], rhs[g], (((1,), (1,)), ((), ())), preferred_element_type=F32) for g in groups]
    decay = [jnp.exp(jnp.where(tril, per_head_columns(v_heads, u) - gr_ref[units[u][0], :, lanes[u]], -jnp.inf))
             for u in groups]
    low = [jnp.where(strict, per_head_columns(0, g) * kkqk[g][:c] * decay[g], 0.0) for g in groups]
    for g in groups:
        p_ref[rows[g], lanes[g]] = (kkqk[g][c:] * decay[g]).astype(p_ref.dtype)

    inv = [eye - low[g] for g in groups]
    bd = [block_diag(low[g]) for g in groups]
    power = [jnp.dot(low[g].astype(BF16), bd[g], preferred_element_type=F32) for g in groups]
    span = 2
    while span <= c // 2:
        bd = [block_diag(power[g]) for g in groups]
        if span < c // 2:
            both = [jnp.dot(jnp.concatenate([power[g], inv[g]], axis=0).astype(BF16), bd[g],
                            preferred_element_type=F32) for g in groups]
            power = [both[g][:c] for g in groups]
            inv = [inv[g] + both[g][c:] for g in groups]
        else:
            inv = [inv[g] + jnp.dot(inv[g].astype(BF16), bd[g], preferred_element_type=F32) for g in groups]
        span *= 2
    for g in groups:
        t_ref[rows[g], lanes[g]] = inv[g].astype(t_ref.dtype)


def gdn_solve(qkv, gates, g_rows, v_heads, rep, chunks_per_step):
    m = qkv.shape[0]
    width = v_heads * CHUNK
    qk_width = v_heads * HEAD_DIM // rep
    assert GROUP * CHUNK == MXU_TILE and rep * HEAD_DIM == MXU_TILE and v_heads % GROUP == 0
    assert qk_width * rep == v_heads * HEAD_DIM and qk_width == width
    rb = lax.broadcasted_iota(jnp.int32, (GROUP * CHUNK, MXU_TILE), 0) // CHUNK
    ln = lax.broadcasted_iota(jnp.int32, (GROUP * CHUNK, MXU_TILE), 1)
    bd_mask = (rb == ln // CHUNK).astype(BF16)
    kk_mask = (rb // rep == ln // HEAD_DIM).astype(BF16)
    kernel = functools.partial(_gdn_solve_kernel, v_heads=v_heads, rep=rep)
    row = lambda i: (i, 0)
    const = lambda i: (0, 0)
    rows = chunks_per_step * CHUNK
    assert m % rows == 0
    return pl.pallas_call(
        kernel,
        grid=(m // rows,),
        in_specs=[pl.BlockSpec((rows, qk_width), row), pl.BlockSpec((rows, qk_width), lambda i: (i, 1)),
                  pl.BlockSpec((rows, LANES), row),
                  pl.BlockSpec((chunks_per_step, 1, width), lambda i: (i, 0, 0)),
                  pl.BlockSpec((GROUP * CHUNK, MXU_TILE), const), pl.BlockSpec((GROUP * CHUNK, MXU_TILE), const)],
        out_specs=[pl.BlockSpec((rows, width), row), pl.BlockSpec((rows, width), row)],
        out_shape=[jax.ShapeDtypeStruct((m, width), BF16), jax.ShapeDtypeStruct((m, width), BF16)],
        compiler_params=_params("parallel"),
        name="gdn_solve",
    )(qkv, qkv, gates, g_rows, bd_mask, kk_mask)


def _gdn_chunk_kernel(q_ref, k_ref, v_ref, z_ref, g_ref, t_ref, p_ref, on_ref, o_ref, state_ref, *, qk_heads, rep):
    c = CHUNK
    d = HEAD_DIM

    @pl.when(pl.program_id(1) == 0)
    def _():
        state_ref[...] = jnp.zeros(state_ref.shape, F32)

    n_v = qk_heads * rep
    pw = rep * d
    tw = rep * c
    gain = on_ref[...]
    zeros = jnp.zeros((c, d), BF16)
    pairs = range(qk_heads)

    def on_diagonal(parts):
        return jnp.concatenate(
            [jnp.concatenate([parts[r] if s == r else zeros for s in range(rep)], axis=1) for r in range(rep)],
            axis=0)

    def chunk_step(n, state):
        rows = slice(n * c, (n + 1) * c)
        gates = g_ref[rows, :]

        def head_cols(first_lane, p):
            return [gates[:, first_lane + p * rep + r:first_lane + p * rep + r + 1] for r in range(rep)]

        k = {p: k_ref[rows, p * d:(p + 1) * d] for p in pairs}
        kq = {p: jnp.concatenate([k[p], q_ref[rows, p * d:(p + 1) * d]], axis=0) for p in pairs}
        kqs = {p: jnp.dot(kq[p], state[p].astype(BF16), preferred_element_type=F32) for p in pairs}

        beta = {p: head_cols(0, p) for p in pairs}
        gc = {p: head_cols(n_v, p) for p in pairs}
        eg = {p: [jnp.exp(g) for g in gc[p]] for p in pairs}
        rhs = {p: on_diagonal([(beta[p][r] * (v_ref[rows, p * pw + r * d:p * pw + (r + 1) * d].astype(F32)
                                           - kqs[p][:c, r * d:(r + 1) * d] * eg[p][r])).astype(BF16)
                               for r in range(rep)]) for p in pairs}
        v_new = {p: jnp.dot(t_ref[rows, p * tw:(p + 1) * tw], rhs[p], preferred_element_type=F32)
                 for p in pairs}
        vn_diag = {p: on_diagonal([v_new[p][:, r * d:(r + 1) * d].astype(BF16) for r in range(rep)]) for p in pairs}
        intra = {p: jnp.dot(p_ref[rows, p * tw:(p + 1) * tw], vn_diag[p], preferred_element_type=F32)
                 for p in pairs}

        g_last = {p: [g[c - 1:c, :] for g in gc[p]] for p in pairs}
        v_dec = {p: jnp.concatenate([jnp.exp(g_last[p][r] - gc[p][r]) * v_new[p][:, r * d:(r + 1) * d]
                                     for r in range(rep)], axis=1).astype(BF16) for p in pairs}
        kt = {p: k[p].astype(F32).T.astype(BF16) for p in pairs}
        upd = {p: jnp.dot(kt[p], v_dec[p], preferred_element_type=F32) for p in pairs}
        scale = {p: jnp.concatenate([jnp.broadcast_to(jnp.exp(g_last[p][r]), (1, d)) for r in range(rep)], axis=1)
                 for p in pairs}
        new_state = {p: state[p] * scale[p] + upd[p] for p in pairs}

        for p in pairs:
            for r in range(rep):
                vl = slice(p * pw + r * d, p * pw + (r + 1) * d)
                o = kqs[p][c:, r * d:(r + 1) * d] * eg[p][r] + intra[p][:, r * d:(r + 1) * d]
                on = o * lax.rsqrt(jnp.mean(o * o, axis=-1, keepdims=True) + NORM_EPS) * gain
                zf = z_ref[rows, vl].astype(F32)
                o_ref[rows, vl] = (on * (zf * _sigmoid(zf))).astype(o_ref.dtype)
        return new_state

    state = {p: state_ref[p] for p in pairs}
    for n in range(g_ref.shape[0] // c):
        state = chunk_step(n, state)
    for p in pairs:
        state_ref[p] = state[p]


def gdn_chunk(qkvz, gates, t_inv, p_intra, o_norm, batch, seq, qk_heads, rep, chunks_per_step):
    m = batch * seq
    qk_width = qk_heads * HEAD_DIM
    v_width = qk_width * rep
    assert v_width == 2 * qk_width
    kernel = functools.partial(_gdn_chunk_kernel, qk_heads=qk_heads, rep=rep)
    rows = chunks_per_step * CHUNK
    assert seq % rows == 0
    nc = seq // rows
    row = lambda b, i: (b * nc + i, 0)
    col1 = lambda b, i: (b * nc + i, 1)
    col2 = lambda b, i: (b * nc + i, 2)
    return pl.pallas_call(
        kernel,
        grid=(batch, nc),
        in_specs=[pl.BlockSpec((rows, qk_width), row), pl.BlockSpec((rows, qk_width), col1),
                  pl.BlockSpec((rows, v_width), col1), pl.BlockSpec((rows, v_width), col2),
                  pl.BlockSpec((rows, LANES), row),
                  pl.BlockSpec((rows, t_inv.shape[1]), row), pl.BlockSpec((rows, p_intra.shape[1]), row),
                  pl.BlockSpec((1, HEAD_DIM), lambda b, i: (0, 0))],
        out_specs=pl.BlockSpec((rows, v_width), row),
        out_shape=jax.ShapeDtypeStruct((m, v_width), BF16),
        scratch_shapes=[pltpu.VMEM((qk_heads, HEAD_DIM, rep * HEAD_DIM), F32)],
        compiler_params=_params("arbitrary", "arbitrary"),
        name="gdn_chunk",
    )(qkvz, qkvz, qkvz, qkvz, gates, t_inv, p_intra, o_norm)


def _pad_cols(w, n):
    return jnp.pad(w, ((0, 0), (0, n - w.shape[1])))


def _row(v):
    return v.reshape(1, -1).astype(F32)


def kernel(x, norm_mix0, w_in0, fgate_bias0, conv_w0, conv_b0, conv_ln_g0, conv_ln_b0, w_out0, norm_mix1, w_in1,
           short_conv1, a_log1, dt_bias1, o_norm1, w_out1, norm_ffn, w_gate, w_up, w_down, final_norm):
    batch, seq, d_model = x.shape
    m = batch * seq
    fox_heads = fgate_bias0.shape[1]
    fox_width = fox_heads * HEAD_DIM
    conv_ch = conv_w0.shape[2]
    gdn_v_heads = a_log1.shape[1]
    gdn_v_width = gdn_v_heads * HEAD_DIM
    gdn_qkv_width = short_conv1.shape[2]
    gdn_qk_width = (gdn_qkv_width - gdn_v_width) // 2
    gdn_qk_heads = gdn_qk_width // HEAD_DIM
    rep = gdn_v_heads // gdn_qk_heads

    h = x.reshape(m, d_model)

    w_gate_b, w_up_b, w_down_b = w_gate.astype(BF16), w_up.astype(BF16), w_down.astype(BF16)
    w_in0_t = jnp.swapaxes(w_in0, 1, 2).astype(BF16)
    w_in1_t = jnp.swapaxes(w_in1, 1, 2).astype(BF16)

    def pad_rows(w, n):
        return jnp.pad(w, ((0, n - w.shape[0]), (0, 0)))

    def ffn(h, layer, final_gain=None):
        mid = norm_swiglu(h, _row(norm_ffn[layer]), wview(w_gate_b, layer), wview(w_up_b, layer), *TILE_SWIGLU)
        if final_gain is not None:
            return matmul_residual_rmsnorm(mid, wview(w_down_b, layer), h, final_gain, *TILE_DOWN)
        return matmul_residual([mid], [wview(w_down_b, layer)], h, *TILE_DOWN)

    g0 = _row(norm_mix0[0])
    f0, f1 = 3 * fox_width, 3 * fox_width + fox_heads
    w_qkvg = jnp.concatenate([w_in0_t[0, :f0], w_in0_t[0, f1:]], axis=0)
    qkv, f_logit = norm_matmul(h, g0, wview(w_qkvg, transposed=True), pad_rows(w_in0_t[0, f0:f1], LANES),
                               *TILE_PROJ)

    tq = ATTN_BLOCK
    c = fox_gates(f_logit, _pad_cols(_row(fgate_bias0[0]), LANES), batch, seq)
    c_heads = jnp.transpose(c.reshape(batch, seq, LANES)[:, :, :fox_heads], (0, 2, 1))
    v_t = jnp.transpose(qkv[:, 2 * fox_width:f0].reshape(batch, seq // tq, tq, fox_width), (0, 1, 3, 2))
    a_out = fox_attention(qkv, v_t, c_heads.reshape(batch, fox_heads, 1, seq),
                          c_heads.reshape(batch, fox_heads, seq, 1), batch, seq, fox_heads, tq,
                          hb=min(ATTN_HEADS_PER_STEP, fox_heads))
    u = conformer_conv(qkv, f0, jnp.pad(conv_w0[0], ((0, CONV_HALO - CONV_WIDTH), (0, 0))), _row(conv_b0[0]),
                       _row(conv_ln_g0[0]), _row(conv_ln_b0[0]), batch, seq, ts=CONFORMER_ROWS)
    h = matmul_residual([a_out, u], [wview(w_out0, 0, row0=0, rows=fox_width),
                                     wview(w_out0, 0, row0=fox_width, rows=conv_ch)], h, *TILE_OUT_FOX)
    h = ffn(h, 0)

    g1 = _row(norm_mix1[0])
    z0 = gdn_qkv_width
    qkvz1, ba1 = gdn_in_proj(h, g1, wview(w_in1_t, 0, cols=z0 + gdn_v_width, transposed=True),
                             jnp.pad(short_conv1[0], ((0, SUBLANES - SHORT_CONV), (0, 0))),
                             pad_rows(w_in1_t[0, z0 + gdn_v_width:], LANES), seq, gdn_qk_width, *TILE_PROJ)
    alog_pad = jnp.pad(_row(a_log1[0]), ((0, 0), (gdn_v_heads, LANES - 2 * gdn_v_heads)))
    dtb_pad = jnp.pad(_row(dt_bias1[0]), ((0, 0), (gdn_v_heads, LANES - 2 * gdn_v_heads)))
    gates1 = gdn_gates(ba1, alog_pad, dtb_pad, gdn_v_heads, ts=GATES_ROWS)
    g_rows = jnp.transpose(gates1[:, gdn_v_heads:2 * gdn_v_heads].reshape(m // CHUNK, CHUNK, gdn_v_heads),
                           (0, 2, 1)).reshape(m // CHUNK, 1, gdn_v_heads * CHUNK)
    t_inv, p_intra = gdn_solve(qkvz1, gates1, g_rows, gdn_v_heads, rep, chunks_per_step=SOLVE_CHUNKS_PER_STEP)
    o1 = gdn_chunk(qkvz1, gates1, t_inv, p_intra, _row(o_norm1[0]), batch, seq, gdn_qk_heads, rep,
                   chunks_per_step=SCAN_CHUNKS_PER_STEP)
    h = matmul_residual([o1], [wview(w_out1.astype(BF16), 0)], h, *TILE_OUT_GDN)
    return ffn(h, 1, final_gain=_row(final_norm)).reshape(batch, seq, d_model)
```

```python
import functools
from typing import NamedTuple, Optional

import jax
import jax.numpy as jnp
from jax import lax
from jax.experimental import pallas as pl
from jax.experimental.pallas import tpu as pltpu

F32 = jnp.float32
BF16 = jnp.bfloat16

NORM_EPS = 1e-6
LANES = 128
SUBLANES = 8
MXU_TILE = 256
HEAD_DIM = 128
CHUNK = 64
CONV_WIDTH = 31
CONV_HALO = 32
SHORT_CONV = 4
SHORT_HALO = 8
VMEM_LIMIT_BYTES = 56 * 1024 * 1024

TILE_PROJ = (1024, 1024)
TILE_SWIGLU = (1024, 512)
TILE_DOWN = (512, 1024)
TILE_OUT_FOX = (1024, 1024)
TILE_OUT_GDN = (512, 1024)
ATTN_BLOCK = 256
ATTN_HEADS_PER_STEP = 8
CONFORMER_ROWS = 256
GATES_ROWS = 1024
SOLVE_CHUNKS_PER_STEP = 4
SCAN_CHUNKS_PER_STEP = 2


def _params(*sem):
    return pltpu.CompilerParams(dimension_semantics=sem, vmem_limit_bytes=VMEM_LIMIT_BYTES)


def _blk(dim, pref, align):
    best = dim
    for cand in range(align, min(dim, pref) + 1, align):
        if dim % cand == 0:
            best = cand
    return best


def _sigmoid(x):
    return 1.0 / (1.0 + jnp.exp(-x))


def _softplus(x):
    return jnp.maximum(x, 0.0) + jnp.log(1.0 + jnp.exp(-jnp.abs(x)))


def _rms_to_scratch(x_ref, g_ref, xn_ref):
    x = x_ref[...].astype(F32)
    ms = jnp.mean(x * x, axis=-1, keepdims=True)
    xn_ref[...] = (x * lax.rsqrt(ms + NORM_EPS) * g_ref[...]).astype(xn_ref.dtype)


class WeightView(NamedTuple):
    arr: jax.Array
    lead: Optional[int]
    row0: int
    rows: int
    col0: int
    cols: int
    transposed: bool


def wview(arr, lead=None, row0=0, rows=None, col0=0, cols=None, transposed=False):
    k, n = arr.shape[-2:][::-1] if transposed else arr.shape[-2:]
    return WeightView(arr, lead, row0, k - row0 if rows is None else rows, col0, n - col0 if cols is None else cols,
                      transposed)


def _w_spec(wv, bn, col_block):
    assert wv.row0 % wv.rows == 0 and wv.col0 % bn == 0 and wv.cols % bn == 0
    rb, cb = wv.row0 // wv.rows, wv.col0 // bn
    shape = (bn, wv.rows) if wv.transposed else (wv.rows, bn)
    index = (lambda *g: (cb + col_block(*g), rb)) if wv.transposed else (lambda *g: (rb, cb + col_block(*g)))
    if wv.arr.ndim == 3:
        return pl.BlockSpec((None,) + shape, lambda *g: (wv.lead,) + index(*g))
    return pl.BlockSpec(shape, index)


def _dot_w(x, w, transposed):
    dims = (((1,), (1,)), ((), ())) if transposed else (((1,), (0,)), ((), ()))
    return lax.dot_general(x, w.astype(BF16), dims, preferred_element_type=F32)


def _norm_mm_kernel(x_ref, g_ref, w_ref, wl_ref, o_ref, logit_ref, xn_ref, *, transposed):
    @pl.when(pl.program_id(1) == 0)
    def _():
        _rms_to_scratch(x_ref, g_ref, xn_ref)
        logit_ref[...] = _dot_w(xn_ref[...], wl_ref[...], True)

    o_ref[...] = _dot_w(xn_ref[...], w_ref[...], transposed).astype(o_ref.dtype)


def norm_matmul(x, gain, wv, w_logit_t, bm, bn):
    m, k = x.shape
    n = wv.cols
    bm, bn = _blk(m, bm, 8), _blk(n, bn, LANES)
    return pl.pallas_call(
        functools.partial(_norm_mm_kernel, transposed=wv.transposed),
        grid=(m // bm, n // bn),
        in_specs=[pl.BlockSpec((bm, k), lambda i, j: (i, 0)),
                  pl.BlockSpec((1, k), lambda i, j: (0, 0)),
                  _w_spec(wv, bn, lambda i, j: j),
                  pl.BlockSpec((LANES, k), lambda i, j: (0, 0))],
        out_specs=[pl.BlockSpec((bm, bn), lambda i, j: (i, j)), pl.BlockSpec((bm, LANES), lambda i, j: (i, 0))],
        out_shape=[jax.ShapeDtypeStruct((m, n), BF16), jax.ShapeDtypeStruct((m, LANES), F32)],
        scratch_shapes=[pltpu.VMEM((bm, k), BF16)],
        compiler_params=_params("parallel", "arbitrary"),
        name="norm_matmul",
    )(x, gain, wv.arr, w_logit_t)


def _norm_swiglu_kernel(x_ref, g_ref, wg_ref, wu_ref, o_ref, xn_ref):
    @pl.when(pl.program_id(1) == 0)
    def _():
        _rms_to_scratch(x_ref, g_ref, xn_ref)

    xn = xn_ref[...]
    gate = jnp.dot(xn, wg_ref[...].astype(BF16), preferred_element_type=F32)
    up = jnp.dot(xn, wu_ref[...].astype(BF16), preferred_element_type=F32)
    o_ref[...] = (gate * _sigmoid(gate) * up).astype(o_ref.dtype)


def norm_swiglu(x, gain, wg, wu, bm, bn):
    m, k = x.shape
    n = wg.cols
    bm, bn = _blk(m, bm, 8), _blk(n, bn, LANES)
    return pl.pallas_call(
        _norm_swiglu_kernel,
        grid=(m // bm, n // bn),
        in_specs=[pl.BlockSpec((bm, k), lambda i, j: (i, 0)),
                  pl.BlockSpec((1, k), lambda i, j: (0, 0)),
                  _w_spec(wg, bn, lambda i, j: j),
                  _w_spec(wu, bn, lambda i, j: j)],
        out_specs=pl.BlockSpec((bm, bn), lambda i, j: (i, j)),
        out_shape=jax.ShapeDtypeStruct((m, n), BF16),
        scratch_shapes=[pltpu.VMEM((bm, k), BF16)],
        compiler_params=_params("parallel", "arbitrary"),
        name="norm_swiglu",
    )(x, gain, wg.arr, wu.arr)


def _mm_res_kernel(*refs, n_pairs):
    xs = refs[:n_pairs]
    ws = refs[n_pairs:2 * n_pairs]
    res_ref = refs[2 * n_pairs]
    o_ref = refs[2 * n_pairs + 1]
    wb = refs[2 * n_pairs + 2:]

    @pl.when(pl.program_id(1) == 0)
    def _():
        for w_ref, wb_ref in zip([w for w in ws if w.dtype != BF16], wb):
            wb_ref[...] = w_ref[...].astype(BF16)

    wb = list(wb)
    acc = res_ref[...]
    for x_ref, w_ref in zip(xs, ws):
        w = w_ref[...] if w_ref.dtype == BF16 else wb.pop(0)[...]
        acc = acc + jnp.dot(x_ref[...], w, preferred_element_type=F32)
    o_ref[...] = acc


def matmul_residual(xs, wvs, res, bm, bn):
    m, n = res.shape
    bm, bn = _blk(m, bm, 8), _blk(n, bn, LANES)
    n_pairs = len(xs)
    in_specs = ([pl.BlockSpec((bm, x.shape[1]), lambda j, i: (i, 0)) for x in xs]
                + [_w_spec(wv, bn, lambda j, i: j) for wv in wvs]
                + [pl.BlockSpec((bm, bn), lambda j, i: (i, j))])
    return pl.pallas_call(
        functools.partial(_mm_res_kernel, n_pairs=n_pairs),
        grid=(n // bn, m // bm),
        in_specs=in_specs,
        out_specs=pl.BlockSpec((bm, bn), lambda j, i: (i, j)),
        out_shape=jax.ShapeDtypeStruct((m, n), F32),
        scratch_shapes=[pltpu.VMEM((wv.rows, bn), BF16) for wv in wvs if wv.arr.dtype != BF16],
        compiler_params=_params("parallel", "arbitrary"),
        name="matmul_residual",
    )(*xs, *[wv.arr for wv in wvs], res)


def _mm_res_norm_kernel(x_ref, w_ref, res_ref, g_ref, o_ref, row_ref):
    j = pl.program_id(1)
    nj, _, bn = row_ref.shape
    row_ref[j] = res_ref[...] + jnp.dot(x_ref[...], w_ref[...], preferred_element_type=F32)

    @pl.when(j == nj - 1)
    def _():
        parts = [row_ref[jb] for jb in range(nj)]
        ms = sum(jnp.sum(p * p, axis=-1, keepdims=True) for p in parts) / (nj * bn)
        inv = lax.rsqrt(ms + NORM_EPS)
        for jb in range(nj):
            o_ref[:, jb * bn:(jb + 1) * bn] = parts[jb] * inv * g_ref[:, jb * bn:(jb + 1) * bn]


def matmul_residual_rmsnorm(x, wv, res, gain, bm, bn):
    m, n = res.shape
    bm, bn = _blk(m, bm, 8), _blk(n, bn, LANES)
    assert wv.arr.dtype == BF16 and not wv.transposed
    return pl.pallas_call(
        _mm_res_norm_kernel,
        grid=(m // bm, n // bn),
        in_specs=[pl.BlockSpec((bm, x.shape[1]), lambda i, j: (i, 0)),
                  _w_spec(wv, bn, lambda i, j: j),
                  pl.BlockSpec((bm, bn), lambda i, j: (i, j)),
                  pl.BlockSpec((1, n), lambda i, j: (0, 0))],
        out_specs=pl.BlockSpec((bm, n), lambda i, j: (i, 0)),
        out_shape=jax.ShapeDtypeStruct((m, n), F32),
        scratch_shapes=[pltpu.VMEM((n // bn, bm, bn), F32)],
        compiler_params=_params("parallel", "arbitrary"),
        name="matmul_residual_rmsnorm",
    )(x, wv.arr, res, gain)


def _prefix_sum_rows(x, period):
    row = lax.broadcasted_iota(jnp.int32, x.shape, 0) % period
    shift = 1
    while shift < period:
        x = x + jnp.where(row >= shift, pltpu.roll(x, shift, axis=0), 0.0)
        shift *= 2
    return x


def _fox_gates_kernel(f_ref, b_ref, c_ref):
    z = f_ref[...] + b_ref[...]
    log_f = jnp.minimum(z, 0.0) - jnp.log(1.0 + jnp.exp(-jnp.abs(z)))
    c_ref[...] = _prefix_sum_rows(log_f, log_f.shape[0])


def fox_gates(f_logit, f_bias, batch, seq):
    return pl.pallas_call(
        _fox_gates_kernel,
        grid=(batch,),
        in_specs=[pl.BlockSpec((seq, LANES), lambda b: (b, 0)), pl.BlockSpec((1, LANES), lambda b: (0, 0))],
        out_specs=pl.BlockSpec((seq, LANES), lambda b: (b, 0)),
        out_shape=jax.ShapeDtypeStruct((batch * seq, LANES), F32),
        compiler_params=_params("parallel"),
        name="fox_gates",
    )(f_logit, f_bias)


def _fox_attn_kernel(q_ref, k_ref, vt_ref, cq_ref, ck_ref, o_ref, m_ref, l_ref, acc_ref, *, tq, hb, scale):
    i = pl.program_id(2)
    d = HEAD_DIM
    heads = range(hb)
    q = [q_ref[:, h * d:(h + 1) * d] for h in heads]
    cq = [cq_ref[0, h] for h in heads]
    for h in heads:
        m_ref[h] = jnp.full((1, tq), -jnp.inf, F32)
        l_ref[h] = jnp.zeros((1, tq), F32)
        acc_ref[h] = jnp.zeros((d, tq), F32)
    key_after_query = (lax.broadcasted_iota(jnp.int32, (tq, tq), 0) > lax.broadcasted_iota(jnp.int32, (tq, tq), 1))

    def block(j, on_diagonal):
        start = pl.multiple_of(j * tq, tq)
        s = [lax.dot_general(k_ref[pl.ds(start, tq), h * d:(h + 1) * d], q[h], (((1,), (1,)), ((), ())),
                             preferred_element_type=F32) for h in heads]
        x = [s[h] * scale + (cq[h] - ck_ref[0, h, pl.ds(start, tq), :]) for h in heads]
        if on_diagonal:
            x = [jnp.where(key_after_query, -jnp.inf, x[h]) for h in heads]
        m_old = [m_ref[h] for h in heads]
        m_new = [jnp.maximum(m_old[h], jnp.max(x[h], axis=0, keepdims=True)) for h in heads]
        alpha = [jnp.exp(m_old[h] - m_new[h]) for h in heads]
        p = [jnp.exp(x[h] - m_new[h]) for h in heads]
        pv = [jnp.dot(vt_ref[0, j, h * d:(h + 1) * d, :], p[h].astype(BF16), preferred_element_type=F32)
              for h in heads]
        for h in heads:
            m_ref[h] = m_new[h]
            l_ref[h] = alpha[h] * l_ref[h] + jnp.sum(p[h], axis=0, keepdims=True)
            acc_ref[h] = alpha[h] * acc_ref[h] + pv[h]

    def body(j, carry):
        block(j, False)
        return carry

    lax.fori_loop(0, i, body, 0)
    block(i, True)
    for h in heads:
        o_ref[:, h * d:(h + 1) * d] = (acc_ref[h] / l_ref[h]).T.astype(o_ref.dtype)


def fox_attention(qkv, v_t, c_row, c_col, batch, seq, heads, tq, hb):
    nq = seq // tq
    d = HEAD_DIM
    groups = heads // hb
    kernel = functools.partial(_fox_attn_kernel, tq=tq, hb=hb, scale=d ** -0.5)
    return pl.pallas_call(
        kernel,
        grid=(batch, groups, nq),
        in_specs=[pl.BlockSpec((tq, hb * d), lambda b, g, i: (b * nq + i, g)),
                  pl.BlockSpec((seq, hb * d), lambda b, g, i: (b, groups + g)),
                  pl.BlockSpec((1, nq, hb * d, tq), lambda b, g, i: (b, 0, g, 0)),
                  pl.BlockSpec((1, hb, 1, tq), lambda b, g, i: (b, g, 0, i)),
                  pl.BlockSpec((1, hb, seq, 1), lambda b, g, i: (b, g, 0, 0))],
        out_specs=pl.BlockSpec((tq, hb * d), lambda b, g, i: (b * nq + i, g)),
        out_shape=jax.ShapeDtypeStruct((batch * seq, heads * d), BF16),
        scratch_shapes=[pltpu.VMEM((hb, 1, tq), F32), pltpu.VMEM((hb, 1, tq), F32), pltpu.VMEM((hb, d, tq), F32)],
        compiler_params=_params("parallel", "parallel", "arbitrary"),
        name="fox_attention",
    )(qkv, qkv, v_t, c_row, c_col)


def _conformer_kernel(val_ref, gate_ref, w_ref, cb_ref, lg_ref, lb_ref, o_ref, hist_ref, shift_ref, y_ref,
                      *, ts, rows):
    channels = val_ref.shape[1]

    @pl.when(pl.program_id(1) == 0)
    def _():
        hist_ref[0:CONV_HALO, :] = jnp.zeros((CONV_HALO, channels), F32)

    hist_ref[CONV_HALO:CONV_HALO + ts, :] = val_ref[...].astype(F32) * _sigmoid(gate_ref[...].astype(F32))

    base = CONV_HALO - (CONV_WIDTH - 1)
    hist_rows = ts + CONV_HALO
    for cb in range(channels // LANES):
        lanes = slice(cb * LANES, (cb + 1) * LANES)
        col = hist_ref[:, lanes]
        for b in range(1, SUBLANES):
            shift_ref[b - 1, :, lanes] = pltpu.roll(col, hist_rows - b, axis=0)
        for rb in range(ts // rows):
            acc = jnp.zeros((rows, LANES), F32)
            for j in range(CONV_WIDTH):
                a, b = divmod(base + j, SUBLANES)
                r0 = rb * rows + a * SUBLANES
                src = hist_ref[r0:r0 + rows, lanes] if b == 0 else shift_ref[b - 1, r0:r0 + rows, lanes]
                acc = acc + w_ref[j:j + 1, lanes] * src
            y_ref[rb * rows:(rb + 1) * rows, lanes] = acc + cb_ref[:, lanes]

    hist_ref[0:CONV_HALO, :] = hist_ref[ts:ts + CONV_HALO, :]

    y = y_ref[...]
    mu = jnp.mean(y, axis=-1, keepdims=True)
    yc = y - mu
    var = jnp.mean(yc * yc, axis=-1, keepdims=True)
    yn = yc * lax.rsqrt(var + NORM_EPS) * lg_ref[...] + lb_ref[...]
    o_ref[...] = (yn * _sigmoid(yn)).astype(o_ref.dtype)


def conformer_conv(glu, glu_col0, conv_w, conv_b, ln_g, ln_b, batch, seq, ts):
    channels = conv_w.shape[1]
    assert glu_col0 % channels == 0
    cb = glu_col0 // channels
    ns = seq // ts
    kernel = functools.partial(_conformer_kernel, ts=ts, rows=64)
    vec = lambda: pl.BlockSpec((1, channels), lambda b, i: (0, 0))
    return pl.pallas_call(
        kernel,
        grid=(batch, ns),
        in_specs=[pl.BlockSpec((ts, channels), lambda b, i: (b * ns + i, cb)),
                  pl.BlockSpec((ts, channels), lambda b, i: (b * ns + i, cb + 1)),
                  pl.BlockSpec((conv_w.shape[0], channels), lambda b, i: (0, 0)),
                  vec(), vec(), vec()],
        out_specs=pl.BlockSpec((ts, channels), lambda b, i: (b * ns + i, 0)),
        out_shape=jax.ShapeDtypeStruct((batch * seq, channels), BF16),
        scratch_shapes=[pltpu.VMEM((ts + CONV_HALO, channels), F32),
                        pltpu.VMEM((SUBLANES - 1, ts + CONV_HALO, channels), F32),
                        pltpu.VMEM((ts, channels), F32)],
        compiler_params=_params("arbitrary", "arbitrary"),
        name="conformer_conv",
    )(glu, glu, conv_w, conv_b, ln_g, ln_b)


def _gdn_in_kernel(x_ref, g_ref, w_ref, cw_ref, wl_ref, o_ref, logit_ref, xn_ref, hist_ref, halo_ref,
                   *, bm, blocks_per_seq, qk_blocks, conv_blocks):
    i = pl.program_id(0)
    j = pl.program_id(1)

    @pl.when(jnp.logical_and(i == 0, j == 0))
    def _():
        halo_ref[...] = jnp.zeros(halo_ref.shape, F32)

    @pl.when(j == 0)
    def _():
        _rms_to_scratch(x_ref, g_ref, xn_ref)
        logit_ref[...] = _dot_w(xn_ref[...], wl_ref[...], True)

    @pl.when(j >= conv_blocks)
    def _():
        o_ref[...] = _dot_w(xn_ref[...], w_ref[...], True).astype(o_ref.dtype)

    conv_args = dict(bm=bm, blocks_per_seq=blocks_per_seq, qk_blocks=qk_blocks)

    @pl.when(j < 2 * qk_blocks)
    def _():
        _gdn_conv_block(i, j, w_ref, cw_ref, o_ref, xn_ref, hist_ref, halo_ref, normalize=True, **conv_args)

    @pl.when(jnp.logical_and(j >= 2 * qk_blocks, j < conv_blocks))
    def _():
        _gdn_conv_block(i, j, w_ref, cw_ref, o_ref, xn_ref, hist_ref, halo_ref, normalize=False, **conv_args)


def _gdn_conv_block(i, j, w_ref, cw_ref, o_ref, xn_ref, hist_ref, halo_ref, *, bm, blocks_per_seq, qk_blocks,
                    normalize):
    bn = o_ref.shape[1]
    starts_sequence = (i % blocks_per_seq) == 0
    hist_ref[0:SHORT_HALO, :] = jnp.where(starts_sequence, 0.0, halo_ref[j])

    out_scale = jnp.where(j < qk_blocks, HEAD_DIM ** -0.5, 1.0)
    base = SHORT_HALO - (SHORT_CONV - 1)
    conv_rows = _blk(bm, 128, SUBLANES)

    def project(s):
        cols = slice(s * MXU_TILE, (s + 1) * MXU_TILE)
        hist_ref[SHORT_HALO:SHORT_HALO + bm, cols] = _dot_w(xn_ref[...], w_ref[cols, :], True)

    def conv_slab(s):
        for hb in range(s * MXU_TILE // HEAD_DIM, (s + 1) * MXU_TILE // HEAD_DIM):
            lanes = slice(hb * HEAD_DIM, (hb + 1) * HEAD_DIM)
            for r0 in range(0, bm, conv_rows):
                col = hist_ref[r0:r0 + conv_rows + SHORT_HALO, lanes]
                acc = jnp.zeros((conv_rows, HEAD_DIM), F32)
                for tap in range(SHORT_CONV):
                    ofs = base + tap
                    src = (col[ofs:ofs + conv_rows] if ofs % SUBLANES == 0
                           else pltpu.roll(col, conv_rows + SHORT_HALO - ofs, axis=0)[0:conv_rows])
                    acc = acc + cw_ref[tap:tap + 1, lanes] * src
                y = acc * _sigmoid(acc)
                if normalize:
                    y = y * (lax.rsqrt(jnp.sum(y * y, axis=-1, keepdims=True) + NORM_EPS) * out_scale)
                o_ref[r0:r0 + conv_rows, lanes] = y.astype(o_ref.dtype)

    n_slabs = bn // MXU_TILE
    project(0)
    for s in range(n_slabs):
        if s + 1 < n_slabs:
            project(s + 1)
        conv_slab(s)
    halo_ref[j] = hist_ref[bm:bm + SHORT_HALO, :]


def gdn_in_proj(x, gain, wv, short_conv, w_logit_t, seq, qk_width, bm, bn):
    m, k = x.shape
    n = wv.cols
    conv_width = short_conv.shape[1]
    bm, bn = _blk(seq, bm, 8), _blk(qk_width, bn, LANES)
    assert wv.transposed and seq % bm == 0 and n % bn == 0 and conv_width % bn == 0
    conv_blocks = conv_width // bn
    kernel = functools.partial(_gdn_in_kernel, bm=bm, blocks_per_seq=seq // bm, qk_blocks=qk_width // bn,
                               conv_blocks=conv_blocks)
    return pl.pallas_call(
        kernel,
        grid=(m // bm, n // bn),
        in_specs=[pl.BlockSpec((bm, k), lambda i, j: (i, 0)),
                  pl.BlockSpec((1, k), lambda i, j: (0, 0)),
                  _w_spec(wv, bn, lambda i, j: j),
                  pl.BlockSpec((short_conv.shape[0], bn), lambda i, j: (0, jnp.minimum(j, conv_blocks - 1))),
                  pl.BlockSpec((LANES, k), lambda i, j: (0, 0))],
        out_specs=[pl.BlockSpec((bm, bn), lambda i, j: (i, j)), pl.BlockSpec((bm, LANES), lambda i, j: (i, 0))],
        out_shape=[jax.ShapeDtypeStruct((m, n), BF16), jax.ShapeDtypeStruct((m, LANES), F32)],
        scratch_shapes=[pltpu.VMEM((bm, k), BF16), pltpu.VMEM((bm + SHORT_HALO, bn), F32),
                        pltpu.VMEM((conv_blocks, SHORT_HALO, bn), F32)],
        compiler_params=_params("arbitrary", "arbitrary"),
        name="gdn_in_proj",
    )(x, gain, wv.arr, short_conv, w_logit_t)


def _gdn_gates_kernel(ba_ref, alog_ref, dtb_ref, g_ref, *, gate_heads):
    ba = ba_ref[...]
    beta = _sigmoid(ba)
    g = -jnp.exp(alog_ref[...]) * _softplus(ba + dtb_ref[...])
    gcum = _prefix_sum_rows(g, CHUNK)
    lane = lax.broadcasted_iota(jnp.int32, ba.shape, 1)
    g_ref[...] = jnp.where(lane < gate_heads, beta, gcum)


def gdn_gates(ba, alog_pad, dtb_pad, gate_heads, ts):
    m = ba.shape[0]
    ts = _blk(m, ts, CHUNK)
    row = lambda i: (i, 0)
    const = lambda i: (0, 0)
    return pl.pallas_call(
        functools.partial(_gdn_gates_kernel, gate_heads=gate_heads),
        grid=(m // ts,),
        in_specs=[pl.BlockSpec((ts, LANES), row), pl.BlockSpec((1, LANES), const), pl.BlockSpec((1, LANES), const)],
        out_specs=pl.BlockSpec((ts, LANES), row),
        out_shape=jax.ShapeDtypeStruct((m, LANES), F32),
        compiler_params=_params("parallel"),
        name="gdn_gates",
    )(ba, alog_pad, dtb_pad)


GROUP = MXU_TILE // CHUNK


def _gdn_solve_kernel(q_ref, k_ref, g_ref, gr_ref, bdm_ref, kkm_ref, t_ref, p_ref, *, v_heads, rep):
    c = CHUNK
    gw = MXU_TILE
    n_groups = v_heads // GROUP
    n_chunks = g_ref.shape[0] // c
    row = lax.broadcasted_iota(jnp.int32, (c, gw), 0)
    lane = lax.broadcasted_iota(jnp.int32, (c, gw), 1)
    blk = lane // c
    col = lane % c
    tril = row >= col
    strict = row > col
    eye = (row == col).astype(F32)
    bd_mask = bdm_ref[...]
    kk_mask = kkm_ref[...]

    def block_diag(x):
        xb = x.astype(BF16)
        return jnp.concatenate([xb] * GROUP, axis=0) * bd_mask

    gates = [g_ref[n * c:(n + 1) * c, :] for n in range(n_chunks)]

    def per_head_columns(first_lane, u):
        n, g = units[u]
        out = gates[n][:, first_lane + GROUP * g + GROUP - 1:first_lane + GROUP * g + GROUP]
        for mth in range(GROUP - 2, -1, -1):
            cm = gates[n][:, first_lane + GROUP * g + mth:first_lane + GROUP * g + mth + 1]
            out = jnp.where(blk == mth, cm, out)
        return out

    units = [(n, g) for n in range(n_chunks) for g in range(n_groups)]
    groups = range(len(units))
    rows = [slice(n * c, (n + 1) * c) for n, _ in units]
    lanes = [slice(g * gw, (g + 1) * gw) for _, g in units]
    kq = [jnp.concatenate([k_ref[rows[u], lanes[u]], q_ref[rows[u], lanes[u]]], axis=0) for u in groups]
    rhs = [jnp.concatenate([k_ref[rows[u], lanes[u]]] * GROUP, axis=0) * kk_mask for u in groups]
    kkqk = [lax.dot_general(kq[g], rhs[g], (((1,), (1,)), ((), ())), preferred_element_type=F32) for g in groups]
    decay = [jnp.exp(jnp.where(tril, per_head_columns(v_heads, u) - gr_ref[units[u][0], :, lanes[u]], -jnp.inf))
             for u in groups]
    low = [jnp.where(strict, per_head_columns(0, g) * kkqk[g][:c] * decay[g], 0.0) for g in groups]
    for g in groups:
        p_ref[rows[g], lanes[g]] = (kkqk[g][c:] * decay[g]).astype(p_ref.dtype)

    inv = [eye - low[g] for g in groups]
    bd = [block_diag(low[g]) for g in groups]
    power = [jnp.dot(low[g].astype(BF16), bd[g], preferred_element_type=F32) for g in groups]
    span = 2
    while span <= c // 2:
        bd = [block_diag(power[g]) for g in groups]
        if span < c // 2:
            both = [jnp.dot(jnp.concatenate([power[g], inv[g]], axis=0).astype(BF16), bd[g],
                            preferred_element_type=F32) for g in groups]
            power = [both[g][:c] for g in groups]
            inv = [inv[g] + both[g][c:] for g in groups]
        else:
            inv = [inv[g] + jnp.dot(inv[g].astype(BF16), bd[g], preferred_element_type=F32) for g in groups]
        span *= 2
    for g in groups:
        t_ref[rows[g], lanes[g]] = inv[g].astype(t_ref.dtype)


def gdn_solve(qkv, gates, g_rows, v_heads, rep, chunks_per_step):
    m = qkv.shape[0]
    width = v_heads * CHUNK
    qk_width = v_heads * HEAD_DIM // rep
    assert GROUP * CHUNK == MXU_TILE and rep * HEAD_DIM == MXU_TILE and v_heads % GROUP == 0
    assert qk_width * rep == v_heads * HEAD_DIM and qk_width == width
    rb = lax.broadcasted_iota(jnp.int32, (GROUP * CHUNK, MXU_TILE), 0) // CHUNK
    ln = lax.broadcasted_iota(jnp.int32, (GROUP * CHUNK, MXU_TILE), 1)
    bd_mask = (rb == ln // CHUNK).astype(BF16)
    kk_mask = (rb // rep == ln // HEAD_DIM).astype(BF16)
    kernel = functools.partial(_gdn_solve_kernel, v_heads=v_heads, rep=rep)
    row = lambda i: (i, 0)
    const = lambda i: (0, 0)
    rows = chunks_per_step * CHUNK
    assert m % rows == 0
    return pl.pallas_call(
        kernel,
        grid=(m // rows,),
        in_specs=[pl.BlockSpec((rows, qk_width), row), pl.BlockSpec((rows, qk_width), lambda i: (i, 1)),
                  pl.BlockSpec((rows, LANES), row),
                  pl.BlockSpec((chunks_per_step, 1, width), lambda i: (i, 0, 0)),
                  pl.BlockSpec((GROUP * CHUNK, MXU_TILE), const), pl.BlockSpec((GROUP * CHUNK, MXU_TILE), const)],
        out_specs=[pl.BlockSpec((rows, width), row), pl.BlockSpec((rows, width), row)],
        out_shape=[jax.ShapeDtypeStruct((m, width), BF16), jax.ShapeDtypeStruct((m, width), BF16)],
        compiler_params=_params("parallel"),
        name="gdn_solve",
    )(qkv, qkv, gates, g_rows, bd_mask, kk_mask)


def _gdn_chunk_kernel(q_ref, k_ref, v_ref, z_ref, g_ref, t_ref, p_ref, on_ref, o_ref, state_ref, *, qk_heads, rep):
    c = CHUNK
    d = HEAD_DIM

    @pl.when(pl.program_id(1) == 0)
    def _():
        state_ref[...] = jnp.zeros(state_ref.shape, F32)

    n_v = qk_heads * rep
    pw = rep * d
    tw = rep * c
    gain = on_ref[...]
    zeros = jnp.zeros((c, d), BF16)
    pairs = range(qk_heads)

    def on_diagonal(parts):
        return jnp.concatenate(
            [jnp.concatenate([parts[r] if s == r else zeros for s in range(rep)], axis=1) for r in range(rep)],
            axis=0)

    def chunk_step(n, state):
        rows = slice(n * c, (n + 1) * c)
        gates = g_ref[rows, :]

        def head_cols(first_lane, p):
            return [gates[:, first_lane + p * rep + r:first_lane + p * rep + r + 1] for r in range(rep)]

        k = {p: k_ref[rows, p * d:(p + 1) * d] for p in pairs}
        kq = {p: jnp.concatenate([k[p], q_ref[rows, p * d:(p + 1) * d]], axis=0) for p in pairs}
        kqs = {p: jnp.dot(kq[p], state[p].astype(BF16), preferred_element_type=F32) for p in pairs}

        beta = {p: head_cols(0, p) for p in pairs}
        gc = {p: head_cols(n_v, p) for p in pairs}
        eg = {p: [jnp.exp(g) for g in gc[p]] for p in pairs}
        rhs = {p: on_diagonal([(beta[p][r] * (v_ref[rows, p * pw + r * d:p * pw + (r + 1) * d].astype(F32)
                                           - kqs[p][:c, r * d:(r + 1) * d] * eg[p][r])).astype(BF16)
                               for r in range(rep)]) for p in pairs}
        v_new = {p: jnp.dot(t_ref[rows, p * tw:(p + 1) * tw], rhs[p], preferred_element_type=F32)
                 for p in pairs}
        vn_diag = {p: on_diagonal([v_new[p][:, r * d:(r + 1) * d].astype(BF16) for r in range(rep)]) for p in pairs}
        intra = {p: jnp.dot(p_ref[rows, p * tw:(p + 1) * tw], vn_diag[p], preferred_element_type=F32)
                 for p in pairs}

        g_last = {p: [g[c - 1:c, :] for g in gc[p]] for p in pairs}
        v_dec = {p: jnp.concatenate([jnp.exp(g_last[p][r] - gc[p][r]) * v_new[p][:, r * d:(r + 1) * d]
                                     for r in range(rep)], axis=1).astype(BF16) for p in pairs}
        kt = {p: k[p].astype(F32).T.astype(BF16) for p in pairs}
        upd = {p: jnp.dot(kt[p], v_dec[p], preferred_element_type=F32) for p in pairs}
        scale = {p: jnp.concatenate([jnp.broadcast_to(jnp.exp(g_last[p][r]), (1, d)) for r in range(rep)], axis=1)
                 for p in pairs}
        new_state = {p: state[p] * scale[p] + upd[p] for p in pairs}

        for p in pairs:
            for r in range(rep):
                vl = slice(p * pw + r * d, p * pw + (r + 1) * d)
                o = kqs[p][c:, r * d:(r + 1) * d] * eg[p][r] + intra[p][:, r * d:(r + 1) * d]
                on = o * lax.rsqrt(jnp.mean(o * o, axis=-1, keepdims=True) + NORM_EPS) * gain
                zf = z_ref[rows, vl].astype(F32)
                o_ref[rows, vl] = (on * (zf * _sigmoid(zf))).astype(o_ref.dtype)
        return new_state

    state = {p: state_ref[p] for p in pairs}
    for n in range(g_ref.shape[0] // c):
        state = chunk_step(n, state)
    for p in pairs:
        state_ref[p] = state[p]


def gdn_chunk(qkvz, gates, t_inv, p_intra, o_norm, batch, seq, qk_heads, rep, chunks_per_step):
    m = batch * seq
    qk_width = qk_heads * HEAD_DIM
    v_width = qk_width * rep
    assert v_width == 2 * qk_width
    kernel = functools.partial(_gdn_chunk_kernel, qk_heads=qk_heads, rep=rep)
    rows = chunks_per_step * CHUNK
    assert seq % rows == 0
    nc = seq // rows
    row = lambda b, i: (b * nc + i, 0)
    col1 = lambda b, i: (b * nc + i, 1)
    col2 = lambda b, i: (b * nc + i, 2)
    return pl.pallas_call(
        kernel,
        grid=(batch, nc),
        in_specs=[pl.BlockSpec((rows, qk_width), row), pl.BlockSpec((rows, qk_width), col1),
                  pl.BlockSpec((rows, v_width), col1), pl.BlockSpec((rows, v_width), col2),
                  pl.BlockSpec((rows, LANES), row),
                  pl.BlockSpec((rows, t_inv.shape[1]), row), pl.BlockSpec((rows, p_intra.shape[1]), row),
                  pl.BlockSpec((1, HEAD_DIM), lambda b, i: (0, 0))],
        out_specs=pl.BlockSpec((rows, v_width), row),
        out_shape=jax.ShapeDtypeStruct((m, v_width), BF16),
        scratch_shapes=[pltpu.VMEM((qk_heads, HEAD_DIM, rep * HEAD_DIM), F32)],
        compiler_params=_params("arbitrary", "arbitrary"),
        name="gdn_chunk",
    )(qkvz, qkvz, qkvz, qkvz, gates, t_inv, p_intra, o_norm)


def _pad_cols(w, n):
    return jnp.pad(w, ((0, 0), (0, n - w.shape[1])))


def _row(v):
    return v.reshape(1, -1).astype(F32)


def kernel(x, norm_mix0, w_in0, fgate_bias0, conv_w0, conv_b0, conv_ln_g0, conv_ln_b0, w_out0, norm_mix1, w_in1,
           short_conv1, a_log1, dt_bias1, o_norm1, w_out1, norm_ffn, w_gate, w_up, w_down, final_norm):
    batch, seq, d_model = x.shape
    m = batch * seq
    fox_heads = fgate_bias0.shape[1]
    fox_width = fox_heads * HEAD_DIM
    conv_ch = conv_w0.shape[2]
    gdn_v_heads = a_log1.shape[1]
    gdn_v_width = gdn_v_heads * HEAD_DIM
    gdn_qkv_width = short_conv1.shape[2]
    gdn_qk_width = (gdn_qkv_width - gdn_v_width) // 2
    gdn_qk_heads = gdn_qk_width // HEAD_DIM
    rep = gdn_v_heads // gdn_qk_heads

    h = x.reshape(m, d_model)

    w_gate_b, w_up_b, w_down_b = w_gate, w_up, w_down.astype(BF16)
    w_in0_t = jnp.swapaxes(w_in0, 1, 2).astype(BF16)
    w_in1_t = jnp.swapaxes(w_in1, 1, 2).astype(BF16)

    def pad_rows(w, n):
        return jnp.pad(w, ((0, n - w.shape[0]), (0, 0)))

    def ffn(h, layer, final_gain=None):
        mid = norm_swiglu(h, _row(norm_ffn[layer]), wview(w_gate_b, layer), wview(w_up_b, layer), *TILE_SWIGLU)
        if final_gain is not None:
            return matmul_residual_rmsnorm(mid, wview(w_down_b, layer), h, final_gain, *TILE_DOWN)
        return matmul_residual([mid], [wview(w_down_b, layer)], h, *TILE_DOWN)

    g0 = _row(norm_mix0[0])
    f0, f1 = 3 * fox_width, 3 * fox_width + fox_heads
    w_qkvg = jnp.concatenate([w_in0_t[0, :f0], w_in0_t[0, f1:]], axis=0)
    qkv, f_logit = norm_matmul(h, g0, wview(w_qkvg, transposed=True), pad_rows(w_in0_t[0, f0:f1], LANES),
                               *TILE_PROJ)

    tq = ATTN_BLOCK
    c = fox_gates(f_logit, _pad_cols(_row(fgate_bias0[0]), LANES), batch, seq)
    c_heads = jnp.transpose(c.reshape(batch, seq, LANES)[:, :, :fox_heads], (0, 2, 1))
    v_t = jnp.transpose(qkv[:, 2 * fox_width:f0].reshape(batch, seq // tq, tq, fox_width), (0, 1, 3, 2))
    a_out = fox_attention(qkv, v_t, c_heads.reshape(batch, fox_heads, 1, seq),
                          c_heads.reshape(batch, fox_heads, seq, 1), batch, seq, fox_heads, tq,
                          hb=min(ATTN_HEADS_PER_STEP, fox_heads))
    u = conformer_conv(qkv, f0, jnp.pad(conv_w0[0], ((0, CONV_HALO - CONV_WIDTH), (0, 0))), _row(conv_b0[0]),
                       _row(conv_ln_g0[0]), _row(conv_ln_b0[0]), batch, seq, ts=CONFORMER_ROWS)
    h = matmul_residual([a_out, u], [wview(w_out0, 0, row0=0, rows=fox_width),
                                     wview(w_out0, 0, row0=fox_width, rows=conv_ch)], h, *TILE_OUT_FOX)
    h = ffn(h, 0)

    g1 = _row(norm_mix1[0])
    z0 = gdn_qkv_width
    qkvz1, ba1 = gdn_in_proj(h, g1, wview(w_in1_t, 0, cols=z0 + gdn_v_width, transposed=True),
                             jnp.pad(short_conv1[0], ((0, SUBLANES - SHORT_CONV), (0, 0))),
                             pad_rows(w_in1_t[0, z0 + gdn_v_width:], LANES), seq, gdn_qk_width, *TILE_PROJ)
    alog_pad = jnp.pad(_row(a_log1[0]), ((0, 0), (gdn_v_heads, LANES - 2 * gdn_v_heads)))
    dtb_pad = jnp.pad(_row(dt_bias1[0]), ((0, 0), (gdn_v_heads, LANES - 2 * gdn_v_heads)))
    gates1 = gdn_gates(ba1, alog_pad, dtb_pad, gdn_v_heads, ts=GATES_ROWS)
    g_rows = jnp.transpose(gates1[:, gdn_v_heads:2 * gdn_v_heads].reshape(m // CHUNK, CHUNK, gdn_v_heads),
                           (0, 2, 1)).reshape(m // CHUNK, 1, gdn_v_heads * CHUNK)
    t_inv, p_intra = gdn_solve(qkvz1, gates1, g_rows, gdn_v_heads, rep, chunks_per_step=SOLVE_CHUNKS_PER_STEP)
    o1 = gdn_chunk(qkvz1, gates1, t_inv, p_intra, _row(o_norm1[0]), batch, seq, gdn_qk_heads, rep,
                   chunks_per_step=SCAN_CHUNKS_PER_STEP)
    h = matmul_residual([o1], [wview(w_out1.astype(BF16), 0)], h, *TILE_OUT_GDN)
    return ffn(h, 1, final_gain=_row(final_norm)).reshape(batch, seq, d_model)
```

```python
import functools
from typing import NamedTuple, Optional

import jax
import jax.numpy as jnp
from jax import lax
from jax.experimental import pallas as pl
from jax.experimental.pallas import tpu as pltpu

F32 = jnp.float32
BF16 = jnp.bfloat16

NORM_EPS = 1e-6
LANES = 128
SUBLANES = 8
MXU_TILE = 256
HEAD_DIM = 128
CHUNK = 64
CONV_WIDTH = 31
CONV_HALO = 32
SHORT_CONV = 4
SHORT_HALO = 8
VMEM_LIMIT_BYTES = 56 * 1024 * 1024

TILE_PROJ = (1024, 1024)
TILE_SWIGLU = (1024, 512)
TILE_DOWN = (512, 1024)
TILE_OUT_FOX = (1024, 1024)
TILE_OUT_GDN = (512, 1024)
ATTN_BLOCK = 256
ATTN_HEADS_PER_STEP = 8
CONFORMER_ROWS = 256
GATES_ROWS = 1024
SOLVE_CHUNKS_PER_STEP = 4
SCAN_CHUNKS_PER_STEP = 2


def _params(*sem):
    return pltpu.CompilerParams(dimension_semantics=sem, vmem_limit_bytes=VMEM_LIMIT_BYTES)


def _blk(dim, pref, align):
    best = dim
    for cand in range(align, min(dim, pref) + 1, align):
        if dim % cand == 0:
            best = cand
    return best


def _sigmoid(x):
    return 1.0 / (1.0 + jnp.exp(-x))


def _softplus(x):
    return jnp.maximum(x, 0.0) + jnp.log(1.0 + jnp.exp(-jnp.abs(x)))


def _rms_to_scratch(x_ref, g_ref, xn_ref):
    x = x_ref[...].astype(F32)
    ms = jnp.mean(x * x, axis=-1, keepdims=True)
    xn_ref[...] = (x * lax.rsqrt(ms + NORM_EPS) * g_ref[...]).astype(xn_ref.dtype)


class WeightView(NamedTuple):
    arr: jax.Array
    lead: Optional[int]
    row0: int
    rows: int
    col0: int
    cols: int
    transposed: bool


def wview(arr, lead=None, row0=0, rows=None, col0=0, cols=None, transposed=False):
    k, n = arr.shape[-2:][::-1] if transposed else arr.shape[-2:]
    return WeightView(arr, lead, row0, k - row0 if rows is None else rows, col0, n - col0 if cols is None else cols,
                      transposed)


def _w_spec(wv, bn, col_block):
    assert wv.row0 % wv.rows == 0 and wv.col0 % bn == 0 and wv.cols % bn == 0
    rb, cb = wv.row0 // wv.rows, wv.col0 // bn
    shape = (bn, wv.rows) if wv.transposed else (wv.rows, bn)
    index = (lambda *g: (cb + col_block(*g), rb)) if wv.transposed else (lambda *g: (rb, cb + col_block(*g)))
    if wv.arr.ndim == 3:
        return pl.BlockSpec((None,) + shape, lambda *g: (wv.lead,) + index(*g))
    return pl.BlockSpec(shape, index)


def _dot_w(x, w, transposed):
    dims = (((1,), (1,)), ((), ())) if transposed else (((1,), (0,)), ((), ()))
    return lax.dot_general(x, w.astype(BF16), dims, preferred_element_type=F32)


def _norm_mm_kernel(x_ref, g_ref, w_ref, wl_ref, o_ref, logit_ref, xn_ref, *, transposed):
    @pl.when(pl.program_id(1) == 0)
    def _():
        _rms_to_scratch(x_ref, g_ref, xn_ref)
        logit_ref[...] = _dot_w(xn_ref[...], wl_ref[...], True)

    o_ref[...] = _dot_w(xn_ref[...], w_ref[...], transposed).astype(o_ref.dtype)


def norm_matmul(x, gain, wv, w_logit_t, bm, bn):
    m, k = x.shape
    n = wv.cols
    bm, bn = _blk(m, bm, 8), _blk(n, bn, LANES)
    return pl.pallas_call(
        functools.partial(_norm_mm_kernel, transposed=wv.transposed),
        grid=(m // bm, n // bn),
        in_specs=[pl.BlockSpec((bm, k), lambda i, j: (i, 0)),
                  pl.BlockSpec((1, k), lambda i, j: (0, 0)),
                  _w_spec(wv, bn, lambda i, j: j),
                  pl.BlockSpec((LANES, k), lambda i, j: (0, 0))],
        out_specs=[pl.BlockSpec((bm, bn), lambda i, j: (i, j)), pl.BlockSpec((bm, LANES), lambda i, j: (i, 0))],
        out_shape=[jax.ShapeDtypeStruct((m, n), BF16), jax.ShapeDtypeStruct((m, LANES), F32)],
        scratch_shapes=[pltpu.VMEM((bm, k), BF16)],
        compiler_params=_params("parallel", "arbitrary"),
        name="norm_matmul",
    )(x, gain, wv.arr, w_logit_t)


def _norm_swiglu_kernel(x_ref, g_ref, wg_ref, wu_ref, o_ref, xn_ref):
    @pl.when(pl.program_id(1) == 0)
    def _():
        _rms_to_scratch(x_ref, g_ref, xn_ref)

    xn = xn_ref[...]
    gate = jnp.dot(xn, wg_ref[...].astype(BF16), preferred_element_type=F32)
    up = jnp.dot(xn, wu_ref[...].astype(BF16), preferred_element_type=F32)
    o_ref[...] = (gate * _sigmoid(gate) * up).astype(o_ref.dtype)


def norm_swiglu(x, gain, wg, wu, bm, bn):
    m, k = x.shape
    n = wg.cols
    bm, bn = _blk(m, bm, 8), _blk(n, bn, LANES)
    return pl.pallas_call(
        _norm_swiglu_kernel,
        grid=(m // bm, n // bn),
        in_specs=[pl.BlockSpec((bm, k), lambda i, j: (i, 0)),
                  pl.BlockSpec((1, k), lambda i, j: (0, 0)),
                  _w_spec(wg, bn, lambda i, j: j),
                  _w_spec(wu, bn, lambda i, j: j)],
        out_specs=pl.BlockSpec((bm, bn), lambda i, j: (i, j)),
        out_shape=jax.ShapeDtypeStruct((m, n), BF16),
        scratch_shapes=[pltpu.VMEM((bm, k), BF16)],
        compiler_params=_params("parallel", "arbitrary"),
        name="norm_swiglu",
    )(x, gain, wg.arr, wu.arr)


def _mm_res_kernel(*refs, n_pairs):
    xs = refs[:n_pairs]
    ws = refs[n_pairs:2 * n_pairs]
    res_ref = refs[2 * n_pairs]
    o_ref = refs[2 * n_pairs + 1]
    wb = refs[2 * n_pairs + 2:]

    @pl.when(pl.program_id(1) == 0)
    def _():
        for w_ref, wb_ref in zip([w for w in ws if w.dtype != BF16], wb):
            wb_ref[...] = w_ref[...].astype(BF16)

    wb = list(wb)
    acc = res_ref[...]
    for x_ref, w_ref in zip(xs, ws):
        w = w_ref[...] if w_ref.dtype == BF16 else wb.pop(0)[...]
        acc = acc + jnp.dot(x_ref[...], w, preferred_element_type=F32)
    o_ref[...] = acc


def matmul_residual(xs, wvs, res, bm, bn):
    m, n = res.shape
    bm, bn = _blk(m, bm, 8), _blk(n, bn, LANES)
    n_pairs = len(xs)
    in_specs = ([pl.BlockSpec((bm, x.shape[1]), lambda j, i: (i, 0)) for x in xs]
                + [_w_spec(wv, bn, lambda j, i: j) for wv in wvs]
                + [pl.BlockSpec((bm, bn), lambda j, i: (i, j))])
    return pl.pallas_call(
        functools.partial(_mm_res_kernel, n_pairs=n_pairs),
        grid=(n // bn, m // bm),
        in_specs=in_specs,
        out_specs=pl.BlockSpec((bm, bn), lambda j, i: (i, j)),
        out_shape=jax.ShapeDtypeStruct((m, n), F32),
        scratch_shapes=[pltpu.VMEM((wv.rows, bn), BF16) for wv in wvs if wv.arr.dtype != BF16],
        compiler_params=_params("parallel", "arbitrary"),
        name="matmul_residual",
    )(*xs, *[wv.arr for wv in wvs], res)


def _mm_res_norm_kernel(x_ref, w_ref, res_ref, g_ref, o_ref, row_ref):
    j = pl.program_id(1)
    nj, _, bn = row_ref.shape
    row_ref[j] = res_ref[...] + jnp.dot(x_ref[...], w_ref[...], preferred_element_type=F32)

    @pl.when(j == nj - 1)
    def _():
        parts = [row_ref[jb] for jb in range(nj)]
        ms = sum(jnp.sum(p * p, axis=-1, keepdims=True) for p in parts) / (nj * bn)
        inv = lax.rsqrt(ms + NORM_EPS)
        for jb in range(nj):
            o_ref[:, jb * bn:(jb + 1) * bn] = parts[jb] * inv * g_ref[:, jb * bn:(jb + 1) * bn]


def matmul_residual_rmsnorm(x, wv, res, gain, bm, bn):
    m, n = res.shape
    bm, bn = _blk(m, bm, 8), _blk(n, bn, LANES)
    assert wv.arr.dtype == BF16 and not wv.transposed
    return pl.pallas_call(
        _mm_res_norm_kernel,
        grid=(m // bm, n // bn),
        in_specs=[pl.BlockSpec((bm, x.shape[1]), lambda i, j: (i, 0)),
                  _w_spec(wv, bn, lambda i, j: j),
                  pl.BlockSpec((bm, bn), lambda i, j: (i, j)),
                  pl.BlockSpec((1, n), lambda i, j: (0, 0))],
        out_specs=pl.BlockSpec((bm, n), lambda i, j: (i, 0)),
        out_shape=jax.ShapeDtypeStruct((m, n), F32),
        scratch_shapes=[pltpu.VMEM((n // bn, bm, bn), F32)],
        compiler_params=_params("parallel", "arbitrary"),
        name="matmul_residual_rmsnorm",
    )(x, wv.arr, res, gain)


def _prefix_sum_rows(x, period):
    row = lax.broadcasted_iota(jnp.int32, x.shape, 0) % period
    shift = 1
    while shift < period:
        x = x + jnp.where(row >= shift, pltpu.roll(x, shift, axis=0), 0.0)
        shift *= 2
    return x


def _fox_gates_kernel(f_ref, b_ref, c_ref):
    z = f_ref[...] + b_ref[...]
    log_f = jnp.minimum(z, 0.0) - jnp.log(1.0 + jnp.exp(-jnp.abs(z)))
    c_ref[...] = _prefix_sum_rows(log_f, log_f.shape[0])


def fox_gates(f_logit, f_bias, batch, seq):
    return pl.pallas_call(
        _fox_gates_kernel,
        grid=(batch,),
        in_specs=[pl.BlockSpec((seq, LANES), lambda b: (b, 0)), pl.BlockSpec((1, LANES), lambda b: (0, 0))],
        out_specs=pl.BlockSpec((seq, LANES), lambda b: (b, 0)),
        out_shape=jax.ShapeDtypeStruct((batch * seq, LANES), F32),
        compiler_params=_params("parallel"),
        name="fox_gates",
    )(f_logit, f_bias)


def _fox_attn_kernel(q_ref, k_ref, vt_ref, cq_ref, ck_ref, o_ref, m_ref, l_ref, acc_ref, *, tq, hb, scale):
    i = pl.program_id(2)
    d = HEAD_DIM
    heads = range(hb)
    q = [q_ref[:, h * d:(h + 1) * d] for h in heads]
    cq = [cq_ref[0, h] for h in heads]
    for h in heads:
        m_ref[h] = jnp.full((1, tq), -jnp.inf, F32)
        l_ref[h] = jnp.zeros((1, tq), F32)
        acc_ref[h] = jnp.zeros((d, tq), F32)
    key_after_query = (lax.broadcasted_iota(jnp.int32, (tq, tq), 0) > lax.broadcasted_iota(jnp.int32, (tq, tq), 1))

    def block(j, on_diagonal):
        start = pl.multiple_of(j * tq, tq)
        s = [lax.dot_general(k_ref[pl.ds(start, tq), h * d:(h + 1) * d], q[h], (((1,), (1,)), ((), ())),
                             preferred_element_type=F32) for h in heads]
        ck = ck_ref[pl.ds(start, tq), :]
        x = [s[h] * scale + (cq[h] - ck[:, h:h + 1]) for h in heads]
        if on_diagonal:
            x = [jnp.where(key_after_query, -jnp.inf, x[h]) for h in heads]
        m_old = [m_ref[h] for h in heads]
        m_new = [jnp.maximum(m_old[h], jnp.max(x[h], axis=0, keepdims=True)) for h in heads]
        alpha = [jnp.exp(m_old[h] - m_new[h]) for h in heads]
        p = [jnp.exp(x[h] - m_new[h]) for h in heads]
        pv = [jnp.dot(vt_ref[0, j, h * d:(h + 1) * d, :], p[h].astype(BF16), preferred_element_type=F32)
              for h in heads]
        for h in heads:
            m_ref[h] = m_new[h]
            l_ref[h] = alpha[h] * l_ref[h] + jnp.sum(p[h], axis=0, keepdims=True)
            acc_ref[h] = alpha[h] * acc_ref[h] + pv[h]

    def body(j, carry):
        block(j, False)
        return carry

    lax.fori_loop(0, i, body, 0)
    block(i, True)
    for h in heads:
        o_ref[:, h * d:(h + 1) * d] = (acc_ref[h] / l_ref[h]).T.astype(o_ref.dtype)


def fox_attention(qkv, v_t, c_row, c, batch, seq, heads, tq, hb):
    nq = seq // tq
    d = HEAD_DIM
    groups = heads // hb
    assert groups == 1
    kernel = functools.partial(_fox_attn_kernel, tq=tq, hb=hb, scale=d ** -0.5)
    return pl.pallas_call(
        kernel,
        grid=(batch, groups, nq),
        in_specs=[pl.BlockSpec((tq, hb * d), lambda b, g, i: (b * nq + i, g)),
                  pl.BlockSpec((seq, hb * d), lambda b, g, i: (b, groups + g)),
                  pl.BlockSpec((1, nq, hb * d, tq), lambda b, g, i: (b, 0, g, 0)),
                  pl.BlockSpec((1, hb, 1, tq), lambda b, g, i: (b, g, 0, i)),
                  pl.BlockSpec((seq, LANES), lambda b, g, i: (b, 0))],
        out_specs=pl.BlockSpec((tq, hb * d), lambda b, g, i: (b * nq + i, g)),
        out_shape=jax.ShapeDtypeStruct((batch * seq, heads * d), BF16),
        scratch_shapes=[pltpu.VMEM((hb, 1, tq), F32), pltpu.VMEM((hb, 1, tq), F32), pltpu.VMEM((hb, d, tq), F32)],
        compiler_params=_params("parallel", "parallel", "arbitrary"),
        name="fox_attention",
    )(qkv, qkv, v_t, c_row, c)


def _conformer_kernel(val_ref, gate_ref, w_ref, cb_ref, lg_ref, lb_ref, o_ref, hist_ref, shift_ref, y_ref,
                      *, ts, rows):
    channels = val_ref.shape[1]

    @pl.when(pl.program_id(1) == 0)
    def _():
        hist_ref[0:CONV_HALO, :] = jnp.zeros((CONV_HALO, channels), F32)

    hist_ref[CONV_HALO:CONV_HALO + ts, :] = val_ref[...].astype(F32) * _sigmoid(gate_ref[...].astype(F32))

    base = CONV_HALO - (CONV_WIDTH - 1)
    hist_rows = ts + CONV_HALO
    for cb in range(channels // LANES):
        lanes = slice(cb * LANES, (cb + 1) * LANES)
        col = hist_ref[:, lanes]
        for b in range(1, SUBLANES):
            shift_ref[b - 1, :, lanes] = pltpu.roll(col, hist_rows - b, axis=0)
        for rb in range(ts // rows):
            acc = jnp.zeros((rows, LANES), F32)
            for j in range(CONV_WIDTH):
                a, b = divmod(base + j, SUBLANES)
                r0 = rb * rows + a * SUBLANES
                src = hist_ref[r0:r0 + rows, lanes] if b == 0 else shift_ref[b - 1, r0:r0 + rows, lanes]
                acc = acc + w_ref[j:j + 1, lanes] * src
            y_ref[rb * rows:(rb + 1) * rows, lanes] = acc + cb_ref[:, lanes]

    hist_ref[0:CONV_HALO, :] = hist_ref[ts:ts + CONV_HALO, :]

    y = y_ref[...]
    mu = jnp.mean(y, axis=-1, keepdims=True)
    yc = y - mu
    var = jnp.mean(yc * yc, axis=-1, keepdims=True)
    yn = yc * lax.rsqrt(var + NORM_EPS) * lg_ref[...] + lb_ref[...]
    o_ref[...] = (yn * _sigmoid(yn)).astype(o_ref.dtype)


def conformer_conv(glu, glu_col0, conv_w, conv_b, ln_g, ln_b, batch, seq, ts):
    channels = conv_w.shape[1]
    assert glu_col0 % channels == 0
    cb = glu_col0 // channels
    ns = seq // ts
    kernel = functools.partial(_conformer_kernel, ts=ts, rows=64)
    vec = lambda: pl.BlockSpec((1, channels), lambda b, i: (0, 0))
    return pl.pallas_call(
        kernel,
        grid=(batch, ns),
        in_specs=[pl.BlockSpec((ts, channels), lambda b, i: (b * ns + i, cb)),
                  pl.BlockSpec((ts, channels), lambda b, i: (b * ns + i, cb + 1)),
                  pl.BlockSpec((conv_w.shape[0], channels), lambda b, i: (0, 0)),
                  vec(), vec(), vec()],
        out_specs=pl.BlockSpec((ts, channels), lambda b, i: (b * ns + i, 0)),
        out_shape=jax.ShapeDtypeStruct((batch * seq, channels), BF16),
        scratch_shapes=[pltpu.VMEM((ts + CONV_HALO, channels), F32),
                        pltpu.VMEM((SUBLANES - 1, ts + CONV_HALO, channels), F32),
                        pltpu.VMEM((ts, channels), F32)],
        compiler_params=_params("arbitrary", "arbitrary"),
        name="conformer_conv",
    )(glu, glu, conv_w, conv_b, ln_g, ln_b)


def _gdn_in_kernel(x_ref, g_ref, w_ref, cw_ref, wl_ref, o_ref, logit_ref, xn_ref, hist_ref, halo_ref,
                   *, bm, blocks_per_seq, qk_blocks, conv_blocks):
    i = pl.program_id(0)
    j = pl.program_id(1)

    @pl.when(jnp.logical_and(i == 0, j == 0))
    def _():
        halo_ref[...] = jnp.zeros(halo_ref.shape, F32)

    @pl.when(j == 0)
    def _():
        _rms_to_scratch(x_ref, g_ref, xn_ref)
        logit_ref[...] = _dot_w(xn_ref[...], wl_ref[...], True)

    @pl.when(j >= conv_blocks)
    def _():
        o_ref[...] = _dot_w(xn_ref[...], w_ref[...], True).astype(o_ref.dtype)

    conv_args = dict(bm=bm, blocks_per_seq=blocks_per_seq, qk_blocks=qk_blocks)

    @pl.when(j < 2 * qk_blocks)
    def _():
        _gdn_conv_block(i, j, w_ref, cw_ref, o_ref, xn_ref, hist_ref, halo_ref, normalize=True, **conv_args)

    @pl.when(jnp.logical_and(j >= 2 * qk_blocks, j < conv_blocks))
    def _():
        _gdn_conv_block(i, j, w_ref, cw_ref, o_ref, xn_ref, hist_ref, halo_ref, normalize=False, **conv_args)


def _gdn_conv_block(i, j, w_ref, cw_ref, o_ref, xn_ref, hist_ref, halo_ref, *, bm, blocks_per_seq, qk_blocks,
                    normalize):
    bn = o_ref.shape[1]
    starts_sequence = (i % blocks_per_seq) == 0
    hist_ref[0:SHORT_HALO, :] = jnp.where(starts_sequence, 0.0, halo_ref[j])

    out_scale = jnp.where(j < qk_blocks, HEAD_DIM ** -0.5, 1.0)
    base = SHORT_HALO - (SHORT_CONV - 1)
    conv_rows = _blk(bm, 128, SUBLANES)

    def project(s):
        cols = slice(s * MXU_TILE, (s + 1) * MXU_TILE)
        hist_ref[SHORT_HALO:SHORT_HALO + bm, cols] = _dot_w(xn_ref[...], w_ref[cols, :], True)

    def conv_slab(s):
        for hb in range(s * MXU_TILE // HEAD_DIM, (s + 1) * MXU_TILE // HEAD_DIM):
            lanes = slice(hb * HEAD_DIM, (hb + 1) * HEAD_DIM)
            for r0 in range(0, bm, conv_rows):
                col = hist_ref[r0:r0 + conv_rows + SHORT_HALO, lanes]
                acc = jnp.zeros((conv_rows, HEAD_DIM), F32)
                for tap in range(SHORT_CONV):
                    ofs = base + tap
                    src = (col[ofs:ofs + conv_rows] if ofs % SUBLANES == 0
                           else pltpu.roll(col, conv_rows + SHORT_HALO - ofs, axis=0)[0:conv_rows])
                    acc = acc + cw_ref[tap:tap + 1, lanes] * src
                y = acc * _sigmoid(acc)
                if normalize:
                    y = y * (lax.rsqrt(jnp.sum(y * y, axis=-1, keepdims=True) + NORM_EPS) * out_scale)
                o_ref[r0:r0 + conv_rows, lanes] = y.astype(o_ref.dtype)

    n_slabs = bn // MXU_TILE
    project(0)
    for s in range(n_slabs):
        if s + 1 < n_slabs:
            project(s + 1)
        conv_slab(s)
    halo_ref[j] = hist_ref[bm:bm + SHORT_HALO, :]


def gdn_in_proj(x, gain, wv, short_conv, w_logit_t, seq, qk_width, bm, bn):
    m, k = x.shape
    n = wv.cols
    conv_width = short_conv.shape[1]
    bm, bn = _blk(seq, bm, 8), _blk(qk_width, bn, LANES)
    assert wv.transposed and seq % bm == 0 and n % bn == 0 and conv_width % bn == 0
    conv_blocks = conv_width // bn
    kernel = functools.partial(_gdn_in_kernel, bm=bm, blocks_per_seq=seq // bm, qk_blocks=qk_width // bn,
                               conv_blocks=conv_blocks)
    return pl.pallas_call(
        kernel,
        grid=(m // bm, n // bn),
        in_specs=[pl.BlockSpec((bm, k), lambda i, j: (i, 0)),
                  pl.BlockSpec((1, k), lambda i, j: (0, 0)),
                  _w_spec(wv, bn, lambda i, j: j),
                  pl.BlockSpec((short_conv.shape[0], bn), lambda i, j: (0, jnp.minimum(j, conv_blocks - 1))),
                  pl.BlockSpec((LANES, k), lambda i, j: (0, 0))],
        out_specs=[pl.BlockSpec((bm, bn), lambda i, j: (i, j)), pl.BlockSpec((bm, LANES), lambda i, j: (i, 0))],
        out_shape=[jax.ShapeDtypeStruct((m, n), BF16), jax.ShapeDtypeStruct((m, LANES), F32)],
        scratch_shapes=[pltpu.VMEM((bm, k), BF16), pltpu.VMEM((bm + SHORT_HALO, bn), F32),
                        pltpu.VMEM((conv_blocks, SHORT_HALO, bn), F32)],
        compiler_params=_params("arbitrary", "arbitrary"),
        name="gdn_in_proj",
    )(x, gain, wv.arr, short_conv, w_logit_t)


def _gdn_gates_kernel(ba_ref, alog_ref, dtb_ref, g_ref, *, gate_heads):
    ba = ba_ref[...]
    beta = _sigmoid(ba)
    g = -jnp.exp(alog_ref[...]) * _softplus(ba + dtb_ref[...])
    gcum = _prefix_sum_rows(g, CHUNK)
    lane = lax.broadcasted_iota(jnp.int32, ba.shape, 1)
    g_ref[...] = jnp.where(lane < gate_heads, beta, gcum)


def gdn_gates(ba, alog_pad, dtb_pad, gate_heads, ts):
    m = ba.shape[0]
    ts = _blk(m, ts, CHUNK)
    row = lambda i: (i, 0)
    const = lambda i: (0, 0)
    return pl.pallas_call(
        functools.partial(_gdn_gates_kernel, gate_heads=gate_heads),
        grid=(m // ts,),
        in_specs=[pl.BlockSpec((ts, LANES), row), pl.BlockSpec((1, LANES), const), pl.BlockSpec((1, LANES), const)],
        out_specs=pl.BlockSpec((ts, LANES), row),
        out_shape=jax.ShapeDtypeStruct((m, LANES), F32),
        compiler_params=_params("parallel"),
        name="gdn_gates",
    )(ba, alog_pad, dtb_pad)


GROUP = MXU_TILE // CHUNK


def _gdn_solve_kernel(q_ref, k_ref, g_ref, gr_ref, bdm_ref, kkm_ref, t_ref, p_ref, *, v_heads, rep):
    c = CHUNK
    gw = MXU_TILE
    n_groups = v_heads // GROUP
    n_chunks = g_ref.shape[0] // c
    row = lax.broadcasted_iota(jnp.int32, (c, gw), 0)
    lane = lax.broadcasted_iota(jnp.int32, (c, gw), 1)
    blk = lane // c
    col = lane % c
    tril = row >= col
    strict = row > col
    eye = (row == col).astype(F32)
    bd_mask = bdm_ref[...]
    kk_mask = kkm_ref[...]

    def block_diag(x):
        xb = x.astype(BF16)
        return jnp.concatenate([xb] * GROUP, axis=0) * bd_mask

    gates = [g_ref[n * c:(n + 1) * c, :] for n in range(n_chunks)]

    def per_head_columns(first_lane, u):
        n, g = units[u]
        out = gates[n][:, first_lane + GROUP * g + GROUP - 1:first_lane + GROUP * g + GROUP]
        for mth in range(GROUP - 2, -1, -1):
            cm = gates[n][:, first_lane + GROUP * g + mth:first_lane + GROUP * g + mth + 1]
            out = jnp.where(blk == mth, cm, out)
        return out

    units = [(n, g) for n in range(n_chunks) for g in range(n_groups)]
    groups = range(len(units))
    rows = [slice(n * c, (n + 1) * c) for n, _ in units]
    lanes = [slice(g * gw, (g + 1) * gw) for _, g in units]
    kq = [jnp.concatenate([k_ref[rows[u], lanes[u]], q_ref[rows[u], lanes[u]]], axis=0) for u in groups]
    rhs = [jnp.concatenate([k_ref[rows[u], lanes[u]]] * GROUP, axis=0) * kk_mask for u in groups]
    kkqk = [lax.dot_general(kq[g], rhs[g], (((1,), (1,)), ((), ())), preferred_element_type=F32) for g in groups]
    decay = [jnp.exp(jnp.where(tril, per_head_columns(v_heads, u) - gr_ref[units[u][0], :, lanes[u]], -jnp.inf))
             for u in groups]
    low = [jnp.where(strict, per_head_columns(0, g) * kkqk[g][:c] * decay[g], 0.0) for g in groups]
    for g in groups:
        p_ref[rows[g], lanes[g]] = (kkqk[g][c:] * decay[g]).astype(p_ref.dtype)

    inv = [eye - low[g] for g in groups]
    bd = [block_diag(low[g]) for g in groups]
    power = [jnp.dot(low[g].astype(BF16), bd[g], preferred_element_type=F32) for g in groups]
    span = 2
    while span <= c // 2:
        bd = [block_diag(power[g]) for g in groups]
        if span < c // 2:
            both = [jnp.dot(jnp.concatenate([power[g], inv[g]], axis=0).astype(BF16), bd[g],
                            preferred_element_type=F32) for g in groups]
            power = [both[g][:c] for g in groups]
            inv = [inv[g] + both[g][c:] for g in groups]
        else:
            inv = [inv[g] + jnp.dot(inv[g].astype(BF16), bd[g], preferred_element_type=F32) for g in groups]
        span *= 2
    for g in groups:
        t_ref[rows[g], lanes[g]] = inv[g].astype(t_ref.dtype)


def gdn_solve(qkv, gates, g_rows, v_heads, rep, chunks_per_step):
    m = qkv.shape[0]
    width = v_heads * CHUNK
    qk_width = v_heads * HEAD_DIM // rep
    assert GROUP * CHUNK == MXU_TILE and rep * HEAD_DIM == MXU_TILE and v_heads % GROUP == 0
    assert qk_width * rep == v_heads * HEAD_DIM and qk_width == width
    rb = lax.broadcasted_iota(jnp.int32, (GROUP * CHUNK, MXU_TILE), 0) // CHUNK
    ln = lax.broadcasted_iota(jnp.int32, (GROUP * CHUNK, MXU_TILE), 1)
    bd_mask = (rb == ln // CHUNK).astype(BF16)
    kk_mask = (rb // rep == ln // HEAD_DIM).astype(BF16)
    kernel = functools.partial(_gdn_solve_kernel, v_heads=v_heads, rep=rep)
    row = lambda i: (i, 0)
    const = lambda i: (0, 0)
    rows = chunks_per_step * CHUNK
    assert m % rows == 0
    return pl.pallas_call(
        kernel,
        grid=(m // rows,),
        in_specs=[pl.BlockSpec((rows, qk_width), row), pl.BlockSpec((rows, qk_width), lambda i: (i, 1)),
                  pl.BlockSpec((rows, LANES), row),
                  pl.BlockSpec((chunks_per_step, 1, width), lambda i: (i, 0, 0)),
                  pl.BlockSpec((GROUP * CHUNK, MXU_TILE), const), pl.BlockSpec((GROUP * CHUNK, MXU_TILE), const)],
        out_specs=[pl.BlockSpec((rows, width), row), pl.BlockSpec((rows, width), row)],
        out_shape=[jax.ShapeDtypeStruct((m, width), BF16), jax.ShapeDtypeStruct((m, width), BF16)],
        compiler_params=_params("parallel"),
        name="gdn_solve",
    )(qkv, qkv, gates, g_rows, bd_mask, kk_mask)


def _gdn_chunk_kernel(q_ref, k_ref, v_ref, z_ref, g_ref, t_ref, p_ref, on_ref, o_ref, state_ref, *, qk_heads, rep):
    c = CHUNK
    d = HEAD_DIM

    @pl.when(pl.program_id(1) == 0)
    def _():
        state_ref[...] = jnp.zeros(state_ref.shape, F32)

    n_v = qk_heads * rep
    pw = rep * d
    tw = rep * c
    gain = on_ref[...]
    zeros = jnp.zeros((c, d), BF16)
    pairs = range(qk_heads)

    def on_diagonal(parts):
        return jnp.concatenate(
            [jnp.concatenate([parts[r] if s == r else zeros for s in range(rep)], axis=1) for r in range(rep)],
            axis=0)

    def chunk_step(n, state):
        rows = slice(n * c, (n + 1) * c)
        gates = g_ref[rows, :]

        def head_cols(first_lane, p):
            return [gates[:, first_lane + p * rep + r:first_lane + p * rep + r + 1] for r in range(rep)]

        k = {p: k_ref[rows, p * d:(p + 1) * d] for p in pairs}
        kq = {p: jnp.concatenate([k[p], q_ref[rows, p * d:(p + 1) * d]], axis=0) for p in pairs}
        kqs = {p: jnp.dot(kq[p], state[p].astype(BF16), preferred_element_type=F32) for p in pairs}

        beta = {p: head_cols(0, p) for p in pairs}
        gc = {p: head_cols(n_v, p) for p in pairs}
        eg = {p: [jnp.exp(g) for g in gc[p]] for p in pairs}
        rhs = {p: on_diagonal([(beta[p][r] * (v_ref[rows, p * pw + r * d:p * pw + (r + 1) * d].astype(F32)
                                           - kqs[p][:c, r * d:(r + 1) * d] * eg[p][r])).astype(BF16)
                               for r in range(rep)]) for p in pairs}
        v_new = {p: jnp.dot(t_ref[rows, p * tw:(p + 1) * tw], rhs[p], preferred_element_type=F32)
                 for p in pairs}
        vn_diag = {p: on_diagonal([v_new[p][:, r * d:(r + 1) * d].astype(BF16) for r in range(rep)]) for p in pairs}
        intra = {p: jnp.dot(p_ref[rows, p * tw:(p + 1) * tw], vn_diag[p], preferred_element_type=F32)
                 for p in pairs}

        g_last = {p: [g[c - 1:c, :] for g in gc[p]] for p in pairs}
        v_dec = {p: jnp.concatenate([jnp.exp(g_last[p][r] - gc[p][r]) * v_new[p][:, r * d:(r + 1) * d]
                                     for r in range(rep)], axis=1).astype(BF16) for p in pairs}
        kt = {p: k[p].astype(F32).T.astype(BF16) for p in pairs}
        upd = {p: jnp.dot(kt[p], v_dec[p], preferred_element_type=F32) for p in pairs}
        scale = {p: jnp.concatenate([jnp.broadcast_to(jnp.exp(g_last[p][r]), (1, d)) for r in range(rep)], axis=1)
                 for p in pairs}
        new_state = {p: state[p] * scale[p] + upd[p] for p in pairs}

        for p in pairs:
            for r in range(rep):
                vl = slice(p * pw + r * d, p * pw + (r + 1) * d)
                o = kqs[p][c:, r * d:(r + 1) * d] * eg[p][r] + intra[p][:, r * d:(r + 1) * d]
                on = o * lax.rsqrt(jnp.mean(o * o, axis=-1, keepdims=True) + NORM_EPS) * gain
                zf = z_ref[rows, vl].astype(F32)
                o_ref[rows, vl] = (on * (zf * _sigmoid(zf))).astype(o_ref.dtype)
        return new_state

    state = {p: state_ref[p] for p in pairs}
    for n in range(g_ref.shape[0] // c):
        state = chunk_step(n, state)
    for p in pairs:
        state_ref[p] = state[p]


def gdn_chunk(qkvz, gates, t_inv, p_intra, o_norm, batch, seq, qk_heads, rep, chunks_per_step):
    m = batch * seq
    qk_width = qk_heads * HEAD_DIM
    v_width = qk_width * rep
    assert v_width == 2 * qk_width
    kernel = functools.partial(_gdn_chunk_kernel, qk_heads=qk_heads, rep=rep)
    rows = chunks_per_step * CHUNK
    assert seq % rows == 0
    nc = seq // rows
    row = lambda b, i: (b * nc + i, 0)
    col1 = lambda b, i: (b * nc + i, 1)
    col2 = lambda b, i: (b * nc + i, 2)
    return pl.pallas_call(
        kernel,
        grid=(batch, nc),
        in_specs=[pl.BlockSpec((rows, qk_width), row), pl.BlockSpec((rows, qk_width), col1),
                  pl.BlockSpec((rows, v_width), col1), pl.BlockSpec((rows, v_width), col2),
                  pl.BlockSpec((rows, LANES), row),
                  pl.BlockSpec((rows, t_inv.shape[1]), row), pl.BlockSpec((rows, p_intra.shape[1]), row),
                  pl.BlockSpec((1, HEAD_DIM), lambda b, i: (0, 0))],
        out_specs=pl.BlockSpec((rows, v_width), row),
        out_shape=jax.ShapeDtypeStruct((m, v_width), BF16),
        scratch_shapes=[pltpu.VMEM((qk_heads, HEAD_DIM, rep * HEAD_DIM), F32)],
        compiler_params=_params("arbitrary", "arbitrary"),
        name="gdn_chunk",
    )(qkvz, qkvz, qkvz, qkvz, gates, t_inv, p_intra, o_norm)


def _pad_cols(w, n):
    return jnp.pad(w, ((0, 0), (0, n - w.shape[1])))


def _row(v):
    return v.reshape(1, -1).astype(F32)


def kernel(x, norm_mix0, w_in0, fgate_bias0, conv_w0, conv_b0, conv_ln_g0, conv_ln_b0, w_out0, norm_mix1, w_in1,
           short_conv1, a_log1, dt_bias1, o_norm1, w_out1, norm_ffn, w_gate, w_up, w_down, final_norm):
    batch, seq, d_model = x.shape
    m = batch * seq
    fox_heads = fgate_bias0.shape[1]
    fox_width = fox_heads * HEAD_DIM
    conv_ch = conv_w0.shape[2]
    gdn_v_heads = a_log1.shape[1]
    gdn_v_width = gdn_v_heads * HEAD_DIM
    gdn_qkv_width = short_conv1.shape[2]
    gdn_qk_width = (gdn_qkv_width - gdn_v_width) // 2
    gdn_qk_heads = gdn_qk_width // HEAD_DIM
    rep = gdn_v_heads // gdn_qk_heads

    h = x.reshape(m, d_model)

    w_gate_b, w_up_b, w_down_b = w_gate, w_up, w_down.astype(BF16)
    w_in0_t = jnp.swapaxes(w_in0, 1, 2).astype(BF16)
    w_in1_t = jnp.swapaxes(w_in1, 1, 2).astype(BF16)

    def pad_rows(w, n):
        return jnp.pad(w, ((0, n - w.shape[0]), (0, 0)))

    def ffn(h, layer, final_gain=None):
        mid = norm_swiglu(h, _row(norm_ffn[layer]), wview(w_gate_b, layer), wview(w_up_b, layer), *TILE_SWIGLU)
        if final_gain is not None:
            return matmul_residual_rmsnorm(mid, wview(w_down_b, layer), h, final_gain, *TILE_DOWN)
        return matmul_residual([mid], [wview(w_down_b, layer)], h, *TILE_DOWN)

    g0 = _row(norm_mix0[0])
    f0, f1 = 3 * fox_width, 3 * fox_width + fox_heads
    w_qkvg = jnp.concatenate([w_in0_t[0, :f0], w_in0_t[0, f1:]], axis=0)
    qkv, f_logit = norm_matmul(h, g0, wview(w_qkvg, transposed=True), pad_rows(w_in0_t[0, f0:f1], LANES),
                               *TILE_PROJ)

    tq = ATTN_BLOCK
    c = fox_gates(f_logit, _pad_cols(_row(fgate_bias0[0]), LANES), batch, seq)
    c_heads = jnp.transpose(c.reshape(batch, seq, LANES)[:, :, :fox_heads], (0, 2, 1))
    v_t = jnp.transpose(qkv[:, 2 * fox_width:f0].reshape(batch, seq // tq, tq, fox_width), (0, 1, 3, 2))
    a_out = fox_attention(qkv, v_t, c_heads.reshape(batch, fox_heads, 1, seq), c, batch, seq, fox_heads, tq,
                          hb=min(ATTN_HEADS_PER_STEP, fox_heads))
    u = conformer_conv(qkv, f0, jnp.pad(conv_w0[0], ((0, CONV_HALO - CONV_WIDTH), (0, 0))), _row(conv_b0[0]),
                       _row(conv_ln_g0[0]), _row(conv_ln_b0[0]), batch, seq, ts=CONFORMER_ROWS)
    h = matmul_residual([a_out, u], [wview(w_out0, 0, row0=0, rows=fox_width),
                                     wview(w_out0, 0, row0=fox_width, rows=conv_ch)], h, *TILE_OUT_FOX)
    h = ffn(h, 0)

    g1 = _row(norm_mix1[0])
    z0 = gdn_qkv_width
    qkvz1, ba1 = gdn_in_proj(h, g1, wview(w_in1_t, 0, cols=z0 + gdn_v_width, transposed=True),
                             jnp.pad(short_conv1[0], ((0, SUBLANES - SHORT_CONV), (0, 0))),
                             pad_rows(w_in1_t[0, z0 + gdn_v_width:], LANES), seq, gdn_qk_width, *TILE_PROJ)
    alog_pad = jnp.pad(_row(a_log1[0]), ((0, 0), (gdn_v_heads, LANES - 2 * gdn_v_heads)))
    dtb_pad = jnp.pad(_row(dt_bias1[0]), ((0, 0), (gdn_v_heads, LANES - 2 * gdn_v_heads)))
    gates1 = gdn_gates(ba1, alog_pad, dtb_pad, gdn_v_heads, ts=GATES_ROWS)
    g_rows = jnp.transpose(gates1[:, gdn_v_heads:2 * gdn_v_heads].reshape(m // CHUNK, CHUNK, gdn_v_heads),
                           (0, 2, 1)).reshape(m // CHUNK, 1, gdn_v_heads * CHUNK)
    t_inv, p_intra = gdn_solve(qkvz1, gates1, g_rows, gdn_v_heads, rep, chunks_per_step=SOLVE_CHUNKS_PER_STEP)
    o1 = gdn_chunk(qkvz1, gates1, t_inv, p_intra, _row(o_norm1[0]), batch, seq, gdn_qk_heads, rep,
                   chunks_per_step=SCAN_CHUNKS_PER_STEP)
    h = matmul_residual([o1], [wview(w_out1.astype(BF16), 0)], h, *TILE_OUT_GDN)
    return ffn(h, 1, final_gain=_row(final_norm)).reshape(batch, seq, d_model)
```

```python
import functools
from typing import NamedTuple, Optional

import jax
import jax.numpy as jnp
from jax import lax
from jax.experimental import pallas as pl
from jax.experimental.pallas import tpu as pltpu

F32 = jnp.float32
BF16 = jnp.bfloat16

NORM_EPS = 1e-6
LANES = 128
SUBLANES = 8
MXU_TILE = 256
HEAD_DIM = 128
CHUNK = 64
CONV_WIDTH = 31
CONV_HALO = 32
SHORT_CONV = 4
SHORT_HALO = 8
VMEM_LIMIT_BYTES = 56 * 1024 * 1024

TILE_PROJ = (1024, 1024)
TILE_SWIGLU = (1024, 512)
TILE_DOWN = (1024, 1024)
TILE_DOWN_FINAL = (512, 1024)
TILE_OUT_FOX = (1024, 1024)
TILE_OUT_GDN = (512, 1024)
ATTN_BLOCK = 256
ATTN_HEADS_PER_STEP = 8
CONFORMER_ROWS = 256
GATES_ROWS = 1024
SOLVE_CHUNKS_PER_STEP = 4
SCAN_CHUNKS_PER_STEP = 2


def _params(*sem):
    return pltpu.CompilerParams(dimension_semantics=sem, vmem_limit_bytes=VMEM_LIMIT_BYTES)


def _blk(dim, pref, align):
    best = dim
    for cand in range(align, min(dim, pref) + 1, align):
        if dim % cand == 0:
            best = cand
    return best


def _sigmoid(x):
    return 1.0 / (1.0 + jnp.exp(-x))


def _softplus(x):
    return jnp.maximum(x, 0.0) + jnp.log(1.0 + jnp.exp(-jnp.abs(x)))


def _rms_to_scratch(x_ref, g_ref, xn_ref):
    x = x_ref[...].astype(F32)
    ms = jnp.mean(x * x, axis=-1, keepdims=True)
    xn_ref[...] = (x * lax.rsqrt(ms + NORM_EPS) * g_ref[...]).astype(xn_ref.dtype)


class WeightView(NamedTuple):
    arr: jax.Array
    lead: Optional[int]
    row0: int
    rows: int
    col0: int
    cols: int
    transposed: bool


def wview(arr, lead=None, row0=0, rows=None, col0=0, cols=None, transposed=False):
    k, n = arr.shape[-2:][::-1] if transposed else arr.shape[-2:]
    return WeightView(arr, lead, row0, k - row0 if rows is None else rows, col0, n - col0 if cols is None else cols,
                      transposed)


def _w_spec(wv, bn, col_block, buffers=None):
    assert wv.row0 % wv.rows == 0 and wv.col0 % bn == 0 and wv.cols % bn == 0
    rb, cb = wv.row0 // wv.rows, wv.col0 // bn
    shape = (bn, wv.rows) if wv.transposed else (wv.rows, bn)
    index = (lambda *g: (cb + col_block(*g), rb)) if wv.transposed else (lambda *g: (rb, cb + col_block(*g)))
    mode = None if buffers is None else pl.Buffered(buffers)
    if wv.arr.ndim == 3:
        return pl.BlockSpec((None,) + shape, lambda *g: (wv.lead,) + index(*g), pipeline_mode=mode)
    return pl.BlockSpec(shape, index, pipeline_mode=mode)


def _dot_w(x, w, transposed):
    dims = (((1,), (1,)), ((), ())) if transposed else (((1,), (0,)), ((), ()))
    return lax.dot_general(x, w.astype(BF16), dims, preferred_element_type=F32)


def _norm_mm_kernel(x_ref, g_ref, w_ref, wl_ref, o_ref, logit_ref, xn_ref, *, transposed):
    @pl.when(pl.program_id(1) == 0)
    def _():
        _rms_to_scratch(x_ref, g_ref, xn_ref)
        logit_ref[...] = _dot_w(xn_ref[...], wl_ref[...], True)

    o_ref[...] = _dot_w(xn_ref[...], w_ref[...], transposed).astype(o_ref.dtype)


def norm_matmul(x, gain, wv, w_logit_t, bm, bn):
    m, k = x.shape
    n = wv.cols
    bm, bn = _blk(m, bm, 8), _blk(n, bn, LANES)
    return pl.pallas_call(
        functools.partial(_norm_mm_kernel, transposed=wv.transposed),
        grid=(m // bm, n // bn),
        in_specs=[pl.BlockSpec((bm, k), lambda i, j: (i, 0)),
                  pl.BlockSpec((1, k), lambda i, j: (0, 0)),
                  _w_spec(wv, bn, lambda i, j: j),
                  pl.BlockSpec((LANES, k), lambda i, j: (0, 0))],
        out_specs=[pl.BlockSpec((bm, bn), lambda i, j: (i, j)), pl.BlockSpec((bm, LANES), lambda i, j: (i, 0))],
        out_shape=[jax.ShapeDtypeStruct((m, n), BF16), jax.ShapeDtypeStruct((m, LANES), F32)],
        scratch_shapes=[pltpu.VMEM((bm, k), BF16)],
        compiler_params=_params("parallel", "arbitrary"),
        name="norm_matmul",
    )(x, gain, wv.arr, w_logit_t)


def _norm_swiglu_kernel(x_ref, g_ref, wg_ref, wu_ref, o_ref, xn_ref):
    @pl.when(pl.program_id(1) == 0)
    def _():
        _rms_to_scratch(x_ref, g_ref, xn_ref)

    xn = xn_ref[...]
    gate = jnp.dot(xn, wg_ref[...].astype(BF16), preferred_element_type=F32)
    up = jnp.dot(xn, wu_ref[...].astype(BF16), preferred_element_type=F32)
    o_ref[...] = (gate * _sigmoid(gate) * up).astype(o_ref.dtype)


def norm_swiglu(x, gain, wg, wu, bm, bn):
    m, k = x.shape
    n = wg.cols
    bm, bn = _blk(m, bm, 8), _blk(n, bn, LANES)
    return pl.pallas_call(
        _norm_swiglu_kernel,
        grid=(m // bm, n // bn),
        in_specs=[pl.BlockSpec((bm, k), lambda i, j: (i, 0)),
                  pl.BlockSpec((1, k), lambda i, j: (0, 0)),
                  _w_spec(wg, bn, lambda i, j: j),
                  _w_spec(wu, bn, lambda i, j: j)],
        out_specs=pl.BlockSpec((bm, bn), lambda i, j: (i, j)),
        out_shape=jax.ShapeDtypeStruct((m, n), BF16),
        scratch_shapes=[pltpu.VMEM((bm, k), BF16)],
        compiler_params=_params("parallel", "arbitrary"),
        name="norm_swiglu",
    )(x, gain, wg.arr, wu.arr)


def _mm_res_kernel(*refs, n_pairs):
    xs = refs[:n_pairs]
    ws = refs[n_pairs:2 * n_pairs]
    res_ref = refs[2 * n_pairs]
    o_ref = refs[2 * n_pairs + 1]
    wb = refs[2 * n_pairs + 2:]

    @pl.when(pl.program_id(1) == 0)
    def _():
        for w_ref, wb_ref in zip([w for w in ws if w.dtype != BF16], wb):
            wb_ref[...] = w_ref[...].astype(BF16)

    wb = list(wb)
    acc = res_ref[...]
    for x_ref, w_ref in zip(xs, ws):
        w = w_ref[...] if w_ref.dtype == BF16 else wb.pop(0)[...]
        acc = acc + jnp.dot(x_ref[...], w, preferred_element_type=F32)
    o_ref[...] = acc


def matmul_residual(xs, wvs, res, bm, bn):
    m, n = res.shape
    bm, bn = _blk(m, bm, 8), _blk(n, bn, LANES)
    n_pairs = len(xs)
    in_specs = ([pl.BlockSpec((bm, x.shape[1]), lambda j, i: (i, 0)) for x in xs]
                + [_w_spec(wv, bn, lambda j, i: j, buffers=1) for wv in wvs]
                + [pl.BlockSpec((bm, bn), lambda j, i: (i, j))])
    return pl.pallas_call(
        functools.partial(_mm_res_kernel, n_pairs=n_pairs),
        grid=(n // bn, m // bm),
        in_specs=in_specs,
        out_specs=pl.BlockSpec((bm, bn), lambda j, i: (i, j)),
        out_shape=jax.ShapeDtypeStruct((m, n), F32),
        scratch_shapes=[pltpu.VMEM((wv.rows, bn), BF16) for wv in wvs if wv.arr.dtype != BF16],
        compiler_params=_params("parallel", "arbitrary"),
        name="matmul_residual",
    )(*xs, *[wv.arr for wv in wvs], res)


def _mm_res_norm_kernel(x_ref, w_ref, res_ref, g_ref, o_ref, row_ref):
    j = pl.program_id(1)
    nj, _, bn = row_ref.shape
    row_ref[j] = res_ref[...] + jnp.dot(x_ref[...], w_ref[...], preferred_element_type=F32)

    @pl.when(j == nj - 1)
    def _():
        parts = [row_ref[jb] for jb in range(nj)]
        ms = sum(jnp.sum(p * p, axis=-1, keepdims=True) for p in parts) / (nj * bn)
        inv = lax.rsqrt(ms + NORM_EPS)
        for jb in range(nj):
            o_ref[:, jb * bn:(jb + 1) * bn] = parts[jb] * inv * g_ref[:, jb * bn:(jb + 1) * bn]


def matmul_residual_rmsnorm(x, wv, res, gain, bm, bn):
    m, n = res.shape
    bm, bn = _blk(m, bm, 8), _blk(n, bn, LANES)
    assert wv.arr.dtype == BF16 and not wv.transposed
    return pl.pallas_call(
        _mm_res_norm_kernel,
        grid=(m // bm, n // bn),
        in_specs=[pl.BlockSpec((bm, x.shape[1]), lambda i, j: (i, 0)),
                  _w_spec(wv, bn, lambda i, j: j),
                  pl.BlockSpec((bm, bn), lambda i, j: (i, j)),
                  pl.BlockSpec((1, n), lambda i, j: (0, 0))],
        out_specs=pl.BlockSpec((bm, n), lambda i, j: (i, 0)),
        out_shape=jax.ShapeDtypeStruct((m, n), F32),
        scratch_shapes=[pltpu.VMEM((n // bn, bm, bn), F32)],
        compiler_params=_params("parallel", "arbitrary"),
        name="matmul_residual_rmsnorm",
    )(x, wv.arr, res, gain)


def _prefix_sum_rows(x, period):
    row = lax.broadcasted_iota(jnp.int32, x.shape, 0) % period
    shift = 1
    while shift < period:
        x = x + jnp.where(row >= shift, pltpu.roll(x, shift, axis=0), 0.0)
        shift *= 2
    return x


def _fox_gates_kernel(f_ref, b_ref, c_ref):
    z = f_ref[...] + b_ref[...]
    log_f = jnp.minimum(z, 0.0) - jnp.log(1.0 + jnp.exp(-jnp.abs(z)))
    c_ref[...] = _prefix_sum_rows(log_f, log_f.shape[0])


def fox_gates(f_logit, f_bias, batch, seq):
    return pl.pallas_call(
        _fox_gates_kernel,
        grid=(batch,),
        in_specs=[pl.BlockSpec((seq, LANES), lambda b: (b, 0)), pl.BlockSpec((1, LANES), lambda b: (0, 0))],
        out_specs=pl.BlockSpec((seq, LANES), lambda b: (b, 0)),
        out_shape=jax.ShapeDtypeStruct((batch * seq, LANES), F32),
        compiler_params=_params("parallel"),
        name="fox_gates",
    )(f_logit, f_bias)


def _fox_attn_kernel(q_ref, k_ref, vt_ref, cq_ref, ck_ref, o_ref, m_ref, l_ref, acc_ref, *, tq, hb, scale):
    i = pl.program_id(2)
    d = HEAD_DIM
    heads = range(hb)
    q = [q_ref[:, h * d:(h + 1) * d] for h in heads]
    cq = [cq_ref[0, h] for h in heads]
    for h in heads:
        m_ref[h] = jnp.full((1, tq), -jnp.inf, F32)
        l_ref[h] = jnp.zeros((1, tq), F32)
        acc_ref[h] = jnp.zeros((d, tq), F32)
    key_after_query = (lax.broadcasted_iota(jnp.int32, (tq, tq), 0) > lax.broadcasted_iota(jnp.int32, (tq, tq), 1))

    def block(j, on_diagonal):
        start = pl.multiple_of(j * tq, tq)
        s = [lax.dot_general(k_ref[pl.ds(start, tq), h * d:(h + 1) * d], q[h], (((1,), (1,)), ((), ())),
                             preferred_element_type=F32) for h in heads]
        ck = ck_ref[pl.ds(start, tq), :]
        x = [s[h] * scale + (cq[h] - ck[:, h:h + 1]) for h in heads]
        if on_diagonal:
            x = [jnp.where(key_after_query, -jnp.inf, x[h]) for h in heads]
        m_old = [m_ref[h] for h in heads]
        m_new = [jnp.maximum(m_old[h], jnp.max(x[h], axis=0, keepdims=True)) for h in heads]
        alpha = [jnp.exp(m_old[h] - m_new[h]) for h in heads]
        p = [jnp.exp(x[h] - m_new[h]) for h in heads]
        pv = [jnp.dot(vt_ref[0, j, h * d:(h + 1) * d, :], p[h].astype(BF16), preferred_element_type=F32)
              for h in heads]
        for h in heads:
            m_ref[h] = m_new[h]
            l_ref[h] = alpha[h] * l_ref[h] + jnp.sum(p[h], axis=0, keepdims=True)
            acc_ref[h] = alpha[h] * acc_ref[h] + pv[h]

    def body(j, carry):
        block(j, False)
        return carry

    lax.fori_loop(0, i, body, 0)
    block(i, True)
    for h in heads:
        o_ref[:, h * d:(h + 1) * d] = (acc_ref[h] / l_ref[h]).T.astype(o_ref.dtype)


def fox_attention(qkv, v_t, c_row, c, batch, seq, heads, tq, hb):
    nq = seq // tq
    d = HEAD_DIM
    groups = heads // hb
    assert groups == 1
    kernel = functools.partial(_fox_attn_kernel, tq=tq, hb=hb, scale=d ** -0.5)
    return pl.pallas_call(
        kernel,
        grid=(batch, groups, nq),
        in_specs=[pl.BlockSpec((tq, hb * d), lambda b, g, i: (b * nq + i, g)),
                  pl.BlockSpec((seq, hb * d), lambda b, g, i: (b, groups + g)),
                  pl.BlockSpec((1, nq, hb * d, tq), lambda b, g, i: (b, 0, g, 0)),
                  pl.BlockSpec((1, hb, 1, tq), lambda b, g, i: (b, g, 0, i)),
                  pl.BlockSpec((seq, LANES), lambda b, g, i: (b, 0))],
        out_specs=pl.BlockSpec((tq, hb * d), lambda b, g, i: (b * nq + i, g)),
        out_shape=jax.ShapeDtypeStruct((batch * seq, heads * d), BF16),
        scratch_shapes=[pltpu.VMEM((hb, 1, tq), F32), pltpu.VMEM((hb, 1, tq), F32), pltpu.VMEM((hb, d, tq), F32)],
        compiler_params=_params("parallel", "parallel", "arbitrary"),
        name="fox_attention",
    )(qkv, qkv, v_t, c_row, c)


def _conformer_kernel(val_ref, gate_ref, w_ref, cb_ref, lg_ref, lb_ref, o_ref, hist_ref, shift_ref, y_ref,
                      *, ts, rows):
    channels = val_ref.shape[1]

    @pl.when(pl.program_id(1) == 0)
    def _():
        hist_ref[0:CONV_HALO, :] = jnp.zeros((CONV_HALO, channels), F32)

    hist_ref[CONV_HALO:CONV_HALO + ts, :] = val_ref[...].astype(F32) * _sigmoid(gate_ref[...].astype(F32))

    base = CONV_HALO - (CONV_WIDTH - 1)
    hist_rows = ts + CONV_HALO
    for cb in range(channels // LANES):
        lanes = slice(cb * LANES, (cb + 1) * LANES)
        col = hist_ref[:, lanes]
        for b in range(1, SUBLANES):
            shift_ref[b - 1, :, lanes] = pltpu.roll(col, hist_rows - b, axis=0)
        for rb in range(ts // rows):
            acc = jnp.zeros((rows, LANES), F32)
            for j in range(CONV_WIDTH):
                a, b = divmod(base + j, SUBLANES)
                r0 = rb * rows + a * SUBLANES
                src = hist_ref[r0:r0 + rows, lanes] if b == 0 else shift_ref[b - 1, r0:r0 + rows, lanes]
                acc = acc + w_ref[j:j + 1, lanes] * src
            y_ref[rb * rows:(rb + 1) * rows, lanes] = acc + cb_ref[:, lanes]

    hist_ref[0:CONV_HALO, :] = hist_ref[ts:ts + CONV_HALO, :]

    y = y_ref[...]
    mu = jnp.mean(y, axis=-1, keepdims=True)
    yc = y - mu
    var = jnp.mean(yc * yc, axis=-1, keepdims=True)
    yn = yc * lax.rsqrt(var + NORM_EPS) * lg_ref[...] + lb_ref[...]
    o_ref[...] = (yn * _sigmoid(yn)).astype(o_ref.dtype)


def conformer_conv(glu, glu_col0, conv_w, conv_b, ln_g, ln_b, batch, seq, ts):
    channels = conv_w.shape[1]
    assert glu_col0 % channels == 0
    cb = glu_col0 // channels
    ns = seq // ts
    kernel = functools.partial(_conformer_kernel, ts=ts, rows=64)
    vec = lambda: pl.BlockSpec((1, channels), lambda b, i: (0, 0))
    return pl.pallas_call(
        kernel,
        grid=(batch, ns),
        in_specs=[pl.BlockSpec((ts, channels), lambda b, i: (b * ns + i, cb)),
                  pl.BlockSpec((ts, channels), lambda b, i: (b * ns + i, cb + 1)),
                  pl.BlockSpec((conv_w.shape[0], channels), lambda b, i: (0, 0)),
                  vec(), vec(), vec()],
        out_specs=pl.BlockSpec((ts, channels), lambda b, i: (b * ns + i, 0)),
        out_shape=jax.ShapeDtypeStruct((batch * seq, channels), BF16),
        scratch_shapes=[pltpu.VMEM((ts + CONV_HALO, channels), F32),
                        pltpu.VMEM((SUBLANES - 1, ts + CONV_HALO, channels), F32),
                        pltpu.VMEM((ts, channels), F32)],
        compiler_params=_params("arbitrary", "arbitrary"),
        name="conformer_conv",
    )(glu, glu, conv_w, conv_b, ln_g, ln_b)


def _gdn_in_kernel(x_ref, g_ref, w_ref, cw_ref, wl_ref, o_ref, logit_ref, xn_ref, hist_ref, halo_ref,
                   *, bm, blocks_per_seq, qk_blocks, conv_blocks):
    i = pl.program_id(0)
    j = pl.program_id(1)

    @pl.when(jnp.logical_and(i == 0, j == 0))
    def _():
        halo_ref[...] = jnp.zeros(halo_ref.shape, F32)

    @pl.when(j == 0)
    def _():
        _rms_to_scratch(x_ref, g_ref, xn_ref)
        logit_ref[...] = _dot_w(xn_ref[...], wl_ref[...], True)

    @pl.when(j >= conv_blocks)
    def _():
        o_ref[...] = _dot_w(xn_ref[...], w_ref[...], True).astype(o_ref.dtype)

    conv_args = dict(bm=bm, blocks_per_seq=blocks_per_seq, qk_blocks=qk_blocks)

    @pl.when(j < 2 * qk_blocks)
    def _():
        _gdn_conv_block(i, j, w_ref, cw_ref, o_ref, xn_ref, hist_ref, halo_ref, normalize=True, **conv_args)

    @pl.when(jnp.logical_and(j >= 2 * qk_blocks, j < conv_blocks))
    def _():
        _gdn_conv_block(i, j, w_ref, cw_ref, o_ref, xn_ref, hist_ref, halo_ref, normalize=False, **conv_args)


def _gdn_conv_block(i, j, w_ref, cw_ref, o_ref, xn_ref, hist_ref, halo_ref, *, bm, blocks_per_seq, qk_blocks,
                    normalize):
    bn = o_ref.shape[1]
    starts_sequence = (i % blocks_per_seq) == 0
    hist_ref[0:SHORT_HALO, :] = jnp.where(starts_sequence, 0.0, halo_ref[j])

    out_scale = jnp.where(j < qk_blocks, HEAD_DIM ** -0.5, 1.0)
    base = SHORT_HALO - (SHORT_CONV - 1)
    conv_rows = _blk(bm, 128, SUBLANES)

    def project(s):
        cols = slice(s * MXU_TILE, (s + 1) * MXU_TILE)
        hist_ref[SHORT_HALO:SHORT_HALO + bm, cols] = _dot_w(xn_ref[...], w_ref[cols, :], True)

    def conv_slab(s):
        for hb in range(s * MXU_TILE // HEAD_DIM, (s + 1) * MXU_TILE // HEAD_DIM):
            lanes = slice(hb * HEAD_DIM, (hb + 1) * HEAD_DIM)
            for r0 in range(0, bm, conv_rows):
                col = hist_ref[r0:r0 + conv_rows + SHORT_HALO, lanes]
                acc = jnp.zeros((conv_rows, HEAD_DIM), F32)
                for tap in range(SHORT_CONV):
                    ofs = base + tap
                    src = (col[ofs:ofs + conv_rows] if ofs % SUBLANES == 0
                           else pltpu.roll(col, conv_rows + SHORT_HALO - ofs, axis=0)[0:conv_rows])
                    acc = acc + cw_ref[tap:tap + 1, lanes] * src
                y = acc * _sigmoid(acc)
                if normalize:
                    y = y * (lax.rsqrt(jnp.sum(y * y, axis=-1, keepdims=True) + NORM_EPS) * out_scale)
                o_ref[r0:r0 + conv_rows, lanes] = y.astype(o_ref.dtype)

    n_slabs = bn // MXU_TILE
    project(0)
    for s in range(n_slabs):
        if s + 1 < n_slabs:
            project(s + 1)
        conv_slab(s)
    halo_ref[j] = hist_ref[bm:bm + SHORT_HALO, :]


def gdn_in_proj(x, gain, wv, short_conv, w_logit_t, seq, qk_width, bm, bn):
    m, k = x.shape
    n = wv.cols
    conv_width = short_conv.shape[1]
    bm, bn = _blk(seq, bm, 8), _blk(qk_width, bn, LANES)
    assert wv.transposed and seq % bm == 0 and n % bn == 0 and conv_width % bn == 0
    conv_blocks = conv_width // bn
    kernel = functools.partial(_gdn_in_kernel, bm=bm, blocks_per_seq=seq // bm, qk_blocks=qk_width // bn,
                               conv_blocks=conv_blocks)
    return pl.pallas_call(
        kernel,
        grid=(m // bm, n // bn),
        in_specs=[pl.BlockSpec((bm, k), lambda i, j: (i, 0)),
                  pl.BlockSpec((1, k), lambda i, j: (0, 0)),
                  _w_spec(wv, bn, lambda i, j: j),
                  pl.BlockSpec((short_conv.shape[0], bn), lambda i, j: (0, jnp.minimum(j, conv_blocks - 1))),
                  pl.BlockSpec((LANES, k), lambda i, j: (0, 0))],
        out_specs=[pl.BlockSpec((bm, bn), lambda i, j: (i, j)), pl.BlockSpec((bm, LANES), lambda i, j: (i, 0))],
        out_shape=[jax.ShapeDtypeStruct((m, n), BF16), jax.ShapeDtypeStruct((m, LANES), F32)],
        scratch_shapes=[pltpu.VMEM((bm, k), BF16), pltpu.VMEM((bm + SHORT_HALO, bn), F32),
                        pltpu.VMEM((conv_blocks, SHORT_HALO, bn), F32)],
        compiler_params=_params("arbitrary", "arbitrary"),
        name="gdn_in_proj",
    )(x, gain, wv.arr, short_conv, w_logit_t)


def _gdn_gates_kernel(ba_ref, alog_ref, dtb_ref, g_ref, *, gate_heads):
    ba = ba_ref[...]
    beta = _sigmoid(ba)
    g = -jnp.exp(alog_ref[...]) * _softplus(ba + dtb_ref[...])
    gcum = _prefix_sum_rows(g, CHUNK)
    lane = lax.broadcasted_iota(jnp.int32, ba.shape, 1)
    g_ref[...] = jnp.where(lane < gate_heads, beta, gcum)


def gdn_gates(ba, alog_pad, dtb_pad, gate_heads, ts):
    m = ba.shape[0]
    ts = _blk(m, ts, CHUNK)
    row = lambda i: (i, 0)
    const = lambda i: (0, 0)
    return pl.pallas_call(
        functools.partial(_gdn_gates_kernel, gate_heads=gate_heads),
        grid=(m // ts,),
        in_specs=[pl.BlockSpec((ts, LANES), row), pl.BlockSpec((1, LANES), const), pl.BlockSpec((1, LANES), const)],
        out_specs=pl.BlockSpec((ts, LANES), row),
        out_shape=jax.ShapeDtypeStruct((m, LANES), F32),
        compiler_params=_params("parallel"),
        name="gdn_gates",
    )(ba, alog_pad, dtb_pad)


GROUP = MXU_TILE // CHUNK


def _gdn_solve_kernel(q_ref, k_ref, g_ref, gr_ref, bdm_ref, kkm_ref, t_ref, p_ref, *, v_heads, rep):
    c = CHUNK
    gw = MXU_TILE
    n_groups = v_heads // GROUP
    n_chunks = g_ref.shape[0] // c
    row = lax.broadcasted_iota(jnp.int32, (c, gw), 0)
    lane = lax.broadcasted_iota(jnp.int32, (c, gw), 1)
    blk = lane // c
    col = lane % c
    tril = row >= col
    strict = row > col
    eye = (row == col).astype(F32)
    bd_mask = bdm_ref[...]
    kk_mask = kkm_ref[...]

    def block_diag(x):
        xb = x.astype(BF16)
        return jnp.concatenate([xb] * GROUP, axis=0) * bd_mask

    gates = [g_ref[n * c:(n + 1) * c, :] for n in range(n_chunks)]

    def per_head_columns(first_lane, u):
        n, g = units[u]
        out = gates[n][:, first_lane + GROUP * g + GROUP - 1:first_lane + GROUP * g + GROUP]
        for mth in range(GROUP - 2, -1, -1):
            cm = gates[n][:, first_lane + GROUP * g + mth:first_lane + GROUP * g + mth + 1]
            out = jnp.where(blk == mth, cm, out)
        return out

    units = [(n, g) for n in range(n_chunks) for g in range(n_groups)]
    groups = range(len(units))
    rows = [slice(n * c, (n + 1) * c) for n, _ in units]
    lanes = [slice(g * gw, (g + 1) * gw) for _, g in units]
    kq = [jnp.concatenate([k_ref[rows[u], lanes[u]], q_ref[rows[u], lanes[u]]], axis=0) for u in groups]
    rhs = [jnp.concatenate([k_ref[rows[u], lanes[u]]] * GROUP, axis=0) * kk_mask for u in groups]
    kkqk = [lax.dot_general(kq[g], rhs[g], (((1,), (1,)), ((), ())), preferred_element_type=F32) for g in groups]
    decay = [jnp.exp(jnp.where(tril, per_head_columns(v_heads, u) - gr_ref[units[u][0], :, lanes[u]], -jnp.inf))
             for u in groups]
    low = [jnp.where(strict, per_head_columns(0, g) * kkqk[g][:c] * decay[g], 0.0) for g in groups]
    for g in groups:
        p_ref[rows[g], lanes[g]] = (kkqk[g][c:] * decay[g]).astype(p_ref.dtype)

    inv = [eye - low[g] for g in groups]
    bd = [block_diag(low[g]) for g in groups]
    power = [jnp.dot(low[g].astype(BF16), bd[g], preferred_element_type=F32) for g in groups]
    span = 2
    while span <= c // 2:
        bd = [block_diag(power[g]) for g in groups]
        if span < c // 2:
            both = [jnp.dot(jnp.concatenate([power[g], inv[g]], axis=0).astype(BF16), bd[g],
                            preferred_element_type=F32) for g in groups]
            power = [both[g][:c] for g in groups]
            inv = [inv[g] + both[g][c:] for g in groups]
        else:
            inv = [inv[g] + jnp.dot(inv[g].astype(BF16), bd[g], preferred_element_type=F32) for g in groups]
        span *= 2
    for g in groups:
        t_ref[rows[g], lanes[g]] = inv[g].astype(t_ref.dtype)


def gdn_solve(qkv, gates, g_rows, v_heads, rep, chunks_per_step):
    m = qkv.shape[0]
    width = v_heads * CHUNK
    qk_width = v_heads * HEAD_DIM // rep
    assert GROUP * CHUNK == MXU_TILE and rep * HEAD_DIM == MXU_TILE and v_heads % GROUP == 0
    assert qk_width * rep == v_heads * HEAD_DIM and qk_width == width
    rb = lax.broadcasted_iota(jnp.int32, (GROUP * CHUNK, MXU_TILE), 0) // CHUNK
    ln = lax.broadcasted_iota(jnp.int32, (GROUP * CHUNK, MXU_TILE), 1)
    bd_mask = (rb == ln // CHUNK).astype(BF16)
    kk_mask = (rb // rep == ln // HEAD_DIM).astype(BF16)
    kernel = functools.partial(_gdn_solve_kernel, v_heads=v_heads, rep=rep)
    row = lambda i: (i, 0)
    const = lambda i: (0, 0)
    rows = chunks_per_step * CHUNK
    assert m % rows == 0
    return pl.pallas_call(
        kernel,
        grid=(m // rows,),
        in_specs=[pl.BlockSpec((rows, qk_width), row), pl.BlockSpec((rows, qk_width), lambda i: (i, 1)),
                  pl.BlockSpec((rows, LANES), row),
                  pl.BlockSpec((chunks_per_step, 1, width), lambda i: (i, 0, 0)),
                  pl.BlockSpec((GROUP * CHUNK, MXU_TILE), const), pl.BlockSpec((GROUP * CHUNK, MXU_TILE), const)],
        out_specs=[pl.BlockSpec((rows, width), row), pl.BlockSpec((rows, width), row)],
        out_shape=[jax.ShapeDtypeStruct((m, width), BF16), jax.ShapeDtypeStruct((m, width), BF16)],
        compiler_params=_params("parallel"),
        name="gdn_solve",
    )(qkv, qkv, gates, g_rows, bd_mask, kk_mask)


def _gdn_chunk_kernel(q_ref, k_ref, v_ref, z_ref, g_ref, t_ref, p_ref, on_ref, o_ref, state_ref, *, qk_heads, rep):
    c = CHUNK
    d = HEAD_DIM

    @pl.when(pl.program_id(1) == 0)
    def _():
        state_ref[...] = jnp.zeros(state_ref.shape, F32)

    n_v = qk_heads * rep
    pw = rep * d
    tw = rep * c
    gain = on_ref[...]
    zeros = jnp.zeros((c, d), BF16)
    pairs = range(qk_heads)

    def on_diagonal(parts):
        return jnp.concatenate(
            [jnp.concatenate([parts[r] if s == r else zeros for s in range(rep)], axis=1) for r in range(rep)],
            axis=0)

    def chunk_step(n, state):
        rows = slice(n * c, (n + 1) * c)
        gates = g_ref[rows, :]

        def head_cols(first_lane, p):
            return [gates[:, first_lane + p * rep + r:first_lane + p * rep + r + 1] for r in range(rep)]

        k = {p: k_ref[rows, p * d:(p + 1) * d] for p in pairs}
        kq = {p: jnp.concatenate([k[p], q_ref[rows, p * d:(p + 1) * d]], axis=0) for p in pairs}
        kqs = {p: jnp.dot(kq[p], state[p].astype(BF16), preferred_element_type=F32) for p in pairs}

        beta = {p: head_cols(0, p) for p in pairs}
        gc = {p: head_cols(n_v, p) for p in pairs}
        eg = {p: [jnp.exp(g) for g in gc[p]] for p in pairs}
        rhs = {p: on_diagonal([(beta[p][r] * (v_ref[rows, p * pw + r * d:p * pw + (r + 1) * d].astype(F32)
                                           - kqs[p][:c, r * d:(r + 1) * d] * eg[p][r])).astype(BF16)
                               for r in range(rep)]) for p in pairs}
        v_new = {p: jnp.dot(t_ref[rows, p * tw:(p + 1) * tw], rhs[p], preferred_element_type=F32)
                 for p in pairs}
        vn_diag = {p: on_diagonal([v_new[p][:, r * d:(r + 1) * d].astype(BF16) for r in range(rep)]) for p in pairs}
        intra = {p: jnp.dot(p_ref[rows, p * tw:(p + 1) * tw], vn_diag[p], preferred_element_type=F32)
                 for p in pairs}

        g_last = {p: [g[c - 1:c, :] for g in gc[p]] for p in pairs}
        v_dec = {p: jnp.concatenate([jnp.exp(g_last[p][r] - gc[p][r]) * v_new[p][:, r * d:(r + 1) * d]
                                     for r in range(rep)], axis=1).astype(BF16) for p in pairs}
        kt = {p: k[p].astype(F32).T.astype(BF16) for p in pairs}
        upd = {p: jnp.dot(kt[p], v_dec[p], preferred_element_type=F32) for p in pairs}
        scale = {p: jnp.concatenate([jnp.broadcast_to(jnp.exp(g_last[p][r]), (1, d)) for r in range(rep)], axis=1)
                 for p in pairs}
        new_state = {p: state[p] * scale[p] + upd[p] for p in pairs}

        for p in pairs:
            for r in range(rep):
                vl = slice(p * pw + r * d, p * pw + (r + 1) * d)
                o = kqs[p][c:, r * d:(r + 1) * d] * eg[p][r] + intra[p][:, r * d:(r + 1) * d]
                on = o * lax.rsqrt(jnp.mean(o * o, axis=-1, keepdims=True) + NORM_EPS) * gain
                zf = z_ref[rows, vl].astype(F32)
                o_ref[rows, vl] = (on * (zf * _sigmoid(zf))).astype(o_ref.dtype)
        return new_state

    state = {p: state_ref[p] for p in pairs}
    for n in range(g_ref.shape[0] // c):
        state = chunk_step(n, state)
    for p in pairs:
        state_ref[p] = state[p]


def gdn_chunk(qkvz, gates, t_inv, p_intra, o_norm, batch, seq, qk_heads, rep, chunks_per_step):
    m = batch * seq
    qk_width = qk_heads * HEAD_DIM
    v_width = qk_width * rep
    assert v_width == 2 * qk_width
    kernel = functools.partial(_gdn_chunk_kernel, qk_heads=qk_heads, rep=rep)
    rows = chunks_per_step * CHUNK
    assert seq % rows == 0
    nc = seq // rows
    row = lambda b, i: (b * nc + i, 0)
    col1 = lambda b, i: (b * nc + i, 1)
    col2 = lambda b, i: (b * nc + i, 2)
    return pl.pallas_call(
        kernel,
        grid=(batch, nc),
        in_specs=[pl.BlockSpec((rows, qk_width), row), pl.BlockSpec((rows, qk_width), col1),
                  pl.BlockSpec((rows, v_width), col1), pl.BlockSpec((rows, v_width), col2),
                  pl.BlockSpec((rows, LANES), row),
                  pl.BlockSpec((rows, t_inv.shape[1]), row), pl.BlockSpec((rows, p_intra.shape[1]), row),
                  pl.BlockSpec((1, HEAD_DIM), lambda b, i: (0, 0))],
        out_specs=pl.BlockSpec((rows, v_width), row),
        out_shape=jax.ShapeDtypeStruct((m, v_width), BF16),
        scratch_shapes=[pltpu.VMEM((qk_heads, HEAD_DIM, rep * HEAD_DIM), F32)],
        compiler_params=_params("arbitrary", "arbitrary"),
        name="gdn_chunk",
    )(qkvz, qkvz, qkvz, qkvz, gates, t_inv, p_intra, o_norm)


def _pad_cols(w, n):
    return jnp.pad(w, ((0, 0), (0, n - w.shape[1])))


def _row(v):
    return v.reshape(1, -1).astype(F32)


def kernel(x, norm_mix0, w_in0, fgate_bias0, conv_w0, conv_b0, conv_ln_g0, conv_ln_b0, w_out0, norm_mix1, w_in1,
           short_conv1, a_log1, dt_bias1, o_norm1, w_out1, norm_ffn, w_gate, w_up, w_down, final_norm):
    batch, seq, d_model = x.shape
    m = batch * seq
    fox_heads = fgate_bias0.shape[1]
    fox_width = fox_heads * HEAD_DIM
    conv_ch = conv_w0.shape[2]
    gdn_v_heads = a_log1.shape[1]
    gdn_v_width = gdn_v_heads * HEAD_DIM
    gdn_qkv_width = short_conv1.shape[2]
    gdn_qk_width = (gdn_qkv_width - gdn_v_width) // 2
    gdn_qk_heads = gdn_qk_width // HEAD_DIM
    rep = gdn_v_heads // gdn_qk_heads

    h = x.reshape(m, d_model)

    w_gate_b, w_up_b, w_down_b = w_gate, w_up, w_down.astype(BF16)
    w_in0_t = jnp.swapaxes(w_in0, 1, 2).astype(BF16)
    w_in1_t = jnp.swapaxes(w_in1, 1, 2).astype(BF16)

    def pad_rows(w, n):
        return jnp.pad(w, ((0, n - w.shape[0]), (0, 0)))

    def ffn(h, layer, final_gain=None):
        mid = norm_swiglu(h, _row(norm_ffn[layer]), wview(w_gate_b, layer), wview(w_up_b, layer), *TILE_SWIGLU)
        if final_gain is not None:
            return matmul_residual_rmsnorm(mid, wview(w_down_b, layer), h, final_gain, *TILE_DOWN_FINAL)
        return matmul_residual([mid], [wview(w_down_b, layer)], h, *TILE_DOWN)

    g0 = _row(norm_mix0[0])
    f0, f1 = 3 * fox_width, 3 * fox_width + fox_heads
    w_qkvg = jnp.concatenate([w_in0_t[0, :f0], w_in0_t[0, f1:]], axis=0)
    qkv, f_logit = norm_matmul(h, g0, wview(w_qkvg, transposed=True), pad_rows(w_in0_t[0, f0:f1], LANES),
                               *TILE_PROJ)

    tq = ATTN_BLOCK
    c = fox_gates(f_logit, _pad_cols(_row(fgate_bias0[0]), LANES), batch, seq)
    c_heads = jnp.transpose(c.reshape(batch, seq, LANES)[:, :, :fox_heads], (0, 2, 1))
    v_t = jnp.transpose(qkv[:, 2 * fox_width:f0].reshape(batch, seq // tq, tq, fox_width), (0, 1, 3, 2))
    a_out = fox_attention(qkv, v_t, c_heads.reshape(batch, fox_heads, 1, seq), c, batch, seq, fox_heads, tq,
                          hb=min(ATTN_HEADS_PER_STEP, fox_heads))
    u = conformer_conv(qkv, f0, jnp.pad(conv_w0[0], ((0, CONV_HALO - CONV_WIDTH), (0, 0))), _row(conv_b0[0]),
                       _row(conv_ln_g0[0]), _row(conv_ln_b0[0]), batch, seq, ts=CONFORMER_ROWS)
    h = matmul_residual([a_out, u], [wview(w_out0, 0, row0=0, rows=fox_width),
                                     wview(w_out0, 0, row0=fox_width, rows=conv_ch)], h, *TILE_OUT_FOX)
    h = ffn(h, 0)

    g1 = _row(norm_mix1[0])
    z0 = gdn_qkv_width
    qkvz1, ba1 = gdn_in_proj(h, g1, wview(w_in1_t, 0, cols=z0 + gdn_v_width, transposed=True),
                             jnp.pad(short_conv1[0], ((0, SUBLANES - SHORT_CONV), (0, 0))),
                             pad_rows(w_in1_t[0, z0 + gdn_v_width:], LANES), seq, gdn_qk_width, *TILE_PROJ)
    alog_pad = jnp.pad(_row(a_log1[0]), ((0, 0), (gdn_v_heads, LANES - 2 * gdn_v_heads)))
    dtb_pad = jnp.pad(_row(dt_bias1[0]), ((0, 0), (gdn_v_heads, LANES - 2 * gdn_v_heads)))
    gates1 = gdn_gates(ba1, alog_pad, dtb_pad, gdn_v_heads, ts=GATES_ROWS)
    g_rows = jnp.transpose(gates1[:, gdn_v_heads:2 * gdn_v_heads].reshape(m // CHUNK, CHUNK, gdn_v_heads),
                           (0, 2, 1)).reshape(m // CHUNK, 1, gdn_v_heads * CHUNK)
    t_inv, p_intra = gdn_solve(qkvz1, gates1, g_rows, gdn_v_heads, rep, chunks_per_step=SOLVE_CHUNKS_PER_STEP)
    o1 = gdn_chunk(qkvz1, gates1, t_inv, p_intra, _row(o_norm1[0]), batch, seq, gdn_qk_heads, rep,
                   chunks_per_step=SCAN_CHUNKS_PER_STEP)
    h = matmul_residual([o1], [wview(w_out1.astype(BF16), 0)], h, *TILE_OUT_GDN)
    return ffn(h, 1, final_gain=_row(final_norm)).reshape(batch, seq, d_model)
```
